```python
import jax, jax.numpy as jnp
from jax import lax
import numpy as np

D_MODEL = 1024
BATCH = 32
SEQ = 2048
DEPTH = 2

RMS_EPS = 1e-6
GLA_HEADS = 4
GLA_KEY_DIM = 64
GLA_VALUE_DIM = 128
GLA_GATE_RANK = 16
GLA_GATE_NORMALIZER = 16.0
GLA_CHUNK = 64
LRU_WIDTH = 512
LRU_BLOCKS = 8
LRU_CONV_WIDTH = 4
LRU_C = 8.0
DIL_PATTERNS = ((128, 1), (512, 4), (2048, 16))
N_PATTERNS = 3
DIL_HEADS = 4
DIL_HEAD_DIM = 128
ATTN_BLOCK = 128
ROPE_THETA = 500000.0
ROPE_DIMS = DIL_HEAD_DIM // 4
N_BRANCHES = 3
BRANCH_WIDTH = 512
N_EXPERTS = 16
N_GROUPS = 4
EXPERTS_PER_GROUP = N_EXPERTS // N_GROUPS
TOPK_GROUPS = 1
TOP_K = 2
EXPERT_FF = 512
MOE_BLOCK = 256

IN_SPLITS = (GLA_HEADS * GLA_KEY_DIM, GLA_HEADS * GLA_KEY_DIM, GLA_HEADS * GLA_VALUE_DIM,
             GLA_HEADS * GLA_VALUE_DIM, GLA_GATE_RANK, LRU_WIDTH, LRU_WIDTH,
             3 * N_PATTERNS * DIL_HEADS * DIL_HEAD_DIM, N_BRANCHES * D_MODEL)
IN_COLS = sum(IN_SPLITS)

kernel_name = 'hybrid_gla_rglru_dilated_moe'


def rmsnorm(x, gain):
    xf = x.astype(jnp.float32)
    y = xf * lax.rsqrt(jnp.mean(xf * xf, axis=-1, keepdims=True) + RMS_EPS)
    return (y * gain.astype(jnp.float32)).astype(x.dtype)


def gla_branch(q, k, v, og, g_low, w_gate_up, b_gate, out_gain):
    B, S, _ = q.shape
    H, DK, DV, C = GLA_HEADS, GLA_KEY_DIM, GLA_VALUE_DIM, GLA_CHUNK
    N = S // C
    f32 = jnp.float32
    log_alpha = jax.nn.log_sigmoid(jnp.einsum('bsr,rk->bsk', g_low, w_gate_up).astype(f32)
                                   + b_gate.astype(f32)) / GLA_GATE_NORMALIZER

    def to_chunks(t, d):
        return t.astype(f32).reshape(B, N, C, H, d).transpose(0, 3, 1, 2, 4)

    qc = to_chunks(q, DK) * (DK ** -0.5)
    kc = to_chunks(k, DK)
    vc = to_chunks(v, DV)
    cum = jnp.cumsum(to_chunks(log_alpha, DK), axis=3)
    q_dec = qc * jnp.exp(cum)
    k_inv = kc * jnp.exp(-cum)
    causal = jnp.tril(jnp.ones((C, C), dtype=bool))
    scores = jnp.where(causal, jnp.einsum('bhnid,bhnjd->bhnij', q_dec, k_inv), 0.0)
    o_intra = jnp.einsum('bhnij,bhnjv->bhniv', scores, vc)
    total = cum[:, :, :, -1]
    kv_chunk = jnp.einsum('bhncd,bhncv->bhndv', kc * jnp.exp(total[:, :, :, None] - cum), vc)

    def step(state, inp):
        decay, kv = inp
        return state * jnp.exp(decay)[..., None] + kv, state

    init = jnp.zeros((B, H, DK, DV), f32)
    _, prev = lax.scan(step, init, (jnp.moveaxis(total, 2, 0), jnp.moveaxis(kv_chunk, 2, 0)))
    prev = jnp.moveaxis(prev, 0, 2)
    o = o_intra + jnp.einsum('bhncd,bhndv->bhncv', q_dec, prev)
    o = o.transpose(0, 2, 3, 1, 4).reshape(B, S, H, DV).astype(v.dtype)
    o = rmsnorm(o, out_gain).reshape(B, S, H * DV)
    return o * jax.nn.silu(og)


def rglru_branch(xb, yb, conv_w, conv_b, w_a, b_a, w_x, b_x, lam):
    B, S, W = xb.shape
    f32 = jnp.float32
    xc = lax.conv_general_dilated(xb, conv_w[:, None, :], window_strides=(1,),
                                  padding=((LRU_CONV_WIDTH - 1, 0),),
                                  dimension_numbers=('NWC', 'WIO', 'NWC'),
                                  feature_group_count=W) + conv_b
    xg = xc.reshape(B, S, LRU_BLOCKS, W // LRU_BLOCKS)
    r = jax.nn.sigmoid(jnp.einsum('bsnc,ncd->bsnd', xg, w_a).reshape(B, S, W).astype(f32) + b_a.astype(f32))
    i = jax.nn.sigmoid(jnp.einsum('bsnc,ncd->bsnd', xg, w_x).reshape(B, S, W).astype(f32) + b_x.astype(f32))
    log_a = -LRU_C * r * jax.nn.softplus(-lam.astype(f32))
    a = jnp.exp(log_a)
    u = jnp.sqrt(-jnp.expm1(2.0 * log_a)) * i * xc.astype(f32)

    def combine(left, right):
        a_l, h_l = left
        a_r, h_r = right
        return a_l * a_r, a_r * h_l + h_r

    _, h = lax.associative_scan(combine, (a, u), axis=1)
    return (h * jax.nn.gelu(yb.astype(f32))).astype(xb.dtype)


def partial_rope(t, positions):
    half = ROPE_DIMS // 2
    f32 = jnp.float32
    inv_freq = ROPE_THETA ** (-jnp.arange(half, dtype=f32) / half)
    ang = positions.astype(f32)[:, :, None] * inv_freq
    cos = jnp.cos(ang)[:, :, None, None, :]
    sin = jnp.sin(ang)[:, :, None, None, :]
    tf = t.astype(f32)
    x1 = tf[..., :half]
    x2 = tf[..., half:ROPE_DIMS]
    return jnp.concatenate([x1 * cos - x2 * sin, x2 * cos + x1 * sin, tf[..., ROPE_DIMS:]],
                           axis=-1).astype(t.dtype)


def banded_causal_attention(q, k, v, span):
    Bp, L, H, Dh = q.shape
    BLK = ATTN_BLOCK
    nb = -(-L // BLK)
    pad = nb * BLK - L
    f32 = jnp.float32

    def blocks(t):
        return jnp.pad(t.astype(f32), ((0, 0), (0, pad), (0, 0), (0, 0))).reshape(Bp, nb, BLK, H, Dh)

    def with_prev(t):
        prev = jnp.pad(t, ((0, 0), (1, 0), (0, 0), (0, 0), (0, 0)))[:, :-1]
        return jnp.concatenate([prev, t], axis=2)

    qb = blocks(q)
    kk = with_prev(blocks(k))
    vv = with_prev(blocks(v))
    s = jnp.einsum('bnqhd,bnkhd->bnhqk', qb, kk) * (Dh ** -0.5)
    qi = jnp.arange(BLK)[:, None]
    kj = jnp.arange(2 * BLK)[None, :]
    dist = qi + BLK - kj
    band = (dist >= 0) & (dist <= span)
    first = jnp.arange(nb)[:, None, None] == 0
    valid = band[None] & ~(first & (kj < BLK)[None])
    s = jnp.where(valid[None, :, None], s, -jnp.inf)
    m = jnp.max(s, axis=-1, keepdims=True)
    p = jnp.exp(s - m)
    denom = jnp.sum(p, axis=-1)
    o = jnp.einsum('bnhqk,bnkhd->bnqhd', p, vv) / jnp.transpose(denom, (0, 1, 3, 2))[..., None]
    lse = jnp.transpose(m[..., 0] + jnp.log(denom), (0, 1, 3, 2))
    return o.reshape(Bp, nb * BLK, H, Dh)[:, :L], lse.reshape(Bp, nb * BLK, H)[:, :L]


def stride_gather(t, dil):
    B, S, H, Dh = t.shape
    return t.reshape(B, S // dil, dil, H, Dh).transpose(0, 2, 1, 3, 4).reshape(B * dil, S // dil, H, Dh)


def stride_scatter(t, B, dil):
    L = t.shape[1]
    rest = t.shape[2:]
    t = t.reshape((B, dil, L) + rest)
    t = jnp.moveaxis(t, 1, 2)
    return t.reshape((B, L * dil) + rest)


def dilated_branch(q, k, v, positions, q_gain, k_gain):
    B, S, P, H, Dh = q.shape
    q = partial_rope(rmsnorm(q, q_gain), positions)
    k = partial_rope(rmsnorm(k, k_gain), positions)
    outs, lses = [], []
    for p, (window, dil) in enumerate(DIL_PATTERNS):
        o, lse = banded_causal_attention(stride_gather(q[:, :, p], dil), stride_gather(k[:, :, p], dil),
                                         stride_gather(v[:, :, p], dil), window // dil)
        outs.append(stride_scatter(o, B, dil))
        lses.append(stride_scatter(lse, B, dil))
    w = jax.nn.softmax(jnp.stack(lses, axis=-1), axis=-1)
    o = jnp.einsum('bshpd,bshp->bshd', jnp.stack(outs, axis=3), w)
    return o.reshape(B, S, H * Dh).astype(v.dtype)


def mixer_block(h, positions, w_in, gla_gate_up, gla_gate_bias, gla_out_gain, lru_conv_w, lru_conv_b,
                lru_w_a, lru_b_a, lru_w_x, lru_b_x, lru_lambda, q_norm_gain, k_norm_gain, w_branch, w_out):
    B, S, D = h.shape
    proj = jnp.einsum('bsd,de->bse', h, w_in)
    cuts = np.cumsum(IN_SPLITS)[:-1].tolist()
    q_a, k_a, v_a, og_a, glow_a, x_b, y_b, qkv_c, gate_logits = jnp.split(proj, cuts, axis=-1)
    br_a = gla_branch(q_a, k_a, v_a, og_a, glow_a, gla_gate_up, gla_gate_bias, gla_out_gain)
    br_b = rglru_branch(x_b, y_b, lru_conv_w, lru_conv_b, lru_w_a, lru_b_a, lru_w_x, lru_b_x, lru_lambda)
    qkv = qkv_c.reshape(B, S, 3, N_PATTERNS, DIL_HEADS, DIL_HEAD_DIM)
    br_c = dilated_branch(qkv[:, :, 0], qkv[:, :, 1], qkv[:, :, 2], positions, q_norm_gain, k_norm_gain)
    gates = jax.nn.sigmoid(gate_logits.astype(jnp.float32)).reshape(B, S, N_BRANCHES, D)
    merged = sum(gates[:, :, n] * jnp.einsum('bsw,wd->bsd', br, w_branch[n]).astype(jnp.float32)
                 for n, br in enumerate((br_a, br_b, br_c)))
    return jnp.einsum('bsd,de->bse', merged.astype(h.dtype), w_out)


def moe_ffn(h, router_w, router_b, w_gate, w_up, w_down):
    B, S, D = h.shape
    T = B * S
    TK = T * TOP_K
    xt = h.reshape(T, D)
    logits = jnp.einsum('td,de->te', xt, router_w).astype(jnp.float32) + router_b.astype(jnp.float32)
    scores = jax.nn.softmax(logits, axis=-1)
    group_score = lax.top_k(scores.reshape(T, N_GROUPS, EXPERTS_PER_GROUP), TOP_K)[0].sum(-1)
    _, gsel = lax.top_k(group_score, TOPK_GROUPS)
    gmask = jnp.any(gsel[..., None] == jnp.arange(N_GROUPS), axis=-2)
    emask = jnp.repeat(gmask, EXPERTS_PER_GROUP, axis=-1)
    topw, topi = lax.top_k(jnp.where(emask, scores, -jnp.inf), TOP_K)
    topw = topw / jnp.sum(topw, axis=-1, keepdims=True)
    flat_e = topi.reshape(TK)
    order = jnp.argsort(flat_e)
    sorted_e = flat_e[order]
    tok = order // TOP_K
    counts = jnp.bincount(flat_e, length=N_EXPERTS)
    starts = jnp.cumsum(counts) - counts
    padded = (counts + MOE_BLOCK - 1) // MOE_BLOCK * MOE_BLOCK
    pad_ends = jnp.cumsum(padded)
    pad_starts = pad_ends - padded
    dest = pad_starts[sorted_e] + jnp.arange(TK) - starts[sorted_e]
    n_rows = TK + N_EXPERTS * MOE_BLOCK
    n_blocks = n_rows // MOE_BLOCK
    xs = jnp.zeros((n_rows, D), h.dtype).at[dest].set(xt[tok])
    block_e = jnp.minimum(jnp.searchsorted(pad_ends, jnp.arange(n_blocks) * MOE_BLOCK, side='right'),
                          N_EXPERTS - 1)

    def expert_rows(args):
        xb, e = args
        return (jax.nn.silu(xb @ w_gate[e]) * (xb @ w_up[e])) @ w_down[e]

    ys = lax.map(expert_rows, (xs.reshape(n_blocks, MOE_BLOCK, D), block_e)).reshape(n_rows, D)
    contrib = ys[dest] * topw.reshape(TK)[order][:, None].astype(ys.dtype)
    return jax.ops.segment_sum(contrib, tok, num_segments=T).reshape(B, S, D)


def setup_inputs(seed: int = 0) -> dict:
    key = jax.random.key(seed)
    ks = jax.random.split(key, 26)
    f32 = jnp.float32
    nrm = lambda k, shape, scale: jax.random.normal(k, shape, f32) * scale
    W = LRU_WIDTH
    u = jax.random.uniform(ks[13], (DEPTH, W), f32, minval=0.9, maxval=0.999) ** (1.0 / LRU_C)
    offsets = jax.random.randint(ks[1], (BATCH, 1), 0, 4096, dtype=jnp.int32)
    res_scale = (2 * DEPTH) ** -0.5
    return {
        'x': nrm(ks[0], (BATCH, SEQ, D_MODEL), 1.0),
        'positions': offsets + jnp.arange(SEQ, dtype=jnp.int32)[None, :],
        'norm1_gain': 1.0 + nrm(ks[2], (DEPTH, D_MODEL), 0.02),
        'w_in': nrm(ks[3], (DEPTH, D_MODEL, IN_COLS), D_MODEL ** -0.5),
        'gla_gate_up': nrm(ks[4], (DEPTH, GLA_GATE_RANK, GLA_HEADS * GLA_KEY_DIM), GLA_GATE_RANK ** -0.5),
        'gla_gate_bias': nrm(ks[5], (DEPTH, GLA_HEADS * GLA_KEY_DIM), 0.1),
        'gla_out_gain': 1.0 + nrm(ks[6], (DEPTH, GLA_VALUE_DIM), 0.02),
        'lru_conv_w': nrm(ks[7], (DEPTH, LRU_CONV_WIDTH, W), LRU_CONV_WIDTH ** -0.5),
        'lru_conv_b': nrm(ks[8], (DEPTH, W), 0.02),
        'lru_w_a': nrm(ks[9], (DEPTH, LRU_BLOCKS, W // LRU_BLOCKS, W // LRU_BLOCKS), (W // LRU_BLOCKS) ** -0.5),
        'lru_b_a': nrm(ks[10], (DEPTH, W), 0.02),
        'lru_w_x': nrm(ks[11], (DEPTH, LRU_BLOCKS, W // LRU_BLOCKS, W // LRU_BLOCKS), (W // LRU_BLOCKS) ** -0.5),
        'lru_b_x': nrm(ks[12], (DEPTH, W), 0.02),
        'lru_lambda': jnp.log(u) - jnp.log1p(-u),
        'q_norm_gain': 1.0 + nrm(ks[14], (DEPTH, DIL_HEAD_DIM), 0.02),
        'k_norm_gain': 1.0 + nrm(ks[15], (DEPTH, DIL_HEAD_DIM), 0.02),
        'w_branch': nrm(ks[16], (DEPTH, N_BRANCHES, BRANCH_WIDTH, D_MODEL), BRANCH_WIDTH ** -0.5),
        'w_out': nrm(ks[17], (DEPTH, D_MODEL, D_MODEL), D_MODEL ** -0.5 * res_scale),
        'norm2_gain': 1.0 + nrm(ks[18], (DEPTH, D_MODEL), 0.02),
        'router_w': nrm(ks[19], (D_MODEL, N_EXPERTS), D_MODEL ** -0.5),
        'router_b': nrm(ks[20], (N_EXPERTS,), 0.01),
        'w_gate': nrm(ks[21], (DEPTH, N_EXPERTS, D_MODEL, EXPERT_FF), D_MODEL ** -0.5),
        'w_up': nrm(ks[22], (DEPTH, N_EXPERTS, D_MODEL, EXPERT_FF), D_MODEL ** -0.5),
        'w_down': nrm(ks[23], (DEPTH, N_EXPERTS, EXPERT_FF, D_MODEL), EXPERT_FF ** -0.5 * res_scale),
    }


def reference(x, positions, norm1_gain, w_in, gla_gate_up, gla_gate_bias, gla_out_gain, lru_conv_w,
              lru_conv_b, lru_w_a, lru_b_a, lru_w_x, lru_b_x, lru_lambda, q_norm_gain, k_norm_gain,
              w_branch, w_out, norm2_gain, router_w, router_b, w_gate, w_up, w_down):
    for l in range(DEPTH):
        h = rmsnorm(x, norm1_gain[l])
        x = x + mixer_block(h, positions, w_in[l], gla_gate_up[l], gla_gate_bias[l], gla_out_gain[l],
                            lru_conv_w[l], lru_conv_b[l], lru_w_a[l], lru_b_a[l], lru_w_x[l], lru_b_x[l],
                            lru_lambda[l], q_norm_gain[l], k_norm_gain[l], w_branch[l], w_out[l])
        h = rmsnorm(x, norm2_gain[l])
        x = x + moe_ffn(h, router_w, router_b, w_gate[l], w_up[l], w_down[l])
    return x
```

```python
import functools

import jax
import jax.numpy as jnp
import numpy as np
from jax import lax
from jax.experimental import pallas as pl
from jax.experimental.pallas import tpu as pltpu

f32 = jnp.float32
bf16 = jnp.bfloat16
i32 = jnp.int32

D_MODEL = 1024
SEQ = 2048
RMS_EPS = 1e-6
GLA_HEADS, GLA_DK, GLA_DV, GLA_RANK, GLA_NORM, GLA_CHUNK = 4, 64, 128, 16, 16.0, 64
LRU_W, LRU_BLOCKS, LRU_CONV, LRU_C = 512, 8, 4, 8.0
DIL_PATTERNS = ((128, 1), (512, 4), (2048, 16))
N_PAT, DIL_HEADS, DIL_DH, ATTN_BLK = 3, 4, 128, 128
ROPE_THETA, ROPE_DIMS = 500000.0, 32
N_BR, BR_W = 3, 512
N_EXP, N_GRP, EXP_PER_GRP, TOP_K, EXP_FF = 16, 4, 4, 2, 512

LANE = 128
GATE_OFF, DIL_OFF, XB_OFF, YB_OFF, Q_OFF, K_OFF, V_OFF, OG_OFF = 0, 3072, 7680, 8192, 8704, 8960, 9216, 9728
PROJ_COLS = 10240
PROJ_TN = 1024
PROJ_RC = 512
MERGE_TM = 512
MOE_BM = 512
VMEM_LIMIT = 56 * 1024 * 1024

HIGHEST = lax.Precision.HIGHEST
NT = (((1,), (1,)), ((), ()))


def _cparams(sem):
    return pltpu.CompilerParams(dimension_semantics=sem, vmem_limit_bytes=VMEM_LIMIT)


def _inproj_body(x_ref, g_ref, w_ref, wg_ref, proj_ref, glow_ref, h_scr):
    j = pl.program_id(1)
    nrc = SEQ // PROJ_RC

    @pl.when(j == 0)
    def _():
        for c in range(nrc):
            rows = pl.ds(c * PROJ_RC, PROJ_RC)
            x = x_ref[0, rows, :]
            ms = jnp.mean(x * x, axis=-1, keepdims=True)
            h = ((x * lax.rsqrt(ms + RMS_EPS)) * g_ref[...]).astype(bf16)
            h_scr[rows, :] = h
            glow_ref[0, rows, :] = jnp.dot(h, wg_ref[...], preferred_element_type=f32)

    for c in range(nrc):
        rows = pl.ds(c * PROJ_RC, PROJ_RC)
        proj_ref[0, rows, :] = jnp.dot(h_scr[rows, :], w_ref[...], preferred_element_type=f32).astype(bf16)


def _inproj(x, gain, w, wg):
    B = x.shape[0]
    nj = PROJ_COLS // PROJ_TN
    return pl.pallas_call(
        _inproj_body,
        grid=(B, nj),
        in_specs=[
            pl.BlockSpec((1, SEQ, D_MODEL), lambda b, j: (b, 0, 0)),
            pl.BlockSpec((1, D_MODEL), lambda b, j: (0, 0)),
            pl.BlockSpec((D_MODEL, PROJ_TN), lambda b, j: (0, j)),
            pl.BlockSpec((D_MODEL, LANE), lambda b, j: (0, 0)),
        ],
        out_specs=[
            pl.BlockSpec((1, SEQ, PROJ_TN), lambda b, j: (b, 0, j)),
            pl.BlockSpec((1, SEQ, LANE), lambda b, j: (b, 0, 0)),
        ],
        out_shape=[
            jax.ShapeDtypeStruct((B, SEQ, PROJ_COLS), bf16),
            jax.ShapeDtypeStruct((B, SEQ, LANE), f32),
        ],
        scratch_shapes=[pltpu.VMEM((SEQ, D_MODEL), bf16)],
        compiler_params=_cparams(("parallel", "arbitrary")),
        name="inproj",
    )(x, gain, w, wg)


def _gla_body(q_ref, k_ref, v_ref, og_ref, glow_ref, wup_ref, bg_ref, gain_ref, o_ref, la_scr, st_scr):
    C = GLA_CHUNK
    HK = GLA_HEADS * GLA_DK
    for c in range(SEQ // 512):
        rows = pl.ds(c * 512, 512)
        z = jnp.dot(glow_ref[0, rows, :], wup_ref[...], preferred_element_type=f32, precision=HIGHEST) + bg_ref[...]
        la_scr[rows, :] = (jnp.minimum(z, 0.0) - jnp.log1p(jnp.exp(-jnp.abs(z)))) * (1.0 / GLA_NORM)
    st_scr[...] = jnp.zeros_like(st_scr)

    ri = lax.broadcasted_iota(i32, (C, C), 0)
    ci = lax.broadcasted_iota(i32, (C, C), 1)
    causal = ri >= ci
    tri = causal.astype(f32)
    lane_head = lax.broadcasted_iota(i32, (1, HK), 1) // GLA_DK
    gain = gain_ref[...]

    def chunk(n, carry):
        rows = pl.ds(pl.multiple_of(n * C, C), C)
        la = la_scr[rows, :]
        cum = jnp.dot(tri, la, preferred_element_type=f32, precision=HIGHEST)
        cum_t = cum.T
        tot_t = cum_t[:, C - 1:C]
        qf = q_ref[0, rows, :].astype(f32)
        kf = k_ref[0, rows, :].astype(f32)
        qd = qf * (GLA_DK ** -0.5) * jnp.exp(cum)
        ki = (kf * jnp.exp(-cum)).astype(bf16)
        kd_t = (kf.T * jnp.exp(tot_t - cum_t)).astype(bf16)
        dec_t = jnp.exp(tot_t)
        vb = v_ref[0, rows, :]
        ogf = og_ref[0, rows, :].astype(f32)
        for h in range(GLA_HEADS):
            cols = slice(h * GLA_DV, (h + 1) * GLA_DV)
            qh = jnp.where(lane_head == h, qd, 0.0).astype(bf16)
            s = lax.dot_general(qh, ki, NT, preferred_element_type=f32)
            s = jnp.where(causal, s, 0.0).astype(bf16)
            vh = vb[:, cols]
            st = st_scr[h]
            o = jnp.dot(s, vh, preferred_element_type=f32) + jnp.dot(qh, st.astype(bf16), preferred_element_type=f32)
            st_scr[h] = st * dec_t + jnp.dot(kd_t, vh, preferred_element_type=f32)
            o = o * lax.rsqrt(jnp.mean(o * o, axis=-1, keepdims=True) + RMS_EPS) * gain
            g = ogf[:, cols]
            o_ref[0, rows, cols] = (o * (g * jax.nn.sigmoid(g))).astype(bf16)
        return carry

    lax.fori_loop(0, SEQ // C, chunk, 0)


def _gla(proj, glow, wup, bg, gain):
    B = proj.shape[0]
    HK = GLA_HEADS * GLA_DK
    HV = GLA_HEADS * GLA_DV
    return pl.pallas_call(
        _gla_body,
        grid=(B,),
        in_specs=[
            pl.BlockSpec((1, SEQ, HK), lambda b: (b, 0, Q_OFF // HK)),
            pl.BlockSpec((1, SEQ, HK), lambda b: (b, 0, K_OFF // HK)),
            pl.BlockSpec((1, SEQ, HV), lambda b: (b, 0, V_OFF // HV)),
            pl.BlockSpec((1, SEQ, HV), lambda b: (b, 0, OG_OFF // HV)),
            pl.BlockSpec((1, SEQ, LANE), lambda b: (b, 0, 0)),
            pl.BlockSpec((LANE, HK), lambda b: (0, 0)),
            pl.BlockSpec((1, HK), lambda b: (0, 0)),
            pl.BlockSpec((1, GLA_DV), lambda b: (0, 0)),
        ],
        out_specs=pl.BlockSpec((1, SEQ, HV), lambda b: (b, 0, 0)),
        out_shape=jax.ShapeDtypeStruct((B, SEQ, HV), bf16),
        scratch_shapes=[pltpu.VMEM((SEQ, HK), f32), pltpu.VMEM((GLA_HEADS, HK, GLA_DV), f32)],
        compiler_params=_cparams(("parallel",)),
        name="gla",
    )(proj, proj, proj, proj, glow, wup, bg, gain)


LRU_SEG = 8
LRU_SEGLEN = SEQ // LRU_SEG


def _lru_body(xb_ref, yb_ref, cw_ref, cb_ref, w_ref, b_ref, lam_ref, o_ref, a_scr, u_scr, hs_scr, as_scr):
    x = xb_ref[0].astype(f32)
    row = lax.broadcasted_iota(i32, (SEQ, LANE), 0)
    xc = x * cw_ref[LRU_CONV - 1:LRU_CONV, :] + cb_ref[...]
    for j in range(1, LRU_CONV):
        xs = jnp.where(row >= j, pltpu.roll(x, j, 0), 0.0)
        xc = xc + xs * cw_ref[LRU_CONV - 1 - j:LRU_CONV - j, :]
    z = jnp.dot(xc.astype(bf16), w_ref[0], preferred_element_type=f32) + b_ref[0]
    r = jax.nn.sigmoid(z[:, :LANE])
    gi = jax.nn.sigmoid(z[:, LANE:])
    nl = -lam_ref[...]
    sp = jnp.maximum(nl, 0.0) + jnp.log1p(jnp.exp(-jnp.abs(nl)))
    log_a = (-LRU_C) * r * sp
    a = jnp.exp(log_a)
    u = jnp.sqrt(jnp.tanh(-log_a) * (a * a + 1.0)) * gi * xc
    a_scr[...] = a
    u_scr[...] = u

    def step(t, carry):
        h, acc = carry
        idx = pl.ds(t, LRU_SEG, stride=LRU_SEGLEN)
        at = a_scr[idx, :]
        h = at * h + u_scr[idx, :]
        acc = at * acc
        hs_scr[idx, :] = h
        as_scr[idx, :] = acc
        return h, acc

    hfin, afin = lax.fori_loop(0, LRU_SEGLEN, step, (jnp.zeros((LRU_SEG, LANE), f32), jnp.ones((LRU_SEG, LANE), f32)),
                               unroll=8)
    cin = jnp.zeros((1, LANE), f32)
    for s in range(LRU_SEG):
        rows = pl.ds(s * LRU_SEGLEN, LRU_SEGLEN)
        h = hs_scr[rows, :] + as_scr[rows, :] * cin
        o_ref[0, rows, :] = (h * jax.nn.gelu(yb_ref[0, rows, :].astype(f32))).astype(bf16)
        cin = hfin[s:s + 1, :] + afin[s:s + 1, :] * cin


def _lru(proj, cw, cb, w, b, lam):
    B = proj.shape[0]
    nt = LRU_W // LANE
    return pl.pallas_call(
        _lru_body,
        grid=(B, nt),
        in_specs=[
            pl.BlockSpec((1, SEQ, LANE), lambda b, t: (b, 0, XB_OFF // LANE + t)),
            pl.BlockSpec((1, SEQ, LANE), lambda b, t: (b, 0, YB_OFF // LANE + t)),
            pl.BlockSpec((LRU_CONV, LANE), lambda b, t: (0, t)),
            pl.BlockSpec((1, LANE), lambda b, t: (0, t)),
            pl.BlockSpec((1, LANE, 2 * LANE), lambda b, t: (t, 0, 0)),
            pl.BlockSpec((1, 1, 2 * LANE), lambda b, t: (t, 0, 0)),
            pl.BlockSpec((1, LANE), lambda b, t: (0, t)),
        ],
        out_specs=pl.BlockSpec((1, SEQ, LANE), lambda b, t: (b, 0, t)),
        out_shape=jax.ShapeDtypeStruct((B, SEQ, LRU_W), bf16),
        scratch_shapes=[pltpu.VMEM((SEQ, LANE), f32)] * 4,
        compiler_params=_cparams(("parallel", "parallel")),
        name="lru",
    )(proj, proj, cw, cb, w, b, lam)


def _rope_body(pos_ref, freq_ref, cos_ref, sa_ref, sb_ref):
    ang = pos_ref[0] * freq_ref[...]
    lane = lax.broadcasted_iota(i32, (SEQ, LANE), 1)
    sn = jnp.sin(ang)
    cos_ref[0] = jnp.cos(ang)
    sa_ref[0] = jnp.where((lane >= ROPE_DIMS // 2) & (lane < ROPE_DIMS), sn, 0.0)
    sb_ref[0] = jnp.where(lane < ROPE_DIMS // 2, -sn, 0.0)


def _rope_tables(posf, freq):
    B = posf.shape[0]
    spec = pl.BlockSpec((1, SEQ, LANE), lambda b: (b, 0, 0))
    return pl.pallas_call(
        _rope_body,
        grid=(B,),
        in_specs=[pl.BlockSpec((1, SEQ, 1), lambda b: (b, 0, 0)), pl.BlockSpec((1, LANE), lambda b: (0, 0))],
        out_specs=[spec, spec, spec],
        out_shape=[jax.ShapeDtypeStruct((B, SEQ, LANE), f32)] * 3,
        compiler_params=_cparams(("parallel",)),
        name="rope_tables",
    )(posf, freq)


ATTN_PC = 256


def _attn_body(q0, q1, q2, k0, k1, k2, v0, v1, v2, cos_ref, sa_ref, sb_ref, qg_ref, kg_ref, o_ref,
               qd, kd, vd, tq, tk, tv, od, ld, on, ln):
    q_refs, k_refs, v_refs = (q0, q1, q2), (k0, k1, k2), (v0, v1, v2)
    BLK = ATTN_BLK
    qg = qg_ref[...]
    kg = kg_ref[...]
    kd[pl.ds(0, BLK), :] = jnp.zeros((BLK, LANE), bf16)
    vd[pl.ds(0, BLK), :] = jnp.zeros((BLK, LANE), bf16)

    qi = lax.broadcasted_iota(i32, (BLK, 2 * BLK), 0)
    kj = lax.broadcasted_iota(i32, (BLK, 2 * BLK), 1)
    is_cur = kj >= BLK
    band = jnp.where(is_cur, kj - BLK, qi) <= jnp.where(is_cur, qi, kj)

    def norm_rope(t, gain, cs, sa, sb):
        t = t * lax.rsqrt(jnp.mean(t * t, axis=-1, keepdims=True) + RMS_EPS) * gain
        return t * cs + pltpu.roll(t, ROPE_DIMS // 2, 1) * sa + pltpu.roll(t, LANE - ROPE_DIMS // 2, 1) * sb

    for p, (_, dil) in enumerate(DIL_PATTERNS):
        L = SEQ // dil
        nbl = L // BLK

        def prep(c, carry, p=p, dil=dil):
            rows = pl.ds(pl.multiple_of(c * ATTN_PC, ATTN_PC), ATTN_PC)
            cs, sa, sb = cos_ref[0, rows, :], sa_ref[0, rows, :], sb_ref[0, rows, :]
            q = norm_rope(q_refs[p][0, rows, :].astype(f32), qg, cs, sa, sb) * (DIL_DH ** -0.5)
            k = norm_rope(k_refs[p][0, rows, :].astype(f32), kg, cs, sa, sb)
            if dil == 1:
                orow = pl.ds(pl.multiple_of(c * ATTN_PC, ATTN_PC) + BLK, ATTN_PC)
                qd[rows, :] = q.astype(bf16)
                kd[orow, :] = k.astype(bf16)
                vd[orow, :] = v_refs[p][0, rows, :]
            else:
                tq[rows, :] = q
                tk[rows, :] = k
                tv[rows, :] = v_refs[p][0, rows, :].astype(f32)
            return carry

        lax.fori_loop(0, SEQ // ATTN_PC, prep, 0)
        if dil > 1:
            for r in range(dil):
                src = pl.ds(r, L, stride=dil)
                qd[pl.ds(r * L, L), :] = tq[src, :].astype(bf16)
                kd[pl.ds(BLK + r * L, L), :] = tk[src, :].astype(bf16)
                vd[pl.ds(BLK + r * L, L), :] = tv[src, :].astype(bf16)

        def block(b, carry, nbl=nbl):
            r0 = pl.multiple_of(b * BLK, BLK)
            q = qd[pl.ds(r0, BLK), :]
            ks = kd[pl.ds(r0, 2 * BLK), :]
            vs = vd[pl.ds(r0, 2 * BLK), :]
            s = lax.dot_general(q, ks, NT, preferred_element_type=f32)
            first = (b % nbl) == 0
            valid = band & (is_cur | jnp.logical_not(first))
            s = jnp.where(valid, s, -jnp.inf)
            m = jnp.max(s, axis=-1, keepdims=True)
            e = jnp.exp(s - m)
            den = jnp.sum(e, axis=-1, keepdims=True)
            o = jnp.dot(e.astype(bf16), vs, preferred_element_type=f32) / den
            od[pl.ds(r0, BLK), :] = o
            ld[pl.ds(r0, BLK), :] = jnp.broadcast_to(m + jnp.log(den), (BLK, LANE))
            return carry

        lax.fori_loop(0, SEQ // BLK, block, 0)
        if dil == 1:
            on[p] = od[...]
            ln[p] = ld[...]
        else:
            for r in range(dil):
                dst = pl.ds(r, L, stride=dil)
                on[p, dst, :] = od[pl.ds(r * L, L), :]
                ln[p, dst, :] = ld[pl.ds(r * L, L), :]

    def combine(c, carry):
        rows = pl.ds(pl.multiple_of(c * ATTN_PC, ATTN_PC), ATTN_PC)
        l0, l1, l2 = ln[0, rows, :], ln[1, rows, :], ln[2, rows, :]
        mx = jnp.maximum(jnp.maximum(l0, l1), l2)
        w0, w1, w2 = jnp.exp(l0 - mx), jnp.exp(l1 - mx), jnp.exp(l2 - mx)
        o = (w0 * on[0, rows, :] + w1 * on[1, rows, :] + w2 * on[2, rows, :]) / (w0 + w1 + w2)
        o_ref[0, rows, :] = o.astype(bf16)
        return carry

    lax.fori_loop(0, SEQ // ATTN_PC, combine, 0)


def _attn(proj, cos, sa, sb, qg, kg):
    B = proj.shape[0]
    base = DIL_OFF // LANE

    def pspec(qkv, p):
        return pl.BlockSpec((1, SEQ, LANE), lambda b, h, o=base + (qkv * N_PAT + p) * DIL_HEADS: (b, 0, o + h))

    tspec = pl.BlockSpec((1, SEQ, LANE), lambda b, h: (b, 0, 0))
    gspec = pl.BlockSpec((1, LANE), lambda b, h: (0, 0))
    in_specs = [pspec(qkv, p) for qkv in range(3) for p in range(N_PAT)] + [tspec, tspec, tspec, gspec, gspec]
    return pl.pallas_call(
        _attn_body,
        grid=(B, DIL_HEADS),
        in_specs=in_specs,
        out_specs=pl.BlockSpec((1, SEQ, LANE), lambda b, h: (b, 0, h)),
        out_shape=jax.ShapeDtypeStruct((B, SEQ, DIL_HEADS * DIL_DH), bf16),
        scratch_shapes=[
            pltpu.VMEM((SEQ, LANE), bf16),
            pltpu.VMEM((SEQ + ATTN_BLK, LANE), bf16),
            pltpu.VMEM((SEQ + ATTN_BLK, LANE), bf16),
            pltpu.VMEM((SEQ, LANE), f32),
            pltpu.VMEM((SEQ, LANE), f32),
            pltpu.VMEM((SEQ, LANE), f32),
            pltpu.VMEM((SEQ, LANE), f32),
            pltpu.VMEM((SEQ, LANE), f32),
            pltpu.VMEM((N_PAT, SEQ, LANE), f32),
            pltpu.VMEM((N_PAT, SEQ, LANE), f32),
        ],
        compiler_params=_cparams(("parallel", "arbitrary")),
        name="dilated_attn",
    )(*([proj] * 9), cos, sa, sb, qg, kg)


def _merge_body(a_ref, b_ref, c_ref, g0_ref, g1_ref, g2_ref, x_ref, wb_ref, wo_ref, n2_ref, rw_ref, rb_ref,
                x1_ref, h2_ref, ei_ref, ew_ref):
    merged = None
    for n, (br, g) in enumerate(((a_ref, g0_ref), (b_ref, g1_ref), (c_ref, g2_ref))):
        t = jax.nn.sigmoid(g[...].astype(f32)) * jnp.dot(br[...], wb_ref[n], preferred_element_type=f32)
        merged = t if merged is None else merged + t
    x1 = x_ref[...] + jnp.dot(merged.astype(bf16), wo_ref[...], preferred_element_type=f32)
    x1_ref[...] = x1
    h2 = ((x1 * lax.rsqrt(jnp.mean(x1 * x1, axis=-1, keepdims=True) + RMS_EPS)) * n2_ref[...]).astype(bf16)
    h2_ref[...] = h2

    lg = lax.dot_general(rw_ref[...], h2, NT, preferred_element_type=f32) + rb_ref[...]
    ex = jnp.exp(lg - jnp.max(lg, axis=0, keepdims=True))
    sc = ex / jnp.sum(ex, axis=0, keepdims=True)
    srow = [sc[e:e + 1, :] for e in range(N_EXP)]
    best = bidx = None
    for gidx in range(N_GRP):
        s0, s1, s2, s3 = srow[4 * gidx:4 * gidx + 4]
        hi01, lo01, hi23, lo23 = jnp.maximum(s0, s1), jnp.minimum(s0, s1), jnp.maximum(s2, s3), jnp.minimum(s2, s3)
        gs = jnp.maximum(hi01, hi23) + jnp.maximum(jnp.minimum(hi01, hi23), jnp.maximum(lo01, lo23))
        if best is None:
            best, bidx = gs, jnp.zeros_like(gs, dtype=i32)
        else:
            better = gs > best
            best = jnp.where(better, gs, best)
            bidx = jnp.where(better, gidx, bidx)
    cand = []
    for i in range(EXP_PER_GRP):
        v = srow[i]
        for gidx in range(1, N_GRP):
            v = jnp.where(bidx == gidx, srow[4 * gidx + i], v)
        cand.append(v)

    def argmax4(vals):
        bw, bi = vals[0], jnp.zeros_like(bidx)
        for i in range(1, EXP_PER_GRP):
            better = vals[i] > bw
            bw = jnp.where(better, vals[i], bw)
            bi = jnp.where(better, i, bi)
        return bw, bi

    w1, i1 = argmax4(cand)
    w2, i2 = argmax4([jnp.where(i1 == i, -jnp.inf, cand[i]) for i in range(EXP_PER_GRP)])
    tot = w1 + w2
    ei_ref[0:1, :] = bidx * EXP_PER_GRP + i1
    ei_ref[1:2, :] = bidx * EXP_PER_GRP + i2
    ew_ref[0:1, :] = w1 / tot
    ew_ref[1:2, :] = w2 / tot


def _merge(bra, brb, brc, proj2d, x2d, wb, wo, n2, rw, rb):
    T = x2d.shape[0]
    TM = MERGE_TM
    rspec = lambda w: pl.BlockSpec((TM, w), lambda i: (i, 0))
    full = lambda shape: pl.BlockSpec(shape, lambda i: (0,) * len(shape))
    return pl.pallas_call(
        _merge_body,
        grid=(T // TM,),
        in_specs=[
            rspec(BR_W), rspec(BR_W), rspec(BR_W),
            pl.BlockSpec((TM, D_MODEL), lambda i: (i, GATE_OFF // D_MODEL + 0)),
            pl.BlockSpec((TM, D_MODEL), lambda i: (i, GATE_OFF // D_MODEL + 1)),
            pl.BlockSpec((TM, D_MODEL), lambda i: (i, GATE_OFF // D_MODEL + 2)),
            rspec(D_MODEL),
            full((N_BR, BR_W, D_MODEL)), full((D_MODEL, D_MODEL)), full((1, D_MODEL)),
            full((N_EXP, D_MODEL)), full((N_EXP, 1)),
        ],
        out_specs=[
            rspec(D_MODEL), rspec(D_MODEL),
            pl.BlockSpec((TOP_K, TM), lambda i: (0, i)),
            pl.BlockSpec((TOP_K, TM), lambda i: (0, i)),
        ],
        out_shape=[
            jax.ShapeDtypeStruct((T, D_MODEL), f32),
            jax.ShapeDtypeStruct((T, D_MODEL), bf16),
            jax.ShapeDtypeStruct((TOP_K, T), i32),
            jax.ShapeDtypeStruct((TOP_K, T), f32),
        ],
        compiler_params=_cparams(("parallel",)),
        name="merge_router",
    )(bra, brb, brc, proj2d, proj2d, proj2d, x2d, wb, wo, n2, rw, rb)


def _expert_body(be_ref, nu_ref, xs_ref, wgu_ref, wd_ref, ys_ref):
    i = pl.program_id(0)

    @pl.when(i < nu_ref[0])
    def _():
        gu = jnp.dot(xs_ref[...], wgu_ref[0], preferred_element_type=f32)
        g = gu[:, :EXP_FF]
        act = (g * jax.nn.sigmoid(g)) * gu[:, EXP_FF:]
        ys_ref[...] = jnp.dot(act.astype(bf16), wd_ref[0], preferred_element_type=f32).astype(bf16)

    @pl.when(i >= nu_ref[0])
    def _():
        ys_ref[...] = jnp.zeros_like(ys_ref)


def _experts(block_e, n_used, xs, wgu, wd):
    n_rows = xs.shape[0]
    grid_spec = pltpu.PrefetchScalarGridSpec(
        num_scalar_prefetch=2,
        grid=(n_rows // MOE_BM,),
        in_specs=[
            pl.BlockSpec((MOE_BM, D_MODEL), lambda i, be, nu: (i, 0)),
            pl.BlockSpec((1, D_MODEL, 2 * EXP_FF), lambda i, be, nu: (be[i], 0, 0)),
            pl.BlockSpec((1, EXP_FF, D_MODEL), lambda i, be, nu: (be[i], 0, 0)),
        ],
        out_specs=pl.BlockSpec((MOE_BM, D_MODEL), lambda i, be, nu: (i, 0)),
    )
    return pl.pallas_call(
        _expert_body,
        grid_spec=grid_spec,
        out_shape=jax.ShapeDtypeStruct((n_rows, D_MODEL), bf16),
        compiler_params=_cparams(("arbitrary",)),
        name="experts",
    )(block_e, n_used, xs, wgu, wd)


def _moe(x1, h2, eidx, ew, wgu, wd):
    T = x1.shape[0]
    TK = T * TOP_K
    n_rows = TK + N_EXP * MOE_BM
    n_blocks = n_rows // MOE_BM
    e_flat = eidx.T.reshape(TK)
    onehot = (e_flat[:, None] == jnp.arange(N_EXP, dtype=i32)[None, :]).astype(i32)
    csum = jnp.cumsum(onehot, axis=0)
    pos = jnp.take_along_axis(csum, e_flat[:, None], axis=1)[:, 0] - 1
    counts = csum[-1]
    padded = (counts + MOE_BM - 1) // MOE_BM * MOE_BM
    pad_ends = jnp.cumsum(padded)
    pad_starts = pad_ends - padded
    dest = pad_starts[e_flat] + pos
    src = jnp.zeros((n_rows,), i32).at[dest].set(jnp.arange(TK, dtype=i32) // TOP_K)
    block_e = jnp.minimum(jnp.searchsorted(pad_ends, jnp.arange(n_blocks, dtype=i32) * MOE_BM, side="right"),
                          N_EXP - 1).astype(i32)
    n_used = (pad_ends[-1:] // MOE_BM).astype(i32)
    xs = h2[src]
    ys = _experts(block_e, n_used, xs, wgu, wd)
    y = ys[dest].reshape(T, TOP_K, D_MODEL).astype(f32)
    w = ew.T
    return x1 + y[:, 0] * w[:, 0:1] + y[:, 1] * w[:, 1:2]


def _pack_w_in(w):
    cuts = np.cumsum([256, 256, 512, 512, 16, 512, 512, 4608, 3072])[:-1].tolist()
    q, k, v, og, glow, xb, yb, qkv, gates = jnp.split(w, cuts, axis=-1)
    packed = jnp.concatenate([gates, qkv, xb, yb, q, k, v, og], axis=-1).astype(bf16)
    wg = jnp.pad(glow, ((0, 0), (0, LANE - GLA_RANK))).astype(bf16)
    return packed, wg


def _block_diag_lru(w_a, w_x):
    bs = LRU_W // LRU_BLOCKS
    per = LANE // bs
    tiles = []
    for t in range(LRU_W // LANE):
        halves = []
        for w in (w_a, w_x):
            m = jnp.zeros((LANE, LANE), f32)
            for j in range(per):
                m = m.at[j * bs:(j + 1) * bs, j * bs:(j + 1) * bs].set(w[t * per + j])
            halves.append(m)
        tiles.append(jnp.concatenate(halves, axis=1))
    return jnp.stack(tiles).astype(bf16)


def kernel(x, positions, norm1_gain, w_in, gla_gate_up, gla_gate_bias, gla_out_gain, lru_conv_w, lru_conv_b, lru_w_a, lru_b_a, lru_w_x, lru_b_x, lru_lambda, q_norm_gain, k_norm_gain, w_branch, w_out, norm2_gain, router_w, router_b, w_gate, w_up, w_down):
    B, S, D = x.shape
    assert S == SEQ and D == D_MODEL
    T = B * S
    depth = w_in.shape[0]

    half = ROPE_DIMS // 2
    inv_freq = ROPE_THETA ** (-jnp.arange(half, dtype=f32) / half)
    freq = jnp.zeros((1, LANE), f32).at[0, :ROPE_DIMS].set(jnp.concatenate([inv_freq, inv_freq]))
    cos, sa, sb = _rope_tables(positions.astype(f32)[:, :, None], freq)

    rw = router_w.T.astype(bf16)
    rb = router_b.astype(f32)[:, None]
    for l in range(depth):
        w_packed, wg = _pack_w_in(w_in[l])
        proj, glow = _inproj(x, norm1_gain[l][None, :], w_packed, wg)
        wup = jnp.pad(gla_gate_up[l], ((0, LANE - GLA_RANK), (0, 0)))
        br_a = _gla(proj, glow, wup, gla_gate_bias[l][None, :], gla_out_gain[l][None, :])
        nt = LRU_W // LANE
        br_b = _lru(proj, lru_conv_w[l], lru_conv_b[l][None, :], _block_diag_lru(lru_w_a[l], lru_w_x[l]),
                    jnp.concatenate([lru_b_a[l].reshape(nt, 1, LANE), lru_b_x[l].reshape(nt, 1, LANE)], axis=-1),
                    lru_lambda[l][None, :])
        br_c = _attn(proj, cos, sa, sb, q_norm_gain[l][None, :], k_norm_gain[l][None, :])
        x1, h2, eidx, ew = _merge(br_a.reshape(T, BR_W), br_b.reshape(T, BR_W), br_c.reshape(T, BR_W),
                                  proj.reshape(T, PROJ_COLS), x.reshape(T, D), w_branch[l].astype(bf16),
                                  w_out[l].astype(bf16), norm2_gain[l][None, :], rw, rb)
        wgu = jnp.concatenate([w_gate[l], w_up[l]], axis=-1).astype(bf16)
        x = _moe(x1, h2, eidx, ew, wgu, w_down[l].astype(bf16)).reshape(B, S, D)
    return x
```

```python
import functools

import jax
import jax.numpy as jnp
import numpy as np
from jax import lax
from jax.experimental import pallas as pl
from jax.experimental.pallas import tpu as pltpu

f32 = jnp.float32
bf16 = jnp.bfloat16
i32 = jnp.int32

D_MODEL = 1024
SEQ = 2048
RMS_EPS = 1e-6
GLA_HEADS, GLA_DK, GLA_DV, GLA_RANK, GLA_NORM, GLA_CHUNK = 4, 64, 128, 16, 16.0, 64
LRU_W, LRU_BLOCKS, LRU_CONV, LRU_C = 512, 8, 4, 8.0
DIL_PATTERNS = ((128, 1), (512, 4), (2048, 16))
N_PAT, DIL_HEADS, DIL_DH, ATTN_BLK = 3, 4, 128, 128
ROPE_THETA, ROPE_DIMS = 500000.0, 32
N_BR, BR_W = 3, 512
N_EXP, N_GRP, EXP_PER_GRP, TOP_K, EXP_FF = 16, 4, 4, 2, 512

LANE = 128
GATE_OFF, DIL_OFF, XB_OFF, YB_OFF, Q_OFF, K_OFF, V_OFF, OG_OFF = 0, 3072, 7680, 8192, 8704, 8960, 9216, 9728
PROJ_COLS = 10240
PROJ_TN = 1024
PROJ_RC = 512
MERGE_TM = 512
MOE_BM = 512
VMEM_LIMIT = 56 * 1024 * 1024

HIGHEST = lax.Precision.HIGHEST
NT = (((1,), (1,)), ((), ()))


def _cparams(sem):
    return pltpu.CompilerParams(dimension_semantics=sem, vmem_limit_bytes=VMEM_LIMIT)


def _inproj_body(x_ref, g_ref, w_ref, wg_ref, proj_ref, glow_ref, h_scr):
    j = pl.program_id(1)
    nrc = SEQ // PROJ_RC

    @pl.when(j == 0)
    def _():
        for c in range(nrc):
            rows = pl.ds(c * PROJ_RC, PROJ_RC)
            x = x_ref[0, rows, :]
            ms = jnp.mean(x * x, axis=-1, keepdims=True)
            h = ((x * lax.rsqrt(ms + RMS_EPS)) * g_ref[...]).astype(bf16)
            h_scr[rows, :] = h
            glow_ref[0, rows, :] = jnp.dot(h, wg_ref[...], preferred_element_type=f32)

    for c in range(nrc):
        rows = pl.ds(c * PROJ_RC, PROJ_RC)
        proj_ref[0, rows, :] = jnp.dot(h_scr[rows, :], w_ref[...], preferred_element_type=f32).astype(bf16)


def _inproj(x, gain, w, wg):
    B = x.shape[0]
    nj = PROJ_COLS // PROJ_TN
    return pl.pallas_call(
        _inproj_body,
        grid=(B, nj),
        in_specs=[
            pl.BlockSpec((1, SEQ, D_MODEL), lambda b, j: (b, 0, 0)),
            pl.BlockSpec((1, D_MODEL), lambda b, j: (0, 0)),
            pl.BlockSpec((D_MODEL, PROJ_TN), lambda b, j: (0, j)),
            pl.BlockSpec((D_MODEL, LANE), lambda b, j: (0, 0)),
        ],
        out_specs=[
            pl.BlockSpec((1, SEQ, PROJ_TN), lambda b, j: (b, 0, j)),
            pl.BlockSpec((1, SEQ, LANE), lambda b, j: (b, 0, 0)),
        ],
        out_shape=[
            jax.ShapeDtypeStruct((B, SEQ, PROJ_COLS), bf16),
            jax.ShapeDtypeStruct((B, SEQ, LANE), f32),
        ],
        scratch_shapes=[pltpu.VMEM((SEQ, D_MODEL), bf16)],
        compiler_params=_cparams(("parallel", "arbitrary")),
        name="inproj",
    )(x, gain, w, wg)


def _gla_body(q_ref, k_ref, v_ref, og_ref, glow_ref, wh_ref, wl_ref, bg_ref, gain_ref, o_ref, cum_scr, st_scr, stb_scr):
    C = GLA_CHUNK
    HK = GLA_HEADS * GLA_DK
    PR = 512
    rc = lax.broadcasted_iota(i32, (PR, HK), 0) % C
    for c in range(SEQ // PR):
        rows = pl.ds(c * PR, PR)
        g = glow_ref[0, rows, :]
        gh = g.astype(bf16)
        gl = (g - gh.astype(f32)).astype(bf16)
        z = (jnp.dot(gh, wh_ref[...], preferred_element_type=f32) + jnp.dot(gl, wh_ref[...], preferred_element_type=f32)
             + jnp.dot(gh, wl_ref[...], preferred_element_type=f32)) + bg_ref[...]
        cum = (jnp.minimum(z, 0.0) - jnp.log1p(jnp.exp(-jnp.abs(z)))) * (1.0 / GLA_NORM)
        k = 1
        while k < C:
            cum = cum + jnp.where(rc >= k, pltpu.roll(cum, k, 0), 0.0)
            k *= 2
        cum_scr[rows, :] = cum
    st_scr[...] = jnp.zeros_like(st_scr)
    stb_scr[...] = jnp.zeros_like(stb_scr)

    ri = lax.broadcasted_iota(i32, (C, C), 0)
    ci = lax.broadcasted_iota(i32, (C, C), 1)
    causal = ri >= ci
    lane_head = lax.broadcasted_iota(i32, (1, HK), 1) // GLA_DK
    gain = gain_ref[...]

    def chunk(n, carry):
        rows = pl.ds(pl.multiple_of(n * C, C), C)
        cum = cum_scr[rows, :]
        cum_t = cum.T
        tot_t = cum_t[:, C - 1:C]
        qf = q_ref[0, rows, :].astype(f32)
        kf = k_ref[0, rows, :].astype(f32)
        qd = qf * (GLA_DK ** -0.5) * jnp.exp(cum)
        ki = (kf * jnp.exp(-cum)).astype(bf16)
        kd_t = (kf.T * jnp.exp(tot_t - cum_t)).astype(bf16)
        dec_t = jnp.exp(tot_t)
        vb = v_ref[0, rows, :]
        ogf = og_ref[0, rows, :].astype(f32)
        for h in range(GLA_HEADS):
            cols = slice(h * GLA_DV, (h + 1) * GLA_DV)
            qh = jnp.where(lane_head == h, qd, 0.0).astype(bf16)
            s = lax.dot_general(qh, ki, NT, preferred_element_type=f32)
            s = jnp.where(causal, s, 0.0).astype(bf16)
            vh = vb[:, cols]
            hr = slice(h * GLA_DK, (h + 1) * GLA_DK)
            o = jnp.dot(s, vh, preferred_element_type=f32) + jnp.dot(qh, stb_scr[h], preferred_element_type=f32)
            st = st_scr[h] * dec_t[hr, :] + jnp.dot(kd_t[hr, :], vh, preferred_element_type=f32)
            st_scr[h] = st
            stb_scr[h, hr, :] = st.astype(bf16)
            o = o * lax.rsqrt(jnp.mean(o * o, axis=-1, keepdims=True) + RMS_EPS) * gain
            g = ogf[:, cols]
            o_ref[0, rows, cols] = (o * (g * jax.nn.sigmoid(g))).astype(bf16)
        return carry

    lax.fori_loop(0, SEQ // C, chunk, 0, unroll=2)


def _gla(proj, glow, wup, bg, gain):
    B = proj.shape[0]
    HK = GLA_HEADS * GLA_DK
    HV = GLA_HEADS * GLA_DV
    wup_hi = wup.astype(bf16)
    wup_lo = (wup - wup_hi.astype(f32)).astype(bf16)
    return pl.pallas_call(
        _gla_body,
        grid=(B,),
        in_specs=[
            pl.BlockSpec((1, SEQ, HK), lambda b: (b, 0, Q_OFF // HK)),
            pl.BlockSpec((1, SEQ, HK), lambda b: (b, 0, K_OFF // HK)),
            pl.BlockSpec((1, SEQ, HV), lambda b: (b, 0, V_OFF // HV)),
            pl.BlockSpec((1, SEQ, HV), lambda b: (b, 0, OG_OFF // HV)),
            pl.BlockSpec((1, SEQ, LANE), lambda b: (b, 0, 0)),
            pl.BlockSpec((LANE, HK), lambda b: (0, 0)),
            pl.BlockSpec((LANE, HK), lambda b: (0, 0)),
            pl.BlockSpec((1, HK), lambda b: (0, 0)),
            pl.BlockSpec((1, GLA_DV), lambda b: (0, 0)),
        ],
        out_specs=pl.BlockSpec((1, SEQ, HV), lambda b: (b, 0, 0)),
        out_shape=jax.ShapeDtypeStruct((B, SEQ, HV), bf16),
        scratch_shapes=[
            pltpu.VMEM((SEQ, HK), f32),
            pltpu.VMEM((GLA_HEADS, GLA_DK, GLA_DV), f32),
            pltpu.VMEM((GLA_HEADS, HK, GLA_DV), bf16),
        ],
        compiler_params=_cparams(("parallel",)),
        name="gla",
    )(proj, proj, proj, proj, glow, wup_hi, wup_lo, bg, gain)


LRU_SEG = 8
LRU_SEGLEN = SEQ // LRU_SEG


def _lru_body(xb_ref, yb_ref, cw_ref, cb_ref, w_ref, b_ref, lam_ref, o_ref, a_scr, u_scr, hs_scr, as_scr):
    x = xb_ref[0].astype(f32)
    row = lax.broadcasted_iota(i32, (SEQ, LANE), 0)
    xc = x * cw_ref[LRU_CONV - 1:LRU_CONV, :] + cb_ref[...]
    for j in range(1, LRU_CONV):
        xs = jnp.where(row >= j, pltpu.roll(x, j, 0), 0.0)
        xc = xc + xs * cw_ref[LRU_CONV - 1 - j:LRU_CONV - j, :]
    z = jnp.dot(xc.astype(bf16), w_ref[0], preferred_element_type=f32) + b_ref[0]
    r = jax.nn.sigmoid(z[:, :LANE])
    gi = jax.nn.sigmoid(z[:, LANE:])
    nl = -lam_ref[...]
    sp = jnp.maximum(nl, 0.0) + jnp.log1p(jnp.exp(-jnp.abs(nl)))
    log_a = (-LRU_C) * r * sp
    a = jnp.exp(log_a)
    u = jnp.sqrt(jnp.tanh(-log_a) * (a * a + 1.0)) * gi * xc
    a_scr[...] = a
    u_scr[...] = u

    def step(t, carry):
        h, acc = carry
        idx = pl.ds(t, LRU_SEG, stride=LRU_SEGLEN)
        at = a_scr[idx, :]
        h = at * h + u_scr[idx, :]
        acc = at * acc
        hs_scr[idx, :] = h
        as_scr[idx, :] = acc
        return h, acc

    hfin, afin = lax.fori_loop(0, LRU_SEGLEN, step, (jnp.zeros((LRU_SEG, LANE), f32), jnp.ones((LRU_SEG, LANE), f32)),
                               unroll=8)
    cin = jnp.zeros((1, LANE), f32)
    for s in range(LRU_SEG):
        rows = pl.ds(s * LRU_SEGLEN, LRU_SEGLEN)
        h = hs_scr[rows, :] + as_scr[rows, :] * cin
        o_ref[0, rows, :] = (h * jax.nn.gelu(yb_ref[0, rows, :].astype(f32))).astype(bf16)
        cin = hfin[s:s + 1, :] + afin[s:s + 1, :] * cin


def _lru(proj, cw, cb, w, b, lam):
    B = proj.shape[0]
    nt = LRU_W // LANE
    return pl.pallas_call(
        _lru_body,
        grid=(B, nt),
        in_specs=[
            pl.BlockSpec((1, SEQ, LANE), lambda b, t: (b, 0, XB_OFF // LANE + t)),
            pl.BlockSpec((1, SEQ, LANE), lambda b, t: (b, 0, YB_OFF // LANE + t)),
            pl.BlockSpec((LRU_CONV, LANE), lambda b, t: (0, t)),
            pl.BlockSpec((1, LANE), lambda b, t: (0, t)),
            pl.BlockSpec((1, LANE, 2 * LANE), lambda b, t: (t, 0, 0)),
            pl.BlockSpec((1, 1, 2 * LANE), lambda b, t: (t, 0, 0)),
            pl.BlockSpec((1, LANE), lambda b, t: (0, t)),
        ],
        out_specs=pl.BlockSpec((1, SEQ, LANE), lambda b, t: (b, 0, t)),
        out_shape=jax.ShapeDtypeStruct((B, SEQ, LRU_W), bf16),
        scratch_shapes=[pltpu.VMEM((SEQ, LANE), f32)] * 4,
        compiler_params=_cparams(("parallel", "parallel")),
        name="lru",
    )(proj, proj, cw, cb, w, b, lam)


def _rope_body(pos_ref, freq_ref, cos_ref, ss_ref):
    ang = pos_ref[0] * freq_ref[...]
    lane = lax.broadcasted_iota(i32, (SEQ, LANE), 1)
    sn = jnp.sin(ang)
    cos_ref[0] = jnp.cos(ang)
    ss_ref[0] = jnp.where(lane < ROPE_DIMS // 2, -sn, sn)


def _rope_tables(posf, freq):
    B = posf.shape[0]
    spec = pl.BlockSpec((1, SEQ, LANE), lambda b: (b, 0, 0))
    return pl.pallas_call(
        _rope_body,
        grid=(B,),
        in_specs=[pl.BlockSpec((1, SEQ, 1), lambda b: (b, 0, 0)), pl.BlockSpec((1, LANE), lambda b: (0, 0))],
        out_specs=[spec, spec],
        out_shape=[jax.ShapeDtypeStruct((B, SEQ, LANE), f32)] * 2,
        compiler_params=_cparams(("parallel",)),
        name="rope_tables",
    )(posf, freq)


def _rope_partner():
    half = ROPE_DIMS // 2
    perm = np.arange(LANE)
    perm[:half] += half
    perm[half:ROPE_DIMS] -= half
    return perm


ATTN_PC = 256


def _attn_body(q0, q1, q2, k0, k1, k2, v0, v1, v2, cos_ref, ss_ref, g_ref, p2_ref, o_ref,
               qd, kd, vd, tq, tk, tv, od, ld, on, ln):
    q_refs, k_refs, v_refs = (q0, q1, q2), (k0, k1, k2), (v0, v1, v2)
    BLK = ATTN_BLK
    qg = g_ref[0:1, :] * (DIL_DH ** -0.5)
    qgr = g_ref[1:2, :] * (DIL_DH ** -0.5)
    kg = g_ref[2:3, :]
    kgr = g_ref[3:4, :]
    kd[pl.ds(0, BLK), :] = jnp.zeros((BLK, LANE), bf16)
    vd[pl.ds(0, BLK), pl.ds(0, LANE)] = jnp.zeros((BLK, LANE), bf16)
    vd[:, pl.ds(LANE, LANE)] = jnp.ones((SEQ + BLK, LANE), bf16)

    qi = lax.broadcasted_iota(i32, (BLK, 2 * BLK), 0)
    kj = lax.broadcasted_iota(i32, (BLK, 2 * BLK), 1)
    is_cur = kj >= BLK
    band = jnp.where(is_cur, kj - BLK, qi) <= jnp.where(is_cur, qi, kj)

    def norm_rope(raw, partner, gain, gain_p, cs, ss):
        rf = raw.astype(f32)
        r = lax.rsqrt(jnp.mean(rf * rf, axis=-1, keepdims=True) + RMS_EPS)
        return (rf * (gain * cs) + partner * (gain_p * ss)) * r

    for p, (_, dil) in enumerate(DIL_PATTERNS):
        L = SEQ // dil
        nbl = L // BLK

        def prep(c, carry, p=p, dil=dil):
            rows = pl.ds(pl.multiple_of(c * ATTN_PC, ATTN_PC), ATTN_PC)
            cs, ss = cos_ref[0, rows, :], ss_ref[0, rows, :]
            qr = q_refs[p][0, rows, :]
            kr = k_refs[p][0, rows, :]
            partner = jnp.dot(jnp.concatenate([qr, kr], axis=1), p2_ref[...], preferred_element_type=f32)
            q = norm_rope(qr, partner[:, :LANE], qg, qgr, cs, ss)
            k = norm_rope(kr, partner[:, LANE:], kg, kgr, cs, ss)
            if dil == 1:
                orow = pl.ds(pl.multiple_of(c * ATTN_PC, ATTN_PC) + BLK, ATTN_PC)
                qd[rows, :] = q.astype(bf16)
                kd[orow, :] = k.astype(bf16)
                vd[orow, pl.ds(0, LANE)] = v_refs[p][0, rows, :]
            else:
                tq[rows, :] = q
                tk[rows, :] = k
                tv[rows, :] = v_refs[p][0, rows, :].astype(f32)
            return carry

        lax.fori_loop(0, SEQ // ATTN_PC, prep, 0, unroll=2)
        if dil > 1:
            for r in range(dil):
                src = pl.ds(r, L, stride=dil)
                qd[pl.ds(r * L, L), :] = tq[src, :].astype(bf16)
                kd[pl.ds(BLK + r * L, L), :] = tk[src, :].astype(bf16)
                vd[pl.ds(BLK + r * L, L), pl.ds(0, LANE)] = tv[src, :].astype(bf16)

        o_dst, l_dst = (on.at[p], ln.at[p]) if dil == 1 else (od, ld)

        def block(b, carry, nbl=nbl, o_dst=o_dst, l_dst=l_dst):
            r0 = pl.multiple_of(b * BLK, BLK)
            q = qd[pl.ds(r0, BLK), :]
            ks = kd[pl.ds(r0, 2 * BLK), :]
            vs = vd[pl.ds(r0, 2 * BLK), :]
            s = lax.dot_general(q, ks, NT, preferred_element_type=f32)
            first = (b % nbl) == 0
            valid = band & (is_cur | jnp.logical_not(first))
            s = jnp.where(valid, s, -jnp.inf)
            m = jnp.max(s, axis=-1, keepdims=True)
            e = jnp.exp(s - m).astype(bf16)
            ov = jnp.dot(e, vs, preferred_element_type=f32)
            den = ov[:, LANE:]
            o_dst[pl.ds(r0, BLK), :] = ov[:, :LANE] / den
            l_dst[pl.ds(r0, BLK), :] = m + jnp.log(den)
            return carry

        lax.fori_loop(0, SEQ // BLK, block, 0, unroll=4)
        if dil > 1:
            for r in range(dil):
                dst = pl.ds(r, L, stride=dil)
                on[p, dst, :] = od[pl.ds(r * L, L), :]
                ln[p, dst, :] = ld[pl.ds(r * L, L), :]

    def combine(c, carry):
        rows = pl.ds(pl.multiple_of(c * ATTN_PC, ATTN_PC), ATTN_PC)
        l0, l1, l2 = ln[0, rows, :], ln[1, rows, :], ln[2, rows, :]
        mx = jnp.maximum(jnp.maximum(l0, l1), l2)
        w0, w1, w2 = jnp.exp(l0 - mx), jnp.exp(l1 - mx), jnp.exp(l2 - mx)
        o = (w0 * on[0, rows, :] + w1 * on[1, rows, :] + w2 * on[2, rows, :]) / (w0 + w1 + w2)
        o_ref[0, rows, :] = o.astype(bf16)
        return carry

    lax.fori_loop(0, SEQ // ATTN_PC, combine, 0)


def _attn(proj, cos, ss, qg, kg):
    B = proj.shape[0]
    base = DIL_OFF // LANE
    perm = _rope_partner()
    gains = jnp.stack([qg, qg[perm], kg, kg[perm]]).astype(f32)
    pm = np.zeros((LANE, LANE), np.float32)
    pm[perm[:ROPE_DIMS], np.arange(ROPE_DIMS)] = 1.0
    p2 = jnp.asarray(np.kron(np.eye(2, dtype=np.float32), pm), dtype=bf16)

    def pspec(qkv, p):
        return pl.BlockSpec((1, SEQ, LANE), lambda b, h, o=base + (qkv * N_PAT + p) * DIL_HEADS: (b, 0, o + h))

    tspec = pl.BlockSpec((1, SEQ, LANE), lambda b, h: (b, 0, 0))
    in_specs = [pspec(qkv, p) for qkv in range(3) for p in range(N_PAT)] + [
        tspec, tspec, pl.BlockSpec((4, LANE), lambda b, h: (0, 0)), pl.BlockSpec((2 * LANE, 2 * LANE), lambda b, h: (0, 0))]
    return pl.pallas_call(
        _attn_body,
        grid=(B, DIL_HEADS),
        in_specs=in_specs,
        out_specs=pl.BlockSpec((1, SEQ, LANE), lambda b, h: (b, 0, h)),
        out_shape=jax.ShapeDtypeStruct((B, SEQ, DIL_HEADS * DIL_DH), bf16),
        scratch_shapes=[
            pltpu.VMEM((SEQ, LANE), bf16),
            pltpu.VMEM((SEQ + ATTN_BLK, LANE), bf16),
            pltpu.VMEM((SEQ + ATTN_BLK, 2 * LANE), bf16),
            pltpu.VMEM((SEQ, LANE), f32),
            pltpu.VMEM((SEQ, LANE), f32),
            pltpu.VMEM((SEQ, LANE), f32),
            pltpu.VMEM((SEQ, LANE), f32),
            pltpu.VMEM((SEQ, LANE), f32),
            pltpu.VMEM((N_PAT, SEQ, LANE), f32),
            pltpu.VMEM((N_PAT, SEQ, LANE), f32),
        ],
        compiler_params=_cparams(("parallel", "arbitrary")),
        name="dilated_attn",
    )(*([proj] * 9), cos, ss, gains, p2)


def _merge_body(a_ref, b_ref, c_ref, g0_ref, g1_ref, g2_ref, x_ref, wb_ref, wo_ref, n2_ref, rw_ref, rb_ref, tri_ref,
                x1_ref, h2_ref, ei_ref, ew_ref, pos_ref, cnt_ref, cnt_scr):
    merged = None
    for n, (br, g) in enumerate(((a_ref, g0_ref), (b_ref, g1_ref), (c_ref, g2_ref))):
        t = jax.nn.sigmoid(g[...].astype(f32)) * jnp.dot(br[...], wb_ref[n], preferred_element_type=f32)
        merged = t if merged is None else merged + t
    x1 = x_ref[...] + jnp.dot(merged.astype(bf16), wo_ref[...], preferred_element_type=f32)
    x1_ref[...] = x1
    h2 = ((x1 * lax.rsqrt(jnp.mean(x1 * x1, axis=-1, keepdims=True) + RMS_EPS)) * n2_ref[...]).astype(bf16)
    h2_ref[...] = h2

    lg = lax.dot_general(rw_ref[...], h2, NT, preferred_element_type=f32) + rb_ref[...]
    ex = jnp.exp(lg - jnp.max(lg, axis=0, keepdims=True))
    sc = ex / jnp.sum(ex, axis=0, keepdims=True)
    srow = [sc[e:e + 1, :] for e in range(N_EXP)]
    best = bidx = None
    for gidx in range(N_GRP):
        s0, s1, s2, s3 = srow[4 * gidx:4 * gidx + 4]
        hi01, lo01, hi23, lo23 = jnp.maximum(s0, s1), jnp.minimum(s0, s1), jnp.maximum(s2, s3), jnp.minimum(s2, s3)
        gs = jnp.maximum(hi01, hi23) + jnp.maximum(jnp.minimum(hi01, hi23), jnp.maximum(lo01, lo23))
        if best is None:
            best, bidx = gs, jnp.zeros_like(gs, dtype=i32)
        else:
            better = gs > best
            best = jnp.where(better, gs, best)
            bidx = jnp.where(better, gidx, bidx)
    cand = []
    for i in range(EXP_PER_GRP):
        v = srow[i]
        for gidx in range(1, N_GRP):
            v = jnp.where(bidx == gidx, srow[4 * gidx + i], v)
        cand.append(v)

    def argmax4(vals):
        bw, bi = vals[0], jnp.zeros_like(bidx)
        for i in range(1, EXP_PER_GRP):
            better = vals[i] > bw
            bw = jnp.where(better, vals[i], bw)
            bi = jnp.where(better, i, bi)
        return bw, bi

    w1, i1 = argmax4(cand)
    w2, i2 = argmax4([jnp.where(i1 == i, -jnp.inf, cand[i]) for i in range(EXP_PER_GRP)])
    tot = w1 + w2
    e1 = bidx * EXP_PER_GRP + i1
    e2 = bidx * EXP_PER_GRP + i2
    ei_ref[0:1, :] = e1
    ei_ref[1:2, :] = e2
    ew_ref[0:1, :] = w1 / tot
    ew_ref[1:2, :] = w2 / tot

    @pl.when(pl.program_id(0) == 0)
    def _():
        cnt_scr[...] = jnp.zeros_like(cnt_scr)

    erow = lax.broadcasted_iota(i32, (N_EXP, MERGE_TM), 0)
    oh1 = erow == e1
    oh2 = erow == e2
    both = jnp.where(oh1 | oh2, 1.0, 0.0)
    incl = jnp.dot(both.astype(bf16), tri_ref[...], preferred_element_type=f32)
    base = cnt_scr[...] + (incl - both)
    pos_ref[0:1, :] = jnp.sum(jnp.where(oh1, base, 0.0), axis=0, keepdims=True).astype(i32)
    pos_ref[1:2, :] = jnp.sum(jnp.where(oh2, base, 0.0), axis=0, keepdims=True).astype(i32)
    cnt = cnt_scr[...] + incl[:, MERGE_TM - 1:MERGE_TM]
    cnt_scr[...] = cnt
    cnt_ref[...] = cnt


def _merge(bra, brb, brc, proj2d, x2d, wb, wo, n2, rw, rb):
    T = x2d.shape[0]
    TM = MERGE_TM
    rspec = lambda w: pl.BlockSpec((TM, w), lambda i: (i, 0))
    full = lambda shape: pl.BlockSpec(shape, lambda i: (0,) * len(shape))
    tri = (jnp.arange(TM)[:, None] <= jnp.arange(TM)[None, :]).astype(bf16)
    return pl.pallas_call(
        _merge_body,
        grid=(T // TM,),
        in_specs=[
            rspec(BR_W), rspec(BR_W), rspec(BR_W),
            pl.BlockSpec((TM, D_MODEL), lambda i: (i, GATE_OFF // D_MODEL + 0)),
            pl.BlockSpec((TM, D_MODEL), lambda i: (i, GATE_OFF // D_MODEL + 1)),
            pl.BlockSpec((TM, D_MODEL), lambda i: (i, GATE_OFF // D_MODEL + 2)),
            rspec(D_MODEL),
            full((N_BR, BR_W, D_MODEL)), full((D_MODEL, D_MODEL)), full((1, D_MODEL)),
            full((N_EXP, D_MODEL)), full((N_EXP, 1)), full((TM, TM)),
        ],
        out_specs=[
            rspec(D_MODEL), rspec(D_MODEL),
            pl.BlockSpec((TOP_K, TM), lambda i: (0, i)),
            pl.BlockSpec((TOP_K, TM), lambda i: (0, i)),
            pl.BlockSpec((TOP_K, TM), lambda i: (0, i)),
            full((N_EXP, 1)),
        ],
        out_shape=[
            jax.ShapeDtypeStruct((T, D_MODEL), f32),
            jax.ShapeDtypeStruct((T, D_MODEL), bf16),
            jax.ShapeDtypeStruct((TOP_K, T), i32),
            jax.ShapeDtypeStruct((TOP_K, T), f32),
            jax.ShapeDtypeStruct((TOP_K, T), i32),
            jax.ShapeDtypeStruct((N_EXP, 1), f32),
        ],
        scratch_shapes=[pltpu.VMEM((N_EXP, 1), f32)],
        compiler_params=_cparams(("arbitrary",)),
        name="merge_router",
    )(bra, brb, brc, proj2d, proj2d, proj2d, x2d, wb, wo, n2, rw, rb, tri)


def _expert_body(be_ref, nu_ref, xs_ref, wgu_ref, wd_ref, ys_ref):
    i = pl.program_id(0)

    @pl.when(i < nu_ref[0])
    def _():
        gu = jnp.dot(xs_ref[...], wgu_ref[0], preferred_element_type=f32)
        g = gu[:, :EXP_FF]
        act = (g * jax.nn.sigmoid(g)) * gu[:, EXP_FF:]
        ys_ref[...] = jnp.dot(act.astype(bf16), wd_ref[0], preferred_element_type=f32).astype(bf16)

    @pl.when(i >= nu_ref[0])
    def _():
        ys_ref[...] = jnp.zeros_like(ys_ref)


def _experts(block_e, n_used, xs, wgu, wd):
    n_rows = xs.shape[0]
    grid_spec = pltpu.PrefetchScalarGridSpec(
        num_scalar_prefetch=2,
        grid=(n_rows // MOE_BM,),
        in_specs=[
            pl.BlockSpec((MOE_BM, D_MODEL), lambda i, be, nu: (i, 0)),
            pl.BlockSpec((1, D_MODEL, 2 * EXP_FF), lambda i, be, nu: (be[i], 0, 0)),
            pl.BlockSpec((1, EXP_FF, D_MODEL), lambda i, be, nu: (be[i], 0, 0)),
        ],
        out_specs=pl.BlockSpec((MOE_BM, D_MODEL), lambda i, be, nu: (i, 0)),
    )
    return pl.pallas_call(
        _expert_body,
        grid_spec=grid_spec,
        out_shape=jax.ShapeDtypeStruct((n_rows, D_MODEL), bf16),
        compiler_params=_cparams(("arbitrary",)),
        name="experts",
    )(block_e, n_used, xs, wgu, wd)


COMB_TM = 1024


def _combine_body(x1_ref, y0_ref, y1_ref, w_ref, o_ref):
    w = w_ref[...]
    o_ref[...] = x1_ref[...] + y0_ref[...].astype(f32) * w[:, 0:1] + y1_ref[...].astype(f32) * w[:, 1:2]


def _combine(x1, y0, y1, w):
    T = x1.shape[0]
    rspec = pl.BlockSpec((COMB_TM, D_MODEL), lambda i: (i, 0))
    return pl.pallas_call(
        _combine_body,
        grid=(T // COMB_TM,),
        in_specs=[rspec, rspec, rspec, pl.BlockSpec((COMB_TM, TOP_K), lambda i: (i, 0))],
        out_specs=rspec,
        out_shape=jax.ShapeDtypeStruct((T, D_MODEL), f32),
        compiler_params=_cparams(("parallel",)),
        name="moe_combine",
    )(x1, y0, y1, w)


def _moe(x1, h2, eidx, ew, pos, cnt, wgu, wd):
    T = x1.shape[0]
    TK = T * TOP_K
    n_rows = TK + N_EXP * MOE_BM
    n_blocks = n_rows // MOE_BM
    counts = cnt[:, 0].astype(i32)
    padded = (counts + MOE_BM - 1) // MOE_BM * MOE_BM
    pad_ends = jnp.cumsum(padded)
    pad_starts = pad_ends - padded
    start_of = jnp.zeros_like(eidx)
    for e in range(N_EXP):
        start_of = jnp.where(eidx == e, pad_starts[e], start_of)
    dest = start_of + pos
    tok = jnp.broadcast_to(jnp.arange(T, dtype=i32)[None, :], (TOP_K, T))
    src = jnp.zeros((n_rows,), i32).at[dest.reshape(TK)].set(tok.reshape(TK))
    blk_start = jnp.arange(n_blocks, dtype=i32) * MOE_BM
    block_e = jnp.minimum(jnp.sum((blk_start[:, None] >= pad_ends[None, :]).astype(i32), axis=1), N_EXP - 1)
    n_used = (pad_ends[-1:] // MOE_BM).astype(i32)
    xs = h2[src]
    ys = _experts(block_e, n_used, xs, wgu, wd)
    return _combine(x1, ys[dest[0]], ys[dest[1]], ew.T)


def _pack_w_in(w):
    cuts = np.cumsum([256, 256, 512, 512, 16, 512, 512, 4608, 3072])[:-1].tolist()
    q, k, v, og, glow, xb, yb, qkv, gates = jnp.split(w, cuts, axis=-1)
    packed = jnp.concatenate([gates, qkv, xb, yb, q, k, v, og], axis=-1).astype(bf16)
    wg = jnp.pad(glow, ((0, 0), (0, LANE - GLA_RANK))).astype(bf16)
    return packed, wg


def _block_diag_lru(w_a, w_x):
    bs = LRU_W // LRU_BLOCKS
    per = LANE // bs
    tiles = []
    for t in range(LRU_W // LANE):
        halves = []
        for w in (w_a, w_x):
            m = jnp.zeros((LANE, LANE), f32)
            for j in range(per):
                m = m.at[j * bs:(j + 1) * bs, j * bs:(j + 1) * bs].set(w[t * per + j])
            halves.append(m)
        tiles.append(jnp.concatenate(halves, axis=1))
    return jnp.stack(tiles).astype(bf16)


def kernel(x, positions, norm1_gain, w_in, gla_gate_up, gla_gate_bias, gla_out_gain, lru_conv_w, lru_conv_b, lru_w_a, lru_b_a, lru_w_x, lru_b_x, lru_lambda, q_norm_gain, k_norm_gain, w_branch, w_out, norm2_gain, router_w, router_b, w_gate, w_up, w_down):
    B, S, D = x.shape
    assert S == SEQ and D == D_MODEL
    T = B * S
    depth = w_in.shape[0]

    half = ROPE_DIMS // 2
    inv_freq = ROPE_THETA ** (-jnp.arange(half, dtype=f32) / half)
    freq = jnp.zeros((1, LANE), f32).at[0, :ROPE_DIMS].set(jnp.concatenate([inv_freq, inv_freq]))
    cos, ss = _rope_tables(positions.astype(f32)[:, :, None], freq)

    rw = router_w.T.astype(bf16)
    rb = router_b.astype(f32)[:, None]
    for l in range(depth):
        w_packed, wg = _pack_w_in(w_in[l])
        proj, glow = _inproj(x, norm1_gain[l][None, :], w_packed, wg)
        wup = jnp.pad(gla_gate_up[l], ((0, LANE - GLA_RANK), (0, 0)))
        br_a = _gla(proj, glow, wup, gla_gate_bias[l][None, :], gla_out_gain[l][None, :])
        nt = LRU_W // LANE
        br_b = _lru(proj, lru_conv_w[l], lru_conv_b[l][None, :], _block_diag_lru(lru_w_a[l], lru_w_x[l]),
                    jnp.concatenate([lru_b_a[l].reshape(nt, 1, LANE), lru_b_x[l].reshape(nt, 1, LANE)], axis=-1),
                    lru_lambda[l][None, :])
        br_c = _attn(proj, cos, ss, q_norm_gain[l], k_norm_gain[l])
        x1, h2, eidx, ew, pos, cnt = _merge(br_a.reshape(T, BR_W), br_b.reshape(T, BR_W), br_c.reshape(T, BR_W),
                                            proj.reshape(T, PROJ_COLS), x.reshape(T, D), w_branch[l].astype(bf16),
                                            w_out[l].astype(bf16), norm2_gain[l][None, :], rw, rb)
        wgu = jnp.concatenate([w_gate[l], w_up[l]], axis=-1).astype(bf16)
        x = _moe(x1, h2, eidx, ew, pos, cnt, wgu, w_down[l].astype(bf16)).reshape(B, S, D)
    return x
```

```python
import functools

import jax
import jax.numpy as jnp
import numpy as np
from jax import lax
from jax.experimental import pallas as pl
from jax.experimental.pallas import tpu as pltpu

f32 = jnp.float32
bf16 = jnp.bfloat16
i32 = jnp.int32

D_MODEL = 1024
SEQ = 2048
RMS_EPS = 1e-6
GLA_HEADS, GLA_DK, GLA_DV, GLA_RANK, GLA_NORM, GLA_CHUNK = 4, 64, 128, 16, 16.0, 64
LRU_W, LRU_BLOCKS, LRU_CONV, LRU_C = 512, 8, 4, 8.0
DIL_PATTERNS = ((128, 1), (512, 4), (2048, 16))
N_PAT, DIL_HEADS, DIL_DH, ATTN_BLK = 3, 4, 128, 128
ROPE_THETA, ROPE_DIMS = 500000.0, 32
N_BR, BR_W = 3, 512
N_EXP, N_GRP, EXP_PER_GRP, TOP_K, EXP_FF = 16, 4, 4, 2, 512

LANE = 128
GATE_OFF, DIL_OFF, XB_OFF, YB_OFF, Q_OFF, K_OFF, V_OFF, OG_OFF = 0, 3072, 7680, 8192, 8704, 8960, 9216, 9728
PROJ_COLS = 10240
PROJ_TN = 1024
PROJ_RC = 512
MERGE_TM = 512
MERGE_RC = 512
MOE_BM = 512
N_STREAMS = 1
GLA_NE = 1
VMEM_LIMIT = 56 * 1024 * 1024

HIGHEST = lax.Precision.HIGHEST
NT = (((1,), (1,)), ((), ()))


def _cparams(sem):
    return pltpu.CompilerParams(dimension_semantics=sem, vmem_limit_bytes=VMEM_LIMIT)


def _inproj_body(x_ref, g_ref, w_ref, wg_ref, proj_ref, glow_ref, h_scr):
    j = pl.program_id(1)
    nrc = SEQ // PROJ_RC

    @pl.when(j == 0)
    def _():
        for c in range(nrc):
            rows = pl.ds(c * PROJ_RC, PROJ_RC)
            x = x_ref[0, rows, :]
            ms = jnp.mean(x * x, axis=-1, keepdims=True)
            h = ((x * lax.rsqrt(ms + RMS_EPS)) * g_ref[...]).astype(bf16)
            h_scr[rows, :] = h
            glow_ref[0, rows, :] = jnp.dot(h, wg_ref[...], preferred_element_type=f32)

    for c in range(nrc):
        rows = pl.ds(c * PROJ_RC, PROJ_RC)
        proj_ref[0, rows, :] = jnp.dot(h_scr[rows, :], w_ref[...], preferred_element_type=f32).astype(bf16)


def _inproj(x, gain, w, wg, b0, B):
    nj = PROJ_COLS // PROJ_TN
    return pl.pallas_call(
        _inproj_body,
        grid=(B, nj),
        in_specs=[
            pl.BlockSpec((1, SEQ, D_MODEL), lambda b, j: (b + b0, 0, 0)),
            pl.BlockSpec((1, D_MODEL), lambda b, j: (0, 0)),
            pl.BlockSpec((D_MODEL, PROJ_TN), lambda b, j: (0, j)),
            pl.BlockSpec((D_MODEL, LANE), lambda b, j: (0, 0)),
        ],
        out_specs=[
            pl.BlockSpec((1, SEQ, PROJ_TN), lambda b, j: (b, 0, j)),
            pl.BlockSpec((1, SEQ, LANE), lambda b, j: (b, 0, 0)),
        ],
        out_shape=[
            jax.ShapeDtypeStruct((B, SEQ, PROJ_COLS), bf16),
            jax.ShapeDtypeStruct((B, SEQ, LANE), f32),
        ],
        scratch_shapes=[pltpu.VMEM((SEQ, D_MODEL), bf16)],
        compiler_params=_cparams(("parallel", "arbitrary")),
        name="inproj",
    )(x, gain, w, wg)


def _gla_body(q_ref, k_ref, v_ref, og_ref, glow_ref, wh_ref, wl_ref, bg_ref, gain_ref, o_ref, la_scr, st_scr, stb_scr):
    C = GLA_CHUNK
    HK = GLA_HEADS * GLA_DK
    HV = GLA_HEADS * GLA_DV
    PR = 512
    NE = q_ref.shape[0]
    for e in range(NE):
        for c in range(SEQ // PR):
            rows = pl.ds(c * PR, PR)
            g = glow_ref[e, rows, :]
            gh = g.astype(bf16)
            gl = (g - gh.astype(f32)).astype(bf16)
            z = (jnp.dot(gh, wh_ref[...], preferred_element_type=f32)
                 + jnp.dot(gl, wh_ref[...], preferred_element_type=f32)
                 + jnp.dot(gh, wl_ref[...], preferred_element_type=f32)) + bg_ref[...]
            la_scr[e, rows, :] = (jnp.minimum(z, 0.0) - jnp.log1p(jnp.exp(-jnp.abs(z)))) * (1.0 / GLA_NORM)
    st_scr[...] = jnp.zeros_like(st_scr)
    stb_scr[...] = jnp.zeros_like(stb_scr)

    causal = (lax.broadcasted_iota(i32, (C, GLA_HEADS * C), 0)
              >= lax.broadcasted_iota(i32, (C, GLA_HEADS * C), 1) % C)
    tri = (lax.broadcasted_iota(i32, (C, C), 0) >= lax.broadcasted_iota(i32, (C, C), 1)).astype(bf16)
    lane_head = lax.broadcasted_iota(i32, (1, HK), 1) // GLA_DK
    gain = gain_ref[...]
    zero_v = jnp.zeros((C, GLA_DV), bf16)

    def chunk(n, carry):
        for e in range(NE):
            chunk_one(n, e)
        return carry

    def chunk_one(n, e):
        rows = pl.ds(pl.multiple_of(n * C, C), C)
        la = la_scr[e, rows, :]
        p1 = la.astype(bf16)
        r1 = la - p1.astype(f32)
        p2 = r1.astype(bf16)
        p3 = (r1 - p2.astype(f32)).astype(bf16)
        c3 = jnp.dot(tri, jnp.concatenate([p1, p2, p3], axis=1), preferred_element_type=f32)
        cum = (c3[:, :HK] + c3[:, HK:2 * HK]) + c3[:, 2 * HK:]
        cum_t = cum.T
        tot_t = cum_t[:, C - 1:C]
        qf = q_ref[e, rows, :].astype(f32)
        kf = k_ref[e, rows, :].astype(f32)
        qd = (qf * (GLA_DK ** -0.5) * jnp.exp(cum)).astype(bf16)
        ki = kf * jnp.exp(-cum)
        kd_t = (kf.T * jnp.exp(tot_t - cum_t)).astype(bf16)
        dec_t = jnp.exp(tot_t)
        vb = v_ref[e, rows, :]
        heads = range(GLA_HEADS)
        vcols = [vb[:, h * GLA_DV:(h + 1) * GLA_DV] for h in heads]
        k_st = jnp.concatenate([jnp.where(lane_head == h, ki, 0.0).astype(bf16) for h in heads], axis=0)
        v_bd = jnp.concatenate([jnp.concatenate([vcols[h] if g == h else zero_v for g in heads], axis=1)
                                for h in heads], axis=0)
        s = lax.dot_general(qd, k_st, NT, preferred_element_type=f32)
        s = jnp.where(causal, s, 0.0).astype(bf16)
        o = jnp.dot(s, v_bd, preferred_element_type=f32) + jnp.dot(qd, stb_scr[e], preferred_element_type=f32)
        ogf = og_ref[e, rows, :].astype(f32)
        for h in heads:
            cols = slice(h * GLA_DV, (h + 1) * GLA_DV)
            hr = slice(h * GLA_DK, (h + 1) * GLA_DK)
            st = st_scr[e, h] * dec_t[hr, :] + jnp.dot(kd_t[hr, :], vcols[h], preferred_element_type=f32)
            st_scr[e, h] = st
            stb_scr[e, hr, cols] = st.astype(bf16)
            oh = o[:, cols]
            oh = oh * lax.rsqrt(jnp.mean(oh * oh, axis=-1, keepdims=True) + RMS_EPS) * gain
            g = ogf[:, cols]
            o_ref[e, rows, cols] = (oh * (g * jax.nn.sigmoid(g))).astype(bf16)

    lax.fori_loop(0, SEQ // C, chunk, 0, unroll=2)


def _gla(proj, glow, wup, bg, gain):
    B = proj.shape[0]
    HK = GLA_HEADS * GLA_DK
    HV = GLA_HEADS * GLA_DV
    wup_hi = wup.astype(bf16)
    wup_lo = (wup - wup_hi.astype(f32)).astype(bf16)
    NE = GLA_NE if B % GLA_NE == 0 else 1
    return pl.pallas_call(
        _gla_body,
        grid=(B // NE,),
        in_specs=[
            pl.BlockSpec((NE, SEQ, HK), lambda b: (b, 0, Q_OFF // HK)),
            pl.BlockSpec((NE, SEQ, HK), lambda b: (b, 0, K_OFF // HK)),
            pl.BlockSpec((NE, SEQ, HV), lambda b: (b, 0, V_OFF // HV)),
            pl.BlockSpec((NE, SEQ, HV), lambda b: (b, 0, OG_OFF // HV)),
            pl.BlockSpec((NE, SEQ, LANE), lambda b: (b, 0, 0)),
            pl.BlockSpec((LANE, HK), lambda b: (0, 0)),
            pl.BlockSpec((LANE, HK), lambda b: (0, 0)),
            pl.BlockSpec((1, HK), lambda b: (0, 0)),
            pl.BlockSpec((1, GLA_DV), lambda b: (0, 0)),
        ],
        out_specs=pl.BlockSpec((NE, SEQ, HV), lambda b: (b, 0, 0)),
        out_shape=jax.ShapeDtypeStruct((B, SEQ, HV), bf16),
        scratch_shapes=[
            pltpu.VMEM((NE, SEQ, HK), f32),
            pltpu.VMEM((NE, GLA_HEADS, GLA_DK, GLA_DV), f32),
            pltpu.VMEM((NE, HK, HV), bf16),
        ],
        compiler_params=_cparams(("parallel",)),
        name="gla",
    )(proj, proj, proj, proj, glow, wup_hi, wup_lo, bg, gain)


LRU_SEG = 8
LRU_SEGLEN = SEQ // LRU_SEG


def _lru_body(xb_ref, yb_ref, cw_ref, cb_ref, w_ref, b_ref, lam_ref, o_ref, a_scr, u_scr, hs_scr, as_scr):
    x = xb_ref[0].astype(f32)
    row = lax.broadcasted_iota(i32, (SEQ, LANE), 0)
    xc = x * cw_ref[LRU_CONV - 1:LRU_CONV, :] + cb_ref[...]
    for j in range(1, LRU_CONV):
        xs = jnp.where(row >= j, pltpu.roll(x, j, 0), 0.0)
        xc = xc + xs * cw_ref[LRU_CONV - 1 - j:LRU_CONV - j, :]
    z = jnp.dot(xc.astype(bf16), w_ref[0], preferred_element_type=f32) + b_ref[0]
    r = jax.nn.sigmoid(z[:, :LANE])
    gi = jax.nn.sigmoid(z[:, LANE:])
    nl = -lam_ref[...]
    sp = jnp.maximum(nl, 0.0) + jnp.log1p(jnp.exp(-jnp.abs(nl)))
    log_a = (-LRU_C) * r * sp
    a = jnp.exp(log_a)
    u = jnp.sqrt(jnp.tanh(-log_a) * (a * a + 1.0)) * gi * xc
    a_scr[...] = a
    u_scr[...] = u

    def step(t, carry):
        h, acc = carry
        idx = pl.ds(t, LRU_SEG, stride=LRU_SEGLEN)
        at = a_scr[idx, :]
        h = at * h + u_scr[idx, :]
        acc = at * acc
        hs_scr[idx, :] = h
        as_scr[idx, :] = acc
        return h, acc

    hfin, afin = lax.fori_loop(0, LRU_SEGLEN, step, (jnp.zeros((LRU_SEG, LANE), f32), jnp.ones((LRU_SEG, LANE), f32)),
                               unroll=8)
    cin = jnp.zeros((1, LANE), f32)
    for s in range(LRU_SEG):
        rows = pl.ds(s * LRU_SEGLEN, LRU_SEGLEN)
        h = hs_scr[rows, :] + as_scr[rows, :] * cin
        o_ref[0, rows, :] = (h * jax.nn.gelu(yb_ref[0, rows, :].astype(f32))).astype(bf16)
        cin = hfin[s:s + 1, :] + afin[s:s + 1, :] * cin


def _lru(proj, cw, cb, w, b, lam):
    B = proj.shape[0]
    nt = LRU_W // LANE
    return pl.pallas_call(
        _lru_body,
        grid=(B, nt),
        in_specs=[
            pl.BlockSpec((1, SEQ, LANE), lambda b, t: (b, 0, XB_OFF // LANE + t)),
            pl.BlockSpec((1, SEQ, LANE), lambda b, t: (b, 0, YB_OFF // LANE + t)),
            pl.BlockSpec((LRU_CONV, LANE), lambda b, t: (0, t)),
            pl.BlockSpec((1, LANE), lambda b, t: (0, t)),
            pl.BlockSpec((1, LANE, 2 * LANE), lambda b, t: (t, 0, 0)),
            pl.BlockSpec((1, 1, 2 * LANE), lambda b, t: (t, 0, 0)),
            pl.BlockSpec((1, LANE), lambda b, t: (0, t)),
        ],
        out_specs=pl.BlockSpec((1, SEQ, LANE), lambda b, t: (b, 0, t)),
        out_shape=jax.ShapeDtypeStruct((B, SEQ, LRU_W), bf16),
        scratch_shapes=[pltpu.VMEM((SEQ, LANE), f32)] * 4,
        compiler_params=_cparams(("parallel", "parallel")),
        name="lru",
    )(proj, proj, cw, cb, w, b, lam)


def _rope_body(pos_ref, freq_ref, cos_ref, ss_ref):
    ang = pos_ref[0] * freq_ref[...]
    lane = lax.broadcasted_iota(i32, (SEQ, LANE), 1)
    sn = jnp.sin(ang)
    cos_ref[0] = jnp.cos(ang)
    ss_ref[0] = jnp.where(lane < ROPE_DIMS // 2, -sn, sn)


def _rope_tables(posf, freq):
    B = posf.shape[0]
    spec = pl.BlockSpec((1, SEQ, LANE), lambda b: (b, 0, 0))
    return pl.pallas_call(
        _rope_body,
        grid=(B,),
        in_specs=[pl.BlockSpec((1, SEQ, 1), lambda b: (b, 0, 0)), pl.BlockSpec((1, LANE), lambda b: (0, 0))],
        out_specs=[spec, spec],
        out_shape=[jax.ShapeDtypeStruct((B, SEQ, LANE), f32)] * 2,
        compiler_params=_cparams(("parallel",)),
        name="rope_tables",
    )(posf, freq)


def _rope_partner():
    half = ROPE_DIMS // 2
    perm = np.arange(LANE)
    perm[:half] += half
    perm[half:ROPE_DIMS] -= half
    return perm


ATTN_PC = 256
ATTN_G = 4


def _attn_body(q0, q1, q2, k0, k1, k2, v0, v1, v2, cos_ref, ss_ref, g_ref, p2_ref, o_ref,
               qd, kd, vd, tq, tk, tv, od, ld, on, ln, sc_a, sc_b):
    q_refs, k_refs, v_refs = (q0, q1, q2), (k0, k1, k2), (v0, v1, v2)
    BLK = ATTN_BLK
    qg = g_ref[0:1, :] * (DIL_DH ** -0.5)
    qgr = g_ref[1:2, :] * (DIL_DH ** -0.5)
    kg = g_ref[2:3, :]
    kgr = g_ref[3:4, :]
    kd[pl.ds(0, BLK), :] = jnp.zeros((BLK, LANE), bf16)
    vd[pl.ds(0, BLK), pl.ds(0, LANE)] = jnp.zeros((BLK, LANE), bf16)
    vd[:, pl.ds(LANE, LANE)] = jnp.ones((SEQ + BLK, LANE), bf16)

    qi = lax.broadcasted_iota(i32, (BLK, 2 * BLK), 0)
    kj = lax.broadcasted_iota(i32, (BLK, 2 * BLK), 1)
    is_cur = kj >= BLK
    band = jnp.where(is_cur, kj - BLK, qi) <= jnp.where(is_cur, qi, kj)

    def norm_rope(raw, partner, gain, gain_p, cs, ss):
        rf = raw.astype(f32)
        r = lax.rsqrt(jnp.mean(rf * rf, axis=-1, keepdims=True) + RMS_EPS)
        return (rf * (gain * cs) + partner * (gain_p * ss)) * r

    for p, (_, dil) in enumerate(DIL_PATTERNS):
        L = SEQ // dil
        nbl = L // BLK

        def prep(c, carry, p=p, dil=dil):
            rows = pl.ds(pl.multiple_of(c * ATTN_PC, ATTN_PC), ATTN_PC)
            cs, ss = cos_ref[0, rows, :], ss_ref[0, rows, :]
            qr = q_refs[p][0, rows, :]
            kr = k_refs[p][0, rows, :]
            partner = jnp.dot(jnp.concatenate([qr, kr], axis=1), p2_ref[...], preferred_element_type=f32)
            q = norm_rope(qr, partner[:, :LANE], qg, qgr, cs, ss)
            k = norm_rope(kr, partner[:, LANE:], kg, kgr, cs, ss)
            if dil == 1:
                orow = pl.ds(pl.multiple_of(c * ATTN_PC, ATTN_PC) + BLK, ATTN_PC)
                qd[rows, :] = q.astype(bf16)
                kd[orow, :] = k.astype(bf16)
                vd[orow, pl.ds(0, LANE)] = v_refs[p][0, rows, :]
            else:
                tq[rows, :] = q
                tk[rows, :] = k
                tv[rows, :] = v_refs[p][0, rows, :].astype(f32)
            return carry

        lax.fori_loop(0, SEQ // ATTN_PC, prep, 0, unroll=2)
        if dil > 1:
            for r in range(dil):
                src = pl.ds(r, L, stride=dil)
                qd[pl.ds(r * L, L), :] = tq[src, :].astype(bf16)
                kd[pl.ds(BLK + r * L, L), :] = tk[src, :].astype(bf16)
                vd[pl.ds(BLK + r * L, L), pl.ds(0, LANE)] = tv[src, :].astype(bf16)

        o_dst, l_dst = (on.at[p], ln.at[p]) if dil == 1 else (od, ld)

        def scores(g, dst):
            for j in range(ATTN_G):
                r0 = pl.multiple_of((g * ATTN_G + j) * BLK, BLK)
                dst[pl.ds(j * BLK, BLK), :] = lax.dot_general(qd[pl.ds(r0, BLK), :], kd[pl.ds(r0, 2 * BLK), :], NT,
                                                              preferred_element_type=f32)

        def finish(g, src, nbl=nbl, o_dst=o_dst, l_dst=l_dst):
            for j in range(ATTN_G):
                b = g * ATTN_G + j
                r0 = pl.multiple_of(b * BLK, BLK)
                first = (b % nbl) == 0
                valid = band & (is_cur | jnp.logical_not(first))
                s = jnp.where(valid, src[pl.ds(j * BLK, BLK), :], -jnp.inf)
                m = jnp.max(s, axis=-1, keepdims=True)
                e = jnp.exp(s - m).astype(bf16)
                ov = jnp.dot(e, vd[pl.ds(r0, 2 * BLK), :], preferred_element_type=f32)
                den = ov[:, LANE:]
                o_dst[pl.ds(r0, BLK), :] = ov[:, :LANE] / den
                l_dst[pl.ds(r0, BLK), :] = m + jnp.log(den)

        n_groups = SEQ // BLK // ATTN_G
        scores(0, sc_a)

        def pair(i, carry, scores=scores, finish=finish):
            scores(2 * i + 1, sc_b)
            finish(2 * i, sc_a)
            scores(jnp.minimum(2 * i + 2, n_groups - 1), sc_a)
            finish(2 * i + 1, sc_b)
            return carry

        lax.fori_loop(0, n_groups // 2, pair, 0)
        if dil > 1:
            for r in range(dil):
                dst = pl.ds(r, L, stride=dil)
                on[p, dst, :] = od[pl.ds(r * L, L), :]
                ln[p, dst, :] = ld[pl.ds(r * L, L), :]

    def combine(c, carry):
        rows = pl.ds(pl.multiple_of(c * ATTN_PC, ATTN_PC), ATTN_PC)
        l0, l1, l2 = ln[0, rows, :], ln[1, rows, :], ln[2, rows, :]
        mx = jnp.maximum(jnp.maximum(l0, l1), l2)
        w0, w1, w2 = jnp.exp(l0 - mx), jnp.exp(l1 - mx), jnp.exp(l2 - mx)
        o = (w0 * on[0, rows, :] + w1 * on[1, rows, :] + w2 * on[2, rows, :]) / (w0 + w1 + w2)
        o_ref[0, rows, :] = o.astype(bf16)
        return carry

    lax.fori_loop(0, SEQ // ATTN_PC, combine, 0)


def _attn(proj, cos, ss, qg, kg, b0):
    B = proj.shape[0]
    base = DIL_OFF // LANE
    perm = _rope_partner()
    gains = jnp.stack([qg, qg[perm], kg, kg[perm]]).astype(f32)
    pm = np.zeros((LANE, LANE), np.float32)
    pm[perm[:ROPE_DIMS], np.arange(ROPE_DIMS)] = 1.0
    p2 = jnp.asarray(np.kron(np.eye(2, dtype=np.float32), pm), dtype=bf16)

    def pspec(qkv, p):
        return pl.BlockSpec((1, SEQ, LANE), lambda b, h, o=base + (qkv * N_PAT + p) * DIL_HEADS: (b, 0, o + h))

    tspec = pl.BlockSpec((1, SEQ, LANE), lambda b, h: (b + b0, 0, 0))
    in_specs = [pspec(qkv, p) for qkv in range(3) for p in range(N_PAT)] + [
        tspec, tspec, pl.BlockSpec((4, LANE), lambda b, h: (0, 0)), pl.BlockSpec((2 * LANE, 2 * LANE), lambda b, h: (0, 0))]
    return pl.pallas_call(
        _attn_body,
        grid=(B, DIL_HEADS),
        in_specs=in_specs,
        out_specs=pl.BlockSpec((1, SEQ, LANE), lambda b, h: (b, 0, h)),
        out_shape=jax.ShapeDtypeStruct((B, SEQ, DIL_HEADS * DIL_DH), bf16),
        scratch_shapes=[
            pltpu.VMEM((SEQ, LANE), bf16),
            pltpu.VMEM((SEQ + ATTN_BLK, LANE), bf16),
            pltpu.VMEM((SEQ + ATTN_BLK, 2 * LANE), bf16),
            pltpu.VMEM((SEQ, LANE), f32),
            pltpu.VMEM((SEQ, LANE), f32),
            pltpu.VMEM((SEQ, LANE), f32),
            pltpu.VMEM((SEQ, LANE), f32),
            pltpu.VMEM((SEQ, LANE), f32),
            pltpu.VMEM((N_PAT, SEQ, LANE), f32),
            pltpu.VMEM((N_PAT, SEQ, LANE), f32),
            pltpu.VMEM((ATTN_G * ATTN_BLK, 2 * ATTN_BLK), f32),
            pltpu.VMEM((ATTN_G * ATTN_BLK, 2 * ATTN_BLK), f32),
        ],
        compiler_params=_cparams(("parallel", "arbitrary")),
        name="dilated_attn",
    )(*([proj] * 9), cos, ss, gains, p2)


def _merge_body(a_ref, b_ref, c_ref, g0_ref, g1_ref, g2_ref, x_ref, wb_ref, wo_ref, n2_ref, rw_ref, rb_ref, tri_ref,
                x1_ref, h2_ref, ei_ref, ew_ref, pos_ref, cnt_ref, cb_ref, cnt_scr):
    for c in range(MERGE_TM // MERGE_RC):
        rows = pl.ds(c * MERGE_RC, MERGE_RC)
        merged = None
        for n, (br, g) in enumerate(((a_ref, g0_ref), (b_ref, g1_ref), (c_ref, g2_ref))):
            gate = 0.5 * jnp.tanh(0.5 * g[rows, :].astype(f32)) + 0.5
            t = gate * jnp.dot(br[rows, :], wb_ref[n], preferred_element_type=f32)
            merged = t if merged is None else merged + t
        x1 = x_ref[rows, :] + jnp.dot(merged.astype(bf16), wo_ref[...], preferred_element_type=f32)
        x1_ref[rows, :] = x1
        h2_ref[rows, :] = ((x1 * lax.rsqrt(jnp.mean(x1 * x1, axis=-1, keepdims=True) + RMS_EPS))
                           * n2_ref[...]).astype(bf16)
    h2 = h2_ref[...]

    lg = lax.dot_general(rw_ref[...], h2, NT, preferred_element_type=f32) + rb_ref[...]
    ex = jnp.exp(lg - jnp.max(lg, axis=0, keepdims=True))
    sc = ex / jnp.sum(ex, axis=0, keepdims=True)
    srow = [sc[e:e + 1, :] for e in range(N_EXP)]
    best = bidx = None
    for gidx in range(N_GRP):
        s0, s1, s2, s3 = srow[4 * gidx:4 * gidx + 4]
        hi01, lo01, hi23, lo23 = jnp.maximum(s0, s1), jnp.minimum(s0, s1), jnp.maximum(s2, s3), jnp.minimum(s2, s3)
        gs = jnp.maximum(hi01, hi23) + jnp.maximum(jnp.minimum(hi01, hi23), jnp.maximum(lo01, lo23))
        if best is None:
            best, bidx = gs, jnp.zeros_like(gs, dtype=i32)
        else:
            better = gs > best
            best = jnp.where(better, gs, best)
            bidx = jnp.where(better, gidx, bidx)
    cand = []
    for i in range(EXP_PER_GRP):
        v = srow[i]
        for gidx in range(1, N_GRP):
            v = jnp.where(bidx == gidx, srow[4 * gidx + i], v)
        cand.append(v)

    def argmax4(vals):
        bw, bi = vals[0], jnp.zeros_like(bidx)
        for i in range(1, EXP_PER_GRP):
            better = vals[i] > bw
            bw = jnp.where(better, vals[i], bw)
            bi = jnp.where(better, i, bi)
        return bw, bi

    w1, i1 = argmax4(cand)
    w2, i2 = argmax4([jnp.where(i1 == i, -jnp.inf, cand[i]) for i in range(EXP_PER_GRP)])
    tot = w1 + w2
    e1 = bidx * EXP_PER_GRP + i1
    e2 = bidx * EXP_PER_GRP + i2
    ei_ref[0:1, :] = e1
    ei_ref[1:2, :] = e2
    ew_ref[0:1, :] = w1 / tot
    ew_ref[1:2, :] = w2 / tot

    @pl.when(pl.program_id(0) == 0)
    def _():
        cnt_scr[...] = jnp.zeros_like(cnt_scr)

    erow = lax.broadcasted_iota(i32, (N_EXP, MERGE_TM), 0)
    oh1 = erow == e1
    oh2 = erow == e2
    both = jnp.where(oh1 | oh2, 1.0, 0.0)
    incl = jnp.dot(both.astype(bf16), tri_ref[...], preferred_element_type=f32)
    cb_ref[0] = cnt_scr[...]
    base = cnt_scr[...] + (incl - both)
    pos_ref[0:1, :] = jnp.sum(jnp.where(oh1, base, 0.0), axis=0, keepdims=True).astype(i32)
    pos_ref[1:2, :] = jnp.sum(jnp.where(oh2, base, 0.0), axis=0, keepdims=True).astype(i32)
    cnt = cnt_scr[...] + incl[:, MERGE_TM - 1:MERGE_TM]
    cnt_scr[...] = cnt
    cnt_ref[...] = cnt


def _merge(bra, brb, brc, proj2d, x2d, x_row0, wb, wo, n2, rw, rb):
    T = bra.shape[0]
    TM = MERGE_TM
    x_blk0 = x_row0 // TM
    rspec = lambda w: pl.BlockSpec((TM, w), lambda i: (i, 0))
    full = lambda shape: pl.BlockSpec(shape, lambda i: (0,) * len(shape))
    tri = (jnp.arange(TM)[:, None] <= jnp.arange(TM)[None, :]).astype(bf16)
    return pl.pallas_call(
        _merge_body,
        grid=(T // TM,),
        in_specs=[
            rspec(BR_W), rspec(BR_W), rspec(BR_W),
            pl.BlockSpec((TM, D_MODEL), lambda i: (i, GATE_OFF // D_MODEL + 0)),
            pl.BlockSpec((TM, D_MODEL), lambda i: (i, GATE_OFF // D_MODEL + 1)),
            pl.BlockSpec((TM, D_MODEL), lambda i: (i, GATE_OFF // D_MODEL + 2)),
            pl.BlockSpec((TM, D_MODEL), lambda i: (i + x_blk0, 0)),
            full((N_BR, BR_W, D_MODEL)), full((D_MODEL, D_MODEL)), full((1, D_MODEL)),
            full((N_EXP, D_MODEL)), full((N_EXP, 1)), full((TM, TM)),
        ],
        out_specs=[
            rspec(D_MODEL), rspec(D_MODEL),
            pl.BlockSpec((TOP_K, TM), lambda i: (0, i)),
            pl.BlockSpec((TOP_K, TM), lambda i: (0, i)),
            pl.BlockSpec((TOP_K, TM), lambda i: (0, i)),
            full((N_EXP, 1)),
            pl.BlockSpec((1, N_EXP, 1), lambda i: (i, 0, 0)),
        ],
        out_shape=[
            jax.ShapeDtypeStruct((T, D_MODEL), f32),
            jax.ShapeDtypeStruct((T, D_MODEL), bf16),
            jax.ShapeDtypeStruct((TOP_K, T), i32),
            jax.ShapeDtypeStruct((TOP_K, T), f32),
            jax.ShapeDtypeStruct((TOP_K, T), i32),
            jax.ShapeDtypeStruct((N_EXP, 1), f32),
            jax.ShapeDtypeStruct((T // TM, N_EXP, 1), f32),
        ],
        scratch_shapes=[pltpu.VMEM((N_EXP, 1), f32)],
        compiler_params=_cparams(("arbitrary",)),
        name="merge_router",
    )(bra, brb, brc, proj2d, proj2d, proj2d, x2d, wb, wo, n2, rw, rb, tri)


def _expert_body(be_ref, nu_ref, xs_ref, wgu_ref, wd_ref, ys_ref):
    i = pl.program_id(0)

    @pl.when(i < nu_ref[0])
    def _():
        gu = jnp.dot(xs_ref[...], wgu_ref[0], preferred_element_type=f32)
        g = gu[:, :EXP_FF]
        act = (g * jax.nn.sigmoid(g)) * gu[:, EXP_FF:]
        ys_ref[...] = jnp.dot(act.astype(bf16), wd_ref[0], preferred_element_type=f32).astype(bf16)

    @pl.when(i >= nu_ref[0])
    def _():
        ys_ref[...] = jnp.zeros_like(ys_ref)


def _experts(block_e, n_used, xs, wgu, wd):
    n_rows = xs.shape[0]
    grid_spec = pltpu.PrefetchScalarGridSpec(
        num_scalar_prefetch=2,
        grid=(n_rows // MOE_BM,),
        in_specs=[
            pl.BlockSpec((MOE_BM, D_MODEL), lambda i, be, nu: (i, 0)),
            pl.BlockSpec((1, D_MODEL, 2 * EXP_FF), lambda i, be, nu: (be[i], 0, 0)),
            pl.BlockSpec((1, EXP_FF, D_MODEL), lambda i, be, nu: (be[i], 0, 0)),
        ],
        out_specs=pl.BlockSpec((MOE_BM, D_MODEL), lambda i, be, nu: (i, 0)),
    )
    return pl.pallas_call(
        _expert_body,
        grid_spec=grid_spec,
        out_shape=jax.ShapeDtypeStruct((n_rows, D_MODEL), bf16),
        compiler_params=_cparams(("arbitrary",)),
        name="experts",
    )(block_e, n_used, xs, wgu, wd)


COMB_TM = 1024


def _combine_body(x1_ref, y0_ref, y1_ref, w_ref, *rest):
    o_ref = rest[-1]
    w = w_ref[...]
    o_ref[...] = x1_ref[...] + y0_ref[...].astype(f32) * w[:, 0:1] + y1_ref[...].astype(f32) * w[:, 1:2]


def _combine(x1, y0, y1, w, out_rows=None, row0=0, out_prev=None):
    T = x1.shape[0]
    out_rows = T if out_rows is None else out_rows
    blk0 = row0 // COMB_TM
    rspec = pl.BlockSpec((COMB_TM, D_MODEL), lambda i: (i, 0))
    in_specs = [rspec, rspec, rspec, pl.BlockSpec((COMB_TM, TOP_K), lambda i: (i, 0))]
    args = [x1, y0, y1, w]
    aliases = {}
    if out_prev is not None:
        in_specs.append(pl.BlockSpec(memory_space=pl.ANY))
        args.append(out_prev)
        aliases = {4: 0}
    return pl.pallas_call(
        _combine_body,
        grid=(T // COMB_TM,),
        in_specs=in_specs,
        out_specs=pl.BlockSpec((COMB_TM, D_MODEL), lambda i: (i + blk0, 0)),
        out_shape=jax.ShapeDtypeStruct((out_rows, D_MODEL), f32),
        input_output_aliases=aliases,
        compiler_params=_cparams(("parallel",)),
        name="moe_combine",
    )(*args)


DISP_ALIGN = 16
DISP_SLOTS = MERGE_TM * TOP_K + N_EXP * DISP_ALIGN
DISP_SIZES = tuple(DISP_ALIGN << s for s in range(5, -1, -1))


def _dispatch_body(off_ref, n_ref, ls_ref, h2_ref, ei_ref, pos_ref, adj_ref, xs0_hbm, xs_hbm, srt, sem):
    del xs0_hbm
    i = pl.program_id(0)
    TM = MERGE_TM
    erow = lax.broadcasted_iota(i32, (N_EXP, TM), 0)
    adj = adj_ref[0]

    def slot(k):
        sel = jnp.sum(jnp.where(erow == ei_ref[k:k + 1, :], adj, 0.0), axis=0, keepdims=True)
        return pos_ref[k:k + 1, :] + sel.astype(i32)

    srow = lax.broadcasted_iota(i32, (DISP_SLOTS, TM), 0)
    perm = jnp.where((srow == slot(0)) | (srow == slot(1)), 1.0, 0.0).astype(bf16)
    srt[...] = jnp.dot(perm, h2_ref[...], preferred_element_type=f32).astype(bf16)

    def slab_copies(start):
        for e in range(N_EXP):
            n = n_ref[i * N_EXP + e]
            ls = ls_ref[i * N_EXP + e]
            off = off_ref[i * N_EXP + e]
            done = jnp.int32(0)
            for sz in DISP_SIZES:
                take = (n & sz) != 0

                @pl.when(take)
                def _(ls=ls, off=off, done=done, sz=sz):
                    cp = pltpu.make_async_copy(srt.at[pl.ds(pl.multiple_of(ls + done, DISP_ALIGN), sz)],
                                               xs_hbm.at[pl.ds(pl.multiple_of(off + done, DISP_ALIGN), sz)], sem)
                    if start:
                        cp.start()
                    else:
                        cp.wait()

                done = done + jnp.where(take, sz, 0)

    slab_copies(True)
    slab_copies(False)


def _dispatch(off, n16, lstart, h2, eidx, pos, adj, n_rows):
    T = h2.shape[0]
    TM = MERGE_TM
    xs0 = jnp.zeros((n_rows, D_MODEL), bf16)
    grid_spec = pltpu.PrefetchScalarGridSpec(
        num_scalar_prefetch=3,
        grid=(T // TM,),
        in_specs=[
            pl.BlockSpec((TM, D_MODEL), lambda i, *_: (i, 0)),
            pl.BlockSpec((TOP_K, TM), lambda i, *_: (0, i)),
            pl.BlockSpec((TOP_K, TM), lambda i, *_: (0, i)),
            pl.BlockSpec((1, N_EXP, 1), lambda i, *_: (i, 0, 0)),
            pl.BlockSpec(memory_space=pl.ANY),
        ],
        out_specs=pl.BlockSpec(memory_space=pl.ANY),
        scratch_shapes=[pltpu.VMEM((DISP_SLOTS, D_MODEL), bf16), pltpu.SemaphoreType.DMA(())],
    )
    return pl.pallas_call(
        _dispatch_body,
        grid_spec=grid_spec,
        out_shape=jax.ShapeDtypeStruct((n_rows, D_MODEL), bf16),
        input_output_aliases={7: 0},
        compiler_params=_cparams(("arbitrary",)),
        name="moe_dispatch",
    )(off, n16, lstart, h2, eidx, pos, adj, xs0)


def _moe(x1, h2, eidx, ew, pos, cnt, cb, wgu, wd, **out_kw):
    T = x1.shape[0]
    TM = MERGE_TM
    n_tiles = T // TM
    ceil_to = lambda v, m: (v + m - 1) // m * m
    n_rows = ceil_to(T * TOP_K + n_tiles * N_EXP * (DISP_ALIGN - 1) + N_EXP * (MOE_BM - 1), MOE_BM)
    n_blocks = n_rows // MOE_BM
    cbi = cb[:, :, 0].astype(i32)
    tile_cnt = jnp.concatenate([cbi[1:], cnt[:, 0].astype(i32)[None, :]], axis=0) - cbi
    n16 = ceil_to(tile_cnt, DISP_ALIGN)
    lstart = jnp.cumsum(n16, axis=1) - n16
    before = jnp.cumsum(n16, axis=0) - n16
    padded = ceil_to(jnp.sum(n16, axis=0), MOE_BM)
    pad_ends = jnp.cumsum(padded)
    off = (pad_ends - padded)[None, :] + before
    adj = (lstart - cbi).astype(f32)[:, :, None]
    row_adj = jnp.repeat(off - cbi, TM, axis=0).T
    dest = pos
    for e in range(N_EXP):
        dest = dest + jnp.where(eidx == e, row_adj[e][None, :], 0)
    blk_start = jnp.arange(n_blocks, dtype=i32) * MOE_BM
    block_e = jnp.minimum(jnp.sum((blk_start[:, None] >= pad_ends[None, :]).astype(i32), axis=1), N_EXP - 1)
    n_used = (pad_ends[-1:] // MOE_BM).astype(i32)
    xs = _dispatch(off.reshape(-1), n16.reshape(-1), lstart.reshape(-1), h2, eidx, pos, adj, n_rows)
    ys = _experts(block_e, n_used, xs, wgu, wd)
    return _combine(x1, ys[dest[0]], ys[dest[1]], ew.T, **out_kw)


def _pack_w_in(w):
    cuts = np.cumsum([256, 256, 512, 512, 16, 512, 512, 4608, 3072])[:-1].tolist()
    q, k, v, og, glow, xb, yb, qkv, gates = jnp.split(w, cuts, axis=-1)
    packed = jnp.concatenate([gates, qkv, xb, yb, q, k, v, og], axis=-1).astype(bf16)
    wg = jnp.pad(glow, ((0, 0), (0, LANE - GLA_RANK))).astype(bf16)
    return packed, wg


def _block_diag_lru(w_a, w_x):
    bs = LRU_W // LRU_BLOCKS
    per = LANE // bs
    tiles = []
    for t in range(LRU_W // LANE):
        halves = []
        for w in (w_a, w_x):
            m = jnp.zeros((LANE, LANE), f32)
            for j in range(per):
                m = m.at[j * bs:(j + 1) * bs, j * bs:(j + 1) * bs].set(w[t * per + j])
            halves.append(m)
        tiles.append(jnp.concatenate(halves, axis=1))
    return jnp.stack(tiles).astype(bf16)


def kernel(x, positions, norm1_gain, w_in, gla_gate_up, gla_gate_bias, gla_out_gain, lru_conv_w, lru_conv_b, lru_w_a, lru_b_a, lru_w_x, lru_b_x, lru_lambda, q_norm_gain, k_norm_gain, w_branch, w_out, norm2_gain, router_w, router_b, w_gate, w_up, w_down):
    B, S, D = x.shape
    assert S == SEQ and D == D_MODEL
    T = B * S
    depth = w_in.shape[0]

    half = ROPE_DIMS // 2
    inv_freq = ROPE_THETA ** (-jnp.arange(half, dtype=f32) / half)
    freq = jnp.zeros((1, LANE), f32).at[0, :ROPE_DIMS].set(jnp.concatenate([inv_freq, inv_freq]))
    cos, ss = _rope_tables(positions.astype(f32)[:, :, None], freq)

    rw = router_w.T.astype(bf16)
    rb = router_b.astype(f32)[:, None]
    n_str = N_STREAMS if B % N_STREAMS == 0 else 1
    Bs = B // n_str
    Ts = Bs * S
    xs_ = [x] * n_str
    out = None
    for l in range(depth):
        w_packed, wg = _pack_w_in(w_in[l])
        wup = jnp.pad(gla_gate_up[l], ((0, LANE - GLA_RANK), (0, 0)))
        nt = LRU_W // LANE
        lru_w = _block_diag_lru(lru_w_a[l], lru_w_x[l])
        lru_b = jnp.concatenate([lru_b_a[l].reshape(nt, 1, LANE), lru_b_x[l].reshape(nt, 1, LANE)], axis=-1)
        wb, wo = w_branch[l].astype(bf16), w_out[l].astype(bf16)
        wgu = jnp.concatenate([w_gate[l], w_up[l]], axis=-1).astype(bf16)
        wd = w_down[l].astype(bf16)
        last = l == depth - 1
        for s in range(n_str):
            b0 = s * Bs
            xin = xs_[s]
            shared_in = xin.shape[0] == B and n_str > 1
            proj, glow = _inproj(xin, norm1_gain[l][None, :], w_packed, wg, b0 if shared_in else 0, Bs)
            br_a = _gla(proj, glow, wup, gla_gate_bias[l][None, :], gla_out_gain[l][None, :])
            br_b = _lru(proj, lru_conv_w[l], lru_conv_b[l][None, :], lru_w, lru_b, lru_lambda[l][None, :])
            br_c = _attn(proj, cos, ss, q_norm_gain[l], k_norm_gain[l], b0)
            routed = _merge(br_a.reshape(Ts, BR_W), br_b.reshape(Ts, BR_W), br_c.reshape(Ts, BR_W),
                            proj.reshape(Ts, PROJ_COLS), xin.reshape(-1, D),
                            b0 * S if shared_in else 0, wb, wo, norm2_gain[l][None, :], rw, rb)
            if last:
                out = _moe(*routed, wgu, wd, out_rows=T, row0=b0 * S, out_prev=out)
            else:
                xs_[s] = _moe(*routed, wgu, wd).reshape(Bs, S, D)
    return out.reshape(B, S, D)
```

```python
import functools

import jax
import jax.numpy as jnp
import numpy as np
from jax import lax
from jax.experimental import pallas as pl
from jax.experimental.pallas import tpu as pltpu

f32 = jnp.float32
bf16 = jnp.bfloat16
i32 = jnp.int32

D_MODEL = 1024
SEQ = 2048
RMS_EPS = 1e-6
GLA_HEADS, GLA_DK, GLA_DV, GLA_RANK, GLA_NORM, GLA_CHUNK = 4, 64, 128, 16, 16.0, 64
LRU_W, LRU_BLOCKS, LRU_CONV, LRU_C = 512, 8, 4, 8.0
DIL_PATTERNS = ((128, 1), (512, 4), (2048, 16))
N_PAT, DIL_HEADS, DIL_DH, ATTN_BLK = 3, 4, 128, 128
ROPE_THETA, ROPE_DIMS = 500000.0, 32
N_BR, BR_W = 3, 512
N_EXP, N_GRP, EXP_PER_GRP, TOP_K, EXP_FF = 16, 4, 4, 2, 512

LANE = 128
GATE_OFF, DIL_OFF, XB_OFF, YB_OFF, Q_OFF, K_OFF, V_OFF, OG_OFF = 0, 3072, 7680, 8192, 8704, 8960, 9216, 9728
PROJ_COLS = 10240
PROJ_TN = 1024
PROJ_RC = 512
MERGE_TM = 512
MERGE_RC = 512
MOE_BM = 512
N_STREAMS = 1
GLA_NE = 1
VMEM_LIMIT = 56 * 1024 * 1024

HIGHEST = lax.Precision.HIGHEST
NT = (((1,), (1,)), ((), ()))


def _cparams(sem):
    return pltpu.CompilerParams(dimension_semantics=sem, vmem_limit_bytes=VMEM_LIMIT)


def _inproj_body(x_ref, g_ref, w_ref, wg_ref, proj_ref, glow_ref, h_scr):
    j = pl.program_id(1)
    nrc = SEQ // PROJ_RC

    @pl.when(j == 0)
    def _():
        for c in range(nrc):
            rows = pl.ds(c * PROJ_RC, PROJ_RC)
            x = x_ref[0, rows, :]
            ms = jnp.mean(x * x, axis=-1, keepdims=True)
            h = ((x * lax.rsqrt(ms + RMS_EPS)) * g_ref[...]).astype(bf16)
            h_scr[rows, :] = h
            glow_ref[0, rows, :] = jnp.dot(h, wg_ref[...], preferred_element_type=f32)

    for c in range(nrc):
        rows = pl.ds(c * PROJ_RC, PROJ_RC)
        proj_ref[0, rows, :] = jnp.dot(h_scr[rows, :], w_ref[...], preferred_element_type=f32).astype(bf16)


def _inproj(x, gain, w, wg, b0, B):
    nj = PROJ_COLS // PROJ_TN
    return pl.pallas_call(
        _inproj_body,
        grid=(B, nj),
        in_specs=[
            pl.BlockSpec((1, SEQ, D_MODEL), lambda b, j: (b + b0, 0, 0)),
            pl.BlockSpec((1, D_MODEL), lambda b, j: (0, 0)),
            pl.BlockSpec((D_MODEL, PROJ_TN), lambda b, j: (0, j)),
            pl.BlockSpec((D_MODEL, LANE), lambda b, j: (0, 0)),
        ],
        out_specs=[
            pl.BlockSpec((1, SEQ, PROJ_TN), lambda b, j: (b, 0, j)),
            pl.BlockSpec((1, SEQ, LANE), lambda b, j: (b, 0, 0)),
        ],
        out_shape=[
            jax.ShapeDtypeStruct((B, SEQ, PROJ_COLS), bf16),
            jax.ShapeDtypeStruct((B, SEQ, LANE), f32),
        ],
        scratch_shapes=[pltpu.VMEM((SEQ, D_MODEL), bf16)],
        compiler_params=_cparams(("parallel", "arbitrary")),
        name="inproj",
    )(x, gain, w, wg)


def _gla_body(q_ref, k_ref, v_ref, og_ref, glow_ref, wh_ref, wl_ref, bg_ref, gain_ref, o_ref, la_scr, st_scr, stb_scr):
    C = GLA_CHUNK
    HK = GLA_HEADS * GLA_DK
    HV = GLA_HEADS * GLA_DV
    PR = 512
    NE = q_ref.shape[0]
    for e in range(NE):
        for c in range(SEQ // PR):
            rows = pl.ds(c * PR, PR)
            g = glow_ref[e, rows, :]
            gh = g.astype(bf16)
            gl = (g - gh.astype(f32)).astype(bf16)
            z = (jnp.dot(gh, wh_ref[...], preferred_element_type=f32)
                 + jnp.dot(gl, wh_ref[...], preferred_element_type=f32)
                 + jnp.dot(gh, wl_ref[...], preferred_element_type=f32)) + bg_ref[...]
            la_scr[e, rows, :] = (jnp.minimum(z, 0.0) - jnp.log1p(jnp.exp(-jnp.abs(z)))) * (1.0 / GLA_NORM)
    st_scr[...] = jnp.zeros_like(st_scr)
    stb_scr[...] = jnp.zeros_like(stb_scr)

    causal = (lax.broadcasted_iota(i32, (C, GLA_HEADS * C), 0)
              >= lax.broadcasted_iota(i32, (C, GLA_HEADS * C), 1) % C)
    tri = (lax.broadcasted_iota(i32, (C, C), 0) >= lax.broadcasted_iota(i32, (C, C), 1)).astype(bf16)
    lane_head = lax.broadcasted_iota(i32, (1, HK), 1) // GLA_DK
    gain = gain_ref[...]
    zero_v = jnp.zeros((C, GLA_DV), bf16)

    def chunk(n, carry):
        for e in range(NE):
            chunk_one(n, e)
        return carry

    def chunk_one(n, e):
        rows = pl.ds(pl.multiple_of(n * C, C), C)
        la = la_scr[e, rows, :]
        p1 = la.astype(bf16)
        r1 = la - p1.astype(f32)
        p2 = r1.astype(bf16)
        p3 = (r1 - p2.astype(f32)).astype(bf16)
        c3 = jnp.dot(tri, jnp.concatenate([p1, p2, p3], axis=1), preferred_element_type=f32)
        cum = (c3[:, :HK] + c3[:, HK:2 * HK]) + c3[:, 2 * HK:]
        cum_t = cum.T
        tot_t = cum_t[:, C - 1:C]
        qf = q_ref[e, rows, :].astype(f32)
        kf = k_ref[e, rows, :].astype(f32)
        qd = (qf * (GLA_DK ** -0.5) * jnp.exp(cum)).astype(bf16)
        ki = kf * jnp.exp(-cum)
        kd_t = (kf.T * jnp.exp(tot_t - cum_t)).astype(bf16)
        dec_t = jnp.exp(tot_t)
        vb = v_ref[e, rows, :]
        heads = range(GLA_HEADS)
        vcols = [vb[:, h * GLA_DV:(h + 1) * GLA_DV] for h in heads]
        k_st = jnp.concatenate([jnp.where(lane_head == h, ki, 0.0).astype(bf16) for h in heads], axis=0)
        v_bd = jnp.concatenate([jnp.concatenate([vcols[h] if g == h else zero_v for g in heads], axis=1)
                                for h in heads], axis=0)
        s = lax.dot_general(qd, k_st, NT, preferred_element_type=f32)
        s = jnp.where(causal, s, 0.0).astype(bf16)
        o = jnp.dot(s, v_bd, preferred_element_type=f32) + jnp.dot(qd, stb_scr[e], preferred_element_type=f32)
        ogf = og_ref[e, rows, :].astype(f32)
        for h in heads:
            cols = slice(h * GLA_DV, (h + 1) * GLA_DV)
            hr = slice(h * GLA_DK, (h + 1) * GLA_DK)
            st = st_scr[e, h] * dec_t[hr, :] + jnp.dot(kd_t[hr, :], vcols[h], preferred_element_type=f32)
            st_scr[e, h] = st
            stb_scr[e, hr, cols] = st.astype(bf16)
            oh = o[:, cols]
            oh = oh * lax.rsqrt(jnp.mean(oh * oh, axis=-1, keepdims=True) + RMS_EPS) * gain
            g = ogf[:, cols]
            o_ref[e, rows, cols] = (oh * (g * jax.nn.sigmoid(g))).astype(bf16)

    lax.fori_loop(0, SEQ // C, chunk, 0, unroll=4 // NE)


def _gla(proj, glow, wup, bg, gain):
    B = proj.shape[0]
    HK = GLA_HEADS * GLA_DK
    HV = GLA_HEADS * GLA_DV
    wup_hi = wup.astype(bf16)
    wup_lo = (wup - wup_hi.astype(f32)).astype(bf16)
    NE = GLA_NE if B % GLA_NE == 0 else 1
    return pl.pallas_call(
        _gla_body,
        grid=(B // NE,),
        in_specs=[
            pl.BlockSpec((NE, SEQ, HK), lambda b: (b, 0, Q_OFF // HK)),
            pl.BlockSpec((NE, SEQ, HK), lambda b: (b, 0, K_OFF // HK)),
            pl.BlockSpec((NE, SEQ, HV), lambda b: (b, 0, V_OFF // HV)),
            pl.BlockSpec((NE, SEQ, HV), lambda b: (b, 0, OG_OFF // HV)),
            pl.BlockSpec((NE, SEQ, LANE), lambda b: (b, 0, 0)),
            pl.BlockSpec((LANE, HK), lambda b: (0, 0)),
            pl.BlockSpec((LANE, HK), lambda b: (0, 0)),
            pl.BlockSpec((1, HK), lambda b: (0, 0)),
            pl.BlockSpec((1, GLA_DV), lambda b: (0, 0)),
        ],
        out_specs=pl.BlockSpec((NE, SEQ, HV), lambda b: (b, 0, 0)),
        out_shape=jax.ShapeDtypeStruct((B, SEQ, HV), bf16),
        scratch_shapes=[
            pltpu.VMEM((NE, SEQ, HK), f32),
            pltpu.VMEM((NE, GLA_HEADS, GLA_DK, GLA_DV), f32),
            pltpu.VMEM((NE, HK, HV), bf16),
        ],
        compiler_params=_cparams(("parallel",)),
        name="gla",
    )(proj, proj, proj, proj, glow, wup_hi, wup_lo, bg, gain)


LRU_SEG = 8
LRU_SEGLEN = SEQ // LRU_SEG


def _lru_body(xb_ref, yb_ref, cw_ref, cb_ref, w_ref, b_ref, lam_ref, o_ref, a_scr, u_scr, hs_scr, as_scr):
    x = xb_ref[0].astype(f32)
    row = lax.broadcasted_iota(i32, (SEQ, LANE), 0)
    xc = x * cw_ref[LRU_CONV - 1:LRU_CONV, :] + cb_ref[...]
    for j in range(1, LRU_CONV):
        xs = jnp.where(row >= j, pltpu.roll(x, j, 0), 0.0)
        xc = xc + xs * cw_ref[LRU_CONV - 1 - j:LRU_CONV - j, :]
    z = jnp.dot(xc.astype(bf16), w_ref[0], preferred_element_type=f32) + b_ref[0]
    r = jax.nn.sigmoid(z[:, :LANE])
    gi = jax.nn.sigmoid(z[:, LANE:])
    nl = -lam_ref[...]
    sp = jnp.maximum(nl, 0.0) + jnp.log1p(jnp.exp(-jnp.abs(nl)))
    log_a = (-LRU_C) * r * sp
    a = jnp.exp(log_a)
    u = jnp.sqrt(jnp.tanh(-log_a) * (a * a + 1.0)) * gi * xc
    a_scr[...] = a
    u_scr[...] = u

    def step(t, carry):
        h, acc = carry
        idx = pl.ds(t, LRU_SEG, stride=LRU_SEGLEN)
        at = a_scr[idx, :]
        h = at * h + u_scr[idx, :]
        acc = at * acc
        hs_scr[idx, :] = h
        as_scr[idx, :] = acc
        return h, acc

    hfin, afin = lax.fori_loop(0, LRU_SEGLEN, step, (jnp.zeros((LRU_SEG, LANE), f32), jnp.ones((LRU_SEG, LANE), f32)),
                               unroll=8)
    cin = jnp.zeros((1, LANE), f32)
    for s in range(LRU_SEG):
        rows = pl.ds(s * LRU_SEGLEN, LRU_SEGLEN)
        h = hs_scr[rows, :] + as_scr[rows, :] * cin
        o_ref[0, rows, :] = (h * jax.nn.gelu(yb_ref[0, rows, :].astype(f32))).astype(bf16)
        cin = hfin[s:s + 1, :] + afin[s:s + 1, :] * cin


def _lru(proj, cw, cb, w, b, lam):
    B = proj.shape[0]
    nt = LRU_W // LANE
    return pl.pallas_call(
        _lru_body,
        grid=(B, nt),
        in_specs=[
            pl.BlockSpec((1, SEQ, LANE), lambda b, t: (b, 0, XB_OFF // LANE + t)),
            pl.BlockSpec((1, SEQ, LANE), lambda b, t: (b, 0, YB_OFF // LANE + t)),
            pl.BlockSpec((LRU_CONV, LANE), lambda b, t: (0, t)),
            pl.BlockSpec((1, LANE), lambda b, t: (0, t)),
            pl.BlockSpec((1, LANE, 2 * LANE), lambda b, t: (t, 0, 0)),
            pl.BlockSpec((1, 1, 2 * LANE), lambda b, t: (t, 0, 0)),
            pl.BlockSpec((1, LANE), lambda b, t: (0, t)),
        ],
        out_specs=pl.BlockSpec((1, SEQ, LANE), lambda b, t: (b, 0, t)),
        out_shape=jax.ShapeDtypeStruct((B, SEQ, LRU_W), bf16),
        scratch_shapes=[pltpu.VMEM((SEQ, LANE), f32)] * 4,
        compiler_params=_cparams(("parallel", "parallel")),
        name="lru",
    )(proj, proj, cw, cb, w, b, lam)


def _rope_body(pos_ref, freq_ref, cos_ref, ss_ref):
    ang = pos_ref[0] * freq_ref[...]
    lane = lax.broadcasted_iota(i32, (SEQ, LANE), 1)
    sn = jnp.sin(ang)
    cos_ref[0] = jnp.cos(ang)
    ss_ref[0] = jnp.where(lane < ROPE_DIMS // 2, -sn, sn)


def _rope_tables(posf, freq):
    B = posf.shape[0]
    spec = pl.BlockSpec((1, SEQ, LANE), lambda b: (b, 0, 0))
    return pl.pallas_call(
        _rope_body,
        grid=(B,),
        in_specs=[pl.BlockSpec((1, SEQ, 1), lambda b: (b, 0, 0)), pl.BlockSpec((1, LANE), lambda b: (0, 0))],
        out_specs=[spec, spec],
        out_shape=[jax.ShapeDtypeStruct((B, SEQ, LANE), f32)] * 2,
        compiler_params=_cparams(("parallel",)),
        name="rope_tables",
    )(posf, freq)


def _rope_partner():
    half = ROPE_DIMS // 2
    perm = np.arange(LANE)
    perm[:half] += half
    perm[half:ROPE_DIMS] -= half
    return perm


ATTN_PC = 256
ATTN_G = 4


def _attn_body(q0, q1, q2, k0, k1, k2, v0, v1, v2, cos_ref, ss_ref, g_ref, p2_ref, o_ref,
               qd, kd, vd, tq, tk, tv, od, ld, on, ln, sc_a, sc_b):
    q_refs, k_refs, v_refs = (q0, q1, q2), (k0, k1, k2), (v0, v1, v2)
    BLK = ATTN_BLK
    qg = g_ref[0:1, :] * (DIL_DH ** -0.5)
    qgr = g_ref[1:2, :] * (DIL_DH ** -0.5)
    kg = g_ref[2:3, :]
    kgr = g_ref[3:4, :]
    kd[pl.ds(0, BLK), :] = jnp.zeros((BLK, LANE), bf16)
    vd[pl.ds(0, BLK), pl.ds(0, LANE)] = jnp.zeros((BLK, LANE), bf16)
    vd[:, pl.ds(LANE, LANE)] = jnp.ones((SEQ + BLK, LANE), bf16)

    qi = lax.broadcasted_iota(i32, (BLK, 2 * BLK), 0)
    kj = lax.broadcasted_iota(i32, (BLK, 2 * BLK), 1)
    is_cur = kj >= BLK
    band = jnp.where(is_cur, kj - BLK, qi) <= jnp.where(is_cur, qi, kj)

    def norm_rope(raw, partner, gain, gain_p, cs, ss):
        rf = raw.astype(f32)
        r = lax.rsqrt(jnp.mean(rf * rf, axis=-1, keepdims=True) + RMS_EPS)
        return (rf * (gain * cs) + partner * (gain_p * ss)) * r

    for p, (_, dil) in enumerate(DIL_PATTERNS):
        L = SEQ // dil
        nbl = L // BLK

        def prep(c, carry, p=p, dil=dil):
            rows = pl.ds(pl.multiple_of(c * ATTN_PC, ATTN_PC), ATTN_PC)
            cs, ss = cos_ref[0, rows, :], ss_ref[0, rows, :]
            qr = q_refs[p][0, rows, :]
            kr = k_refs[p][0, rows, :]
            partner = jnp.dot(jnp.concatenate([qr, kr], axis=1), p2_ref[...], preferred_element_type=f32)
            q = norm_rope(qr, partner[:, :LANE], qg, qgr, cs, ss)
            k = norm_rope(kr, partner[:, LANE:], kg, kgr, cs, ss)
            if dil == 1:
                orow = pl.ds(pl.multiple_of(c * ATTN_PC, ATTN_PC) + BLK, ATTN_PC)
                qd[rows, :] = q.astype(bf16)
                kd[orow, :] = k.astype(bf16)
                vd[orow, pl.ds(0, LANE)] = v_refs[p][0, rows, :]
            else:
                tq[rows, :] = q
                tk[rows, :] = k
                tv[rows, :] = v_refs[p][0, rows, :].astype(f32)
            return carry

        lax.fori_loop(0, SEQ // ATTN_PC, prep, 0, unroll=2)
        if dil > 1:
            for r in range(dil):
                src = pl.ds(r, L, stride=dil)
                qd[pl.ds(r * L, L), :] = tq[src, :].astype(bf16)
                kd[pl.ds(BLK + r * L, L), :] = tk[src, :].astype(bf16)
                vd[pl.ds(BLK + r * L, L), pl.ds(0, LANE)] = tv[src, :].astype(bf16)

        o_dst, l_dst = (on.at[p], ln.at[p]) if dil == 1 else (od, ld)

        def scores(g, dst):
            for j in range(ATTN_G):
                r0 = pl.multiple_of((g * ATTN_G + j) * BLK, BLK)
                dst[pl.ds(j * BLK, BLK), :] = lax.dot_general(qd[pl.ds(r0, BLK), :], kd[pl.ds(r0, 2 * BLK), :], NT,
                                                              preferred_element_type=f32)

        def finish(g, src, nbl=nbl, o_dst=o_dst, l_dst=l_dst):
            for j in range(ATTN_G):
                b = g * ATTN_G + j
                r0 = pl.multiple_of(b * BLK, BLK)
                first = (b % nbl) == 0
                valid = band & (is_cur | jnp.logical_not(first))
                s = jnp.where(valid, src[pl.ds(j * BLK, BLK), :], -jnp.inf)
                m = jnp.max(s, axis=-1, keepdims=True)
                e = jnp.exp(s - m).astype(bf16)
                ov = jnp.dot(e, vd[pl.ds(r0, 2 * BLK), :], preferred_element_type=f32)
                den = ov[:, LANE:]
                o_dst[pl.ds(r0, BLK), :] = ov[:, :LANE] / den
                l_dst[pl.ds(r0, BLK), :] = m + jnp.log(den)

        n_groups = SEQ // BLK // ATTN_G
        scores(0, sc_a)

        def pair(i, carry, scores=scores, finish=finish):
            scores(2 * i + 1, sc_b)
            finish(2 * i, sc_a)
            scores(jnp.minimum(2 * i + 2, n_groups - 1), sc_a)
            finish(2 * i + 1, sc_b)
            return carry

        lax.fori_loop(0, n_groups // 2, pair, 0)
        if dil > 1:
            for r in range(dil):
                dst = pl.ds(r, L, stride=dil)
                on[p, dst, :] = od[pl.ds(r * L, L), :]
                ln[p, dst, :] = ld[pl.ds(r * L, L), :]

    def combine(c, carry):
        rows = pl.ds(pl.multiple_of(c * ATTN_PC, ATTN_PC), ATTN_PC)
        l0, l1, l2 = ln[0, rows, :], ln[1, rows, :], ln[2, rows, :]
        mx = jnp.maximum(jnp.maximum(l0, l1), l2)
        w0, w1, w2 = jnp.exp(l0 - mx), jnp.exp(l1 - mx), jnp.exp(l2 - mx)
        o = (w0 * on[0, rows, :] + w1 * on[1, rows, :] + w2 * on[2, rows, :]) / (w0 + w1 + w2)
        o_ref[0, rows, :] = o.astype(bf16)
        return carry

    lax.fori_loop(0, SEQ // ATTN_PC, combine, 0)


def _attn(proj, cos, ss, qg, kg, b0):
    B = proj.shape[0]
    base = DIL_OFF // LANE
    perm = _rope_partner()
    gains = jnp.stack([qg, qg[perm], kg, kg[perm]]).astype(f32)
    pm = np.zeros((LANE, LANE), np.float32)
    pm[perm[:ROPE_DIMS], np.arange(ROPE_DIMS)] = 1.0
    p2 = jnp.asarray(np.kron(np.eye(2, dtype=np.float32), pm), dtype=bf16)

    def pspec(qkv, p):
        return pl.BlockSpec((1, SEQ, LANE), lambda b, h, o=base + (qkv * N_PAT + p) * DIL_HEADS: (b, 0, o + h))

    tspec = pl.BlockSpec((1, SEQ, LANE), lambda b, h: (b + b0, 0, 0))
    in_specs = [pspec(qkv, p) for qkv in range(3) for p in range(N_PAT)] + [
        tspec, tspec, pl.BlockSpec((4, LANE), lambda b, h: (0, 0)), pl.BlockSpec((2 * LANE, 2 * LANE), lambda b, h: (0, 0))]
    return pl.pallas_call(
        _attn_body,
        grid=(B, DIL_HEADS),
        in_specs=in_specs,
        out_specs=pl.BlockSpec((1, SEQ, LANE), lambda b, h: (b, 0, h)),
        out_shape=jax.ShapeDtypeStruct((B, SEQ, DIL_HEADS * DIL_DH), bf16),
        scratch_shapes=[
            pltpu.VMEM((SEQ, LANE), bf16),
            pltpu.VMEM((SEQ + ATTN_BLK, LANE), bf16),
            pltpu.VMEM((SEQ + ATTN_BLK, 2 * LANE), bf16),
            pltpu.VMEM((SEQ, LANE), f32),
            pltpu.VMEM((SEQ, LANE), f32),
            pltpu.VMEM((SEQ, LANE), f32),
            pltpu.VMEM((SEQ, LANE), f32),
            pltpu.VMEM((SEQ, LANE), f32),
            pltpu.VMEM((N_PAT, SEQ, LANE), f32),
            pltpu.VMEM((N_PAT, SEQ, LANE), f32),
            pltpu.VMEM((ATTN_G * ATTN_BLK, 2 * ATTN_BLK), f32),
            pltpu.VMEM((ATTN_G * ATTN_BLK, 2 * ATTN_BLK), f32),
        ],
        compiler_params=_cparams(("parallel", "arbitrary")),
        name="dilated_attn",
    )(*([proj] * 9), cos, ss, gains, p2)


def _merge_body(a_ref, b_ref, c_ref, g0_ref, g1_ref, g2_ref, x_ref, wb_ref, wo_ref, n2_ref, rw_ref, rb_ref, tri_ref,
                x1_ref, h2_ref, ei_ref, ew_ref, pos_ref, cnt_ref, cb_ref, cnt_scr):
    for c in range(MERGE_TM // MERGE_RC):
        rows = pl.ds(c * MERGE_RC, MERGE_RC)
        merged = None
        for n, (br, g) in enumerate(((a_ref, g0_ref), (b_ref, g1_ref), (c_ref, g2_ref))):
            gate = 0.5 * jnp.tanh(0.5 * g[rows, :].astype(f32)) + 0.5
            t = gate * jnp.dot(br[rows, :], wb_ref[n], preferred_element_type=f32)
            merged = t if merged is None else merged + t
        x1 = x_ref[rows, :] + jnp.dot(merged.astype(bf16), wo_ref[...], preferred_element_type=f32)
        x1_ref[rows, :] = x1
        h2_ref[rows, :] = ((x1 * lax.rsqrt(jnp.mean(x1 * x1, axis=-1, keepdims=True) + RMS_EPS))
                           * n2_ref[...]).astype(bf16)
    h2 = h2_ref[...]

    lg = lax.dot_general(rw_ref[...], h2, NT, preferred_element_type=f32) + rb_ref[...]
    ex = jnp.exp(lg - jnp.max(lg, axis=0, keepdims=True))
    sc = ex / jnp.sum(ex, axis=0, keepdims=True)
    srow = [sc[e:e + 1, :] for e in range(N_EXP)]
    best = bidx = None
    for gidx in range(N_GRP):
        s0, s1, s2, s3 = srow[4 * gidx:4 * gidx + 4]
        hi01, lo01, hi23, lo23 = jnp.maximum(s0, s1), jnp.minimum(s0, s1), jnp.maximum(s2, s3), jnp.minimum(s2, s3)
        gs = jnp.maximum(hi01, hi23) + jnp.maximum(jnp.minimum(hi01, hi23), jnp.maximum(lo01, lo23))
        if best is None:
            best, bidx = gs, jnp.zeros_like(gs, dtype=i32)
        else:
            better = gs > best
            best = jnp.where(better, gs, best)
            bidx = jnp.where(better, gidx, bidx)
    cand = []
    for i in range(EXP_PER_GRP):
        v = srow[i]
        for gidx in range(1, N_GRP):
            v = jnp.where(bidx == gidx, srow[4 * gidx + i], v)
        cand.append(v)

    def argmax4(vals):
        bw, bi = vals[0], jnp.zeros_like(bidx)
        for i in range(1, EXP_PER_GRP):
            better = vals[i] > bw
            bw = jnp.where(better, vals[i], bw)
            bi = jnp.where(better, i, bi)
        return bw, bi

    w1, i1 = argmax4(cand)
    w2, i2 = argmax4([jnp.where(i1 == i, -jnp.inf, cand[i]) for i in range(EXP_PER_GRP)])
    tot = w1 + w2
    e1 = bidx * EXP_PER_GRP + i1
    e2 = bidx * EXP_PER_GRP + i2
    ei_ref[0:1, :] = e1
    ei_ref[1:2, :] = e2
    ew_ref[0:1, :] = w1 / tot
    ew_ref[1:2, :] = w2 / tot

    @pl.when(pl.program_id(0) == 0)
    def _():
        cnt_scr[...] = jnp.zeros_like(cnt_scr)

    erow = lax.broadcasted_iota(i32, (N_EXP, MERGE_TM), 0)
    oh1 = erow == e1
    oh2 = erow == e2
    both = jnp.where(oh1 | oh2, 1.0, 0.0)
    incl = jnp.dot(both.astype(bf16), tri_ref[...], preferred_element_type=f32)
    cb_ref[0] = cnt_scr[...]
    base = cnt_scr[...] + (incl - both)
    pos_ref[0:1, :] = jnp.sum(jnp.where(oh1, base, 0.0), axis=0, keepdims=True).astype(i32)
    pos_ref[1:2, :] = jnp.sum(jnp.where(oh2, base, 0.0), axis=0, keepdims=True).astype(i32)
    cnt = cnt_scr[...] + incl[:, MERGE_TM - 1:MERGE_TM]
    cnt_scr[...] = cnt
    cnt_ref[...] = cnt


def _merge(bra, brb, brc, proj2d, x2d, x_row0, wb, wo, n2, rw, rb):
    T = bra.shape[0]
    TM = MERGE_TM
    x_blk0 = x_row0 // TM
    rspec = lambda w: pl.BlockSpec((TM, w), lambda i: (i, 0))
    full = lambda shape: pl.BlockSpec(shape, lambda i: (0,) * len(shape))
    tri = (jnp.arange(TM)[:, None] <= jnp.arange(TM)[None, :]).astype(bf16)
    return pl.pallas_call(
        _merge_body,
        grid=(T // TM,),
        in_specs=[
            rspec(BR_W), rspec(BR_W), rspec(BR_W),
            pl.BlockSpec((TM, D_MODEL), lambda i: (i, GATE_OFF // D_MODEL + 0)),
            pl.BlockSpec((TM, D_MODEL), lambda i: (i, GATE_OFF // D_MODEL + 1)),
            pl.BlockSpec((TM, D_MODEL), lambda i: (i, GATE_OFF // D_MODEL + 2)),
            pl.BlockSpec((TM, D_MODEL), lambda i: (i + x_blk0, 0)),
            full((N_BR, BR_W, D_MODEL)), full((D_MODEL, D_MODEL)), full((1, D_MODEL)),
            full((N_EXP, D_MODEL)), full((N_EXP, 1)), full((TM, TM)),
        ],
        out_specs=[
            rspec(D_MODEL), rspec(D_MODEL),
            pl.BlockSpec((TOP_K, TM), lambda i: (0, i)),
            pl.BlockSpec((TOP_K, TM), lambda i: (0, i)),
            pl.BlockSpec((TOP_K, TM), lambda i: (0, i)),
            full((N_EXP, 1)),
            pl.BlockSpec((1, N_EXP, 1), lambda i: (i, 0, 0)),
        ],
        out_shape=[
            jax.ShapeDtypeStruct((T, D_MODEL), f32),
            jax.ShapeDtypeStruct((T, D_MODEL), bf16),
            jax.ShapeDtypeStruct((TOP_K, T), i32),
            jax.ShapeDtypeStruct((TOP_K, T), f32),
            jax.ShapeDtypeStruct((TOP_K, T), i32),
            jax.ShapeDtypeStruct((N_EXP, 1), f32),
            jax.ShapeDtypeStruct((T // TM, N_EXP, 1), f32),
        ],
        scratch_shapes=[pltpu.VMEM((N_EXP, 1), f32)],
        compiler_params=_cparams(("arbitrary",)),
        name="merge_router",
    )(bra, brb, brc, proj2d, proj2d, proj2d, x2d, wb, wo, n2, rw, rb, tri)


def _expert_body(be_ref, nu_ref, xs_ref, wgu_ref, wd_ref, ys_ref):
    i = pl.program_id(0)

    @pl.when(i < nu_ref[0])
    def _():
        gu = jnp.dot(xs_ref[...], wgu_ref[0], preferred_element_type=f32)
        g = gu[:, :EXP_FF]
        act = (g * jax.nn.sigmoid(g)) * gu[:, EXP_FF:]
        ys_ref[...] = jnp.dot(act.astype(bf16), wd_ref[0], preferred_element_type=f32).astype(bf16)

    @pl.when(i >= nu_ref[0])
    def _():
        ys_ref[...] = jnp.zeros_like(ys_ref)


def _experts(block_e, n_used, xs, wgu, wd):
    n_rows = xs.shape[0]
    grid_spec = pltpu.PrefetchScalarGridSpec(
        num_scalar_prefetch=2,
        grid=(n_rows // MOE_BM,),
        in_specs=[
            pl.BlockSpec((MOE_BM, D_MODEL), lambda i, be, nu: (jnp.minimum(i, nu[0] - 1), 0)),
            pl.BlockSpec((1, D_MODEL, 2 * EXP_FF), lambda i, be, nu: (be[i], 0, 0)),
            pl.BlockSpec((1, EXP_FF, D_MODEL), lambda i, be, nu: (be[i], 0, 0)),
        ],
        out_specs=pl.BlockSpec((MOE_BM, D_MODEL), lambda i, be, nu: (i, 0)),
    )
    return pl.pallas_call(
        _expert_body,
        grid_spec=grid_spec,
        out_shape=jax.ShapeDtypeStruct((n_rows, D_MODEL), bf16),
        compiler_params=_cparams(("arbitrary",)),
        name="experts",
    )(block_e, n_used, xs, wgu, wd)


COMB_TM = 1024


def _combine_body(x1_ref, y0_ref, y1_ref, w_ref, *rest):
    o_ref = rest[-1]
    w = w_ref[...]
    o_ref[...] = x1_ref[...] + y0_ref[...].astype(f32) * w[:, 0:1] + y1_ref[...].astype(f32) * w[:, 1:2]


def _combine(x1, y0, y1, w, out_rows=None, row0=0, out_prev=None):
    T = x1.shape[0]
    out_rows = T if out_rows is None else out_rows
    blk0 = row0 // COMB_TM
    rspec = pl.BlockSpec((COMB_TM, D_MODEL), lambda i: (i, 0))
    in_specs = [rspec, rspec, rspec, pl.BlockSpec((COMB_TM, TOP_K), lambda i: (i, 0))]
    args = [x1, y0, y1, w]
    aliases = {}
    if out_prev is not None:
        in_specs.append(pl.BlockSpec(memory_space=pl.ANY))
        args.append(out_prev)
        aliases = {4: 0}
    return pl.pallas_call(
        _combine_body,
        grid=(T // COMB_TM,),
        in_specs=in_specs,
        out_specs=pl.BlockSpec((COMB_TM, D_MODEL), lambda i: (i + blk0, 0)),
        out_shape=jax.ShapeDtypeStruct((out_rows, D_MODEL), f32),
        input_output_aliases=aliases,
        compiler_params=_cparams(("parallel",)),
        name="moe_combine",
    )(*args)


DISP_ALIGN = 16
DISP_SLOTS = MERGE_TM * TOP_K + N_EXP * DISP_ALIGN
DISP_SIZES = tuple(DISP_ALIGN << s for s in range(5, -1, -1))


def _pieces(length, sizes, fn):
    done = jnp.int32(0)
    for sz in sizes:
        take = (length & sz) != 0

        @pl.when(take)
        def _(done=done, sz=sz):
            fn(done, sz)

        done = done + jnp.where(take, sz, 0)


def _dispatch_body(off_ref, n_ref, ls_ref, toff_ref, tlen_ref, h2_ref, ei_ref, pos_ref, adj_ref, xs_hbm,
                   srt, zbuf, sem, zsem):
    i = pl.program_id(0)
    last = pl.num_programs(0) - 1
    cur = i % 2
    TM = MERGE_TM

    def slab_copies(step, buf, start):
        for e in range(N_EXP):
            ls = ls_ref[step * N_EXP + e]
            off = off_ref[step * N_EXP + e]

            def piece(done, sz, ls=ls, off=off):
                cp = pltpu.make_async_copy(srt.at[buf, pl.ds(pl.multiple_of(ls + done, DISP_ALIGN), sz)],
                                           xs_hbm.at[pl.ds(pl.multiple_of(off + done, DISP_ALIGN), sz)], sem.at[buf])
                cp.start() if start else cp.wait()

            _pieces(n_ref[step * N_EXP + e], DISP_SIZES, piece)

    def tail_copies(start):
        for e in range(N_EXP):
            off = toff_ref[e]

            def piece(done, sz, off=off):
                cp = pltpu.make_async_copy(zbuf.at[pl.ds(0, sz)],
                                           xs_hbm.at[pl.ds(pl.multiple_of(off + done, DISP_ALIGN), sz)], zsem)
                cp.start() if start else cp.wait()

            _pieces(tlen_ref[e], DISP_SIZES[1:], piece)

        zrows = zbuf.shape[0]

        def block(c, carry):
            cp = pltpu.make_async_copy(zbuf, xs_hbm.at[pl.ds(pl.multiple_of(toff_ref[N_EXP] + c * zrows, zrows), zrows)],
                                       zsem)
            cp.start() if start else cp.wait()
            return carry

        lax.fori_loop(0, tlen_ref[N_EXP] // zrows, block, 0)

    @pl.when(i == 0)
    def _():
        zbuf[...] = jnp.zeros_like(zbuf)
        tail_copies(True)

    @pl.when(i >= 2)
    def _():
        slab_copies(i - 2, cur, False)

    erow = lax.broadcasted_iota(i32, (N_EXP, TM), 0)
    adj = adj_ref[0]

    def slot(k):
        sel = jnp.sum(jnp.where(erow == ei_ref[k:k + 1, :], adj, 0.0), axis=0, keepdims=True)
        return pos_ref[k:k + 1, :] + sel.astype(i32)

    srow = lax.broadcasted_iota(i32, (DISP_SLOTS, TM), 0)
    perm = jnp.where((srow == slot(0)) | (srow == slot(1)), 1.0, 0.0).astype(bf16)
    srt[cur] = jnp.dot(perm, h2_ref[...], preferred_element_type=f32).astype(bf16)
    slab_copies(i, cur, True)

    @pl.when(i == last)
    def _():
        slab_copies(i, cur, False)

        @pl.when(i >= 1)
        def _():
            slab_copies(i - 1, 1 - cur, False)

        tail_copies(False)


def _dispatch(off, n16, lstart, tail_off, tail_len, h2, eidx, pos, adj, n_rows):
    T = h2.shape[0]
    TM = MERGE_TM
    grid_spec = pltpu.PrefetchScalarGridSpec(
        num_scalar_prefetch=5,
        grid=(T // TM,),
        in_specs=[
            pl.BlockSpec((TM, D_MODEL), lambda i, *_: (i, 0)),
            pl.BlockSpec((TOP_K, TM), lambda i, *_: (0, i)),
            pl.BlockSpec((TOP_K, TM), lambda i, *_: (0, i)),
            pl.BlockSpec((1, N_EXP, 1), lambda i, *_: (i, 0, 0)),
        ],
        out_specs=pl.BlockSpec(memory_space=pl.ANY),
        scratch_shapes=[
            pltpu.VMEM((2, DISP_SLOTS, D_MODEL), bf16),
            pltpu.VMEM((DISP_SIZES[1], D_MODEL), bf16),
            pltpu.SemaphoreType.DMA((2,)),
            pltpu.SemaphoreType.DMA(()),
        ],
    )
    return pl.pallas_call(
        _dispatch_body,
        grid_spec=grid_spec,
        out_shape=jax.ShapeDtypeStruct((n_rows, D_MODEL), bf16),
        compiler_params=_cparams(("arbitrary",)),
        name="moe_dispatch",
    )(off, n16, lstart, tail_off, tail_len, h2, eidx, pos, adj)


def _moe(x1, h2, eidx, ew, pos, cnt, cb, wgu, wd, **out_kw):
    T = x1.shape[0]
    TM = MERGE_TM
    n_tiles = T // TM
    ceil_to = lambda v, m: (v + m - 1) // m * m
    n_rows = ceil_to(T * TOP_K + n_tiles * N_EXP * (DISP_ALIGN - 1) + N_EXP * (MOE_BM - 1), MOE_BM)
    n_blocks = n_rows // MOE_BM
    cbi = cb[:, :, 0].astype(i32)
    tile_cnt = jnp.concatenate([cbi[1:], cnt[:, 0].astype(i32)[None, :]], axis=0) - cbi
    n16 = ceil_to(tile_cnt, DISP_ALIGN)
    lstart = jnp.cumsum(n16, axis=1) - n16
    before = jnp.cumsum(n16, axis=0) - n16
    used = jnp.sum(n16, axis=0)
    padded = ceil_to(used, MOE_BM)
    pad_ends = jnp.cumsum(padded)
    off = (pad_ends - padded)[None, :] + before
    adj = (lstart - cbi).astype(f32)[:, :, None]
    row_adj = jnp.repeat(off - cbi, TM, axis=0).T
    dest = pos
    for e in range(N_EXP):
        dest = dest + jnp.where(eidx == e, row_adj[e][None, :], 0)
    blk_start = jnp.arange(n_blocks, dtype=i32) * MOE_BM
    block_e = jnp.minimum(jnp.sum((blk_start[:, None] >= pad_ends[None, :]).astype(i32), axis=1), N_EXP - 1)
    n_used = (pad_ends[-1:] // MOE_BM).astype(i32)
    tail_off = jnp.concatenate([pad_ends - padded + used, pad_ends[-1:]])
    tail_len = jnp.concatenate([padded - used, n_rows - pad_ends[-1:]])
    xs = _dispatch(off.reshape(-1), n16.reshape(-1), lstart.reshape(-1), tail_off, tail_len, h2, eidx, pos, adj, n_rows)
    ys = _experts(block_e, n_used, xs, wgu, wd)
    return _combine(x1, ys[dest[0]], ys[dest[1]], ew.T, **out_kw)


def _pack_w_in(w):
    cuts = np.cumsum([256, 256, 512, 512, 16, 512, 512, 4608, 3072])[:-1].tolist()
    q, k, v, og, glow, xb, yb, qkv, gates = jnp.split(w, cuts, axis=-1)
    packed = jnp.concatenate([gates, qkv, xb, yb, q, k, v, og], axis=-1).astype(bf16)
    wg = jnp.pad(glow, ((0, 0), (0, LANE - GLA_RANK))).astype(bf16)
    return packed, wg


def _block_diag_lru(w_a, w_x):
    bs = LRU_W // LRU_BLOCKS
    per = LANE // bs
    tiles = []
    for t in range(LRU_W // LANE):
        halves = []
        for w in (w_a, w_x):
            m = jnp.zeros((LANE, LANE), f32)
            for j in range(per):
                m = m.at[j * bs:(j + 1) * bs, j * bs:(j + 1) * bs].set(w[t * per + j])
            halves.append(m)
        tiles.append(jnp.concatenate(halves, axis=1))
    return jnp.stack(tiles).astype(bf16)


def kernel(x, positions, norm1_gain, w_in, gla_gate_up, gla_gate_bias, gla_out_gain, lru_conv_w, lru_conv_b, lru_w_a, lru_b_a, lru_w_x, lru_b_x, lru_lambda, q_norm_gain, k_norm_gain, w_branch, w_out, norm2_gain, router_w, router_b, w_gate, w_up, w_down):
    B, S, D = x.shape
    assert S == SEQ and D == D_MODEL
    T = B * S
    depth = w_in.shape[0]

    half = ROPE_DIMS // 2
    inv_freq = ROPE_THETA ** (-jnp.arange(half, dtype=f32) / half)
    freq = jnp.zeros((1, LANE), f32).at[0, :ROPE_DIMS].set(jnp.concatenate([inv_freq, inv_freq]))
    cos, ss = _rope_tables(positions.astype(f32)[:, :, None], freq)

    rw = router_w.T.astype(bf16)
    rb = router_b.astype(f32)[:, None]
    n_str = N_STREAMS if B % N_STREAMS == 0 else 1
    Bs = B // n_str
    Ts = Bs * S
    xs_ = [x] * n_str
    out = None
    for l in range(depth):
        w_packed, wg = _pack_w_in(w_in[l])
        wup = jnp.pad(gla_gate_up[l], ((0, LANE - GLA_RANK), (0, 0)))
        nt = LRU_W // LANE
        lru_w = _block_diag_lru(lru_w_a[l], lru_w_x[l])
        lru_b = jnp.concatenate([lru_b_a[l].reshape(nt, 1, LANE), lru_b_x[l].reshape(nt, 1, LANE)], axis=-1)
        wb, wo = w_branch[l].astype(bf16), w_out[l].astype(bf16)
        wgu = jnp.concatenate([w_gate[l], w_up[l]], axis=-1).astype(bf16)
        wd = w_down[l].astype(bf16)
        last = l == depth - 1
        for s in range(n_str):
            b0 = s * Bs
            xin = xs_[s]
            shared_in = xin.shape[0] == B and n_str > 1
            proj, glow = _inproj(xin, norm1_gain[l][None, :], w_packed, wg, b0 if shared_in else 0, Bs)
            br_a = _gla(proj, glow, wup, gla_gate_bias[l][None, :], gla_out_gain[l][None, :])
            br_b = _lru(proj, lru_conv_w[l], lru_conv_b[l][None, :], lru_w, lru_b, lru_lambda[l][None, :])
            br_c = _attn(proj, cos, ss, q_norm_gain[l], k_norm_gain[l], b0)
            routed = _merge(br_a.reshape(Ts, BR_W), br_b.reshape(Ts, BR_W), br_c.reshape(Ts, BR_W),
                            proj.reshape(Ts, PROJ_COLS), xin.reshape(-1, D),
                            b0 * S if shared_in else 0, wb, wo, norm2_gain[l][None, :], rw, rb)
            if last:
                out = _moe(*routed, wgu, wd, out_rows=T, row0=b0 * S, out_prev=out)
            else:
                xs_[s] = _moe(*routed, wgu, wd).reshape(Bs, S, D)
    return out.reshape(B, S, D)
```

```python
import functools

import jax
import jax.numpy as jnp
import numpy as np
from jax import lax
from jax.experimental import pallas as pl
from jax.experimental.pallas import tpu as pltpu

f32 = jnp.float32
bf16 = jnp.bfloat16
i32 = jnp.int32

D_MODEL = 1024
SEQ = 2048
RMS_EPS = 1e-6
GLA_HEADS, GLA_DK, GLA_DV, GLA_RANK, GLA_NORM, GLA_CHUNK = 4, 64, 128, 16, 16.0, 64
LRU_W, LRU_BLOCKS, LRU_CONV, LRU_C = 512, 8, 4, 8.0
DIL_PATTERNS = ((128, 1), (512, 4), (2048, 16))
N_PAT, DIL_HEADS, DIL_DH, ATTN_BLK = 3, 4, 128, 128
ROPE_THETA, ROPE_DIMS = 500000.0, 32
N_BR, BR_W = 3, 512
N_EXP, N_GRP, EXP_PER_GRP, TOP_K, EXP_FF = 16, 4, 4, 2, 512

LANE = 128
GATE_OFF, DIL_OFF, XB_OFF, YB_OFF, Q_OFF, K_OFF, V_OFF, OG_OFF = 0, 3072, 7680, 8192, 8704, 8960, 9216, 9728
PROJ_COLS = 10240
PROJ_TN = 1024
PROJ_RC = 512
MERGE_TM = 512
MERGE_RC = 512
MOE_BM = 512
N_STREAMS = 1
GLA_NE = 1
VMEM_LIMIT = 56 * 1024 * 1024

HIGHEST = lax.Precision.HIGHEST
NT = (((1,), (1,)), ((), ()))


def _cparams(sem):
    return pltpu.CompilerParams(dimension_semantics=sem, vmem_limit_bytes=VMEM_LIMIT)


def _inproj_body(x_ref, g_ref, w_ref, wg_ref, proj_ref, glow_ref, h_scr):
    j = pl.program_id(1)
    nrc = SEQ // PROJ_RC

    @pl.when(j == 0)
    def _():
        for c in range(nrc):
            rows = pl.ds(c * PROJ_RC, PROJ_RC)
            x = x_ref[0, rows, :]
            ms = jnp.mean(x * x, axis=-1, keepdims=True)
            h = ((x * lax.rsqrt(ms + RMS_EPS)) * g_ref[...]).astype(bf16)
            h_scr[rows, :] = h
            glow_ref[0, rows, :] = jnp.dot(h, wg_ref[...], preferred_element_type=f32)

    for c in range(nrc):
        rows = pl.ds(c * PROJ_RC, PROJ_RC)
        res = jnp.dot(h_scr[rows, :], w_ref[...], preferred_element_type=f32).astype(bf16)
        for t in range(PROJ_TN // LANE):
            proj_ref[0, t, rows, :] = res[:, t * LANE:(t + 1) * LANE]


def _inproj(x, gain, w, wg, b0, B):
    nj = PROJ_COLS // PROJ_TN
    return pl.pallas_call(
        _inproj_body,
        grid=(B, nj),
        in_specs=[
            pl.BlockSpec((1, SEQ, D_MODEL), lambda b, j: (b + b0, 0, 0)),
            pl.BlockSpec((1, D_MODEL), lambda b, j: (0, 0)),
            pl.BlockSpec((D_MODEL, PROJ_TN), lambda b, j: (0, j)),
            pl.BlockSpec((D_MODEL, LANE), lambda b, j: (0, 0)),
        ],
        out_specs=[
            pl.BlockSpec((1, PROJ_TN // LANE, SEQ, LANE), lambda b, j: (b, j, 0, 0)),
            pl.BlockSpec((1, SEQ, LANE), lambda b, j: (b, 0, 0)),
        ],
        out_shape=[
            jax.ShapeDtypeStruct((B, PROJ_COLS // LANE, SEQ, LANE), bf16),
            jax.ShapeDtypeStruct((B, SEQ, LANE), f32),
        ],
        scratch_shapes=[pltpu.VMEM((SEQ, D_MODEL), bf16)],
        compiler_params=_cparams(("parallel", "arbitrary")),
        name="inproj",
    )(x, gain, w, wg)


def _gla_body(q_ref, k_ref, v_ref, og_ref, glow_ref, wh_ref, wl_ref, bg_ref, gain_ref, o_ref, la_scr, st_scr, stb_scr):
    C = GLA_CHUNK
    HK = GLA_HEADS * GLA_DK
    HV = GLA_HEADS * GLA_DV
    PR = 512
    NE = q_ref.shape[0]
    for e in range(NE):
        for c in range(SEQ // PR):
            rows = pl.ds(c * PR, PR)
            g = glow_ref[e, rows, :]
            gh = g.astype(bf16)
            gl = (g - gh.astype(f32)).astype(bf16)
            z = (jnp.dot(gh, wh_ref[...], preferred_element_type=f32)
                 + jnp.dot(gl, wh_ref[...], preferred_element_type=f32)
                 + jnp.dot(gh, wl_ref[...], preferred_element_type=f32)) + bg_ref[...]
            la_scr[e, rows, :] = (jnp.minimum(z, 0.0) - jnp.log1p(jnp.exp(-jnp.abs(z)))) * (1.0 / GLA_NORM)
    st_scr[...] = jnp.zeros_like(st_scr)
    stb_scr[...] = jnp.zeros_like(stb_scr)

    causal = (lax.broadcasted_iota(i32, (C, GLA_HEADS * C), 0)
              >= lax.broadcasted_iota(i32, (C, GLA_HEADS * C), 1) % C)
    tri = (lax.broadcasted_iota(i32, (C, C), 0) >= lax.broadcasted_iota(i32, (C, C), 1)).astype(bf16)
    lane_head = lax.broadcasted_iota(i32, (1, HK), 1) // GLA_DK
    gain = gain_ref[...]
    zero_v = jnp.zeros((C, GLA_DV), bf16)

    def chunk(n, carry):
        for e in range(NE):
            chunk_one(n, e)
        return carry

    def chunk_one(n, e):
        rows = pl.ds(pl.multiple_of(n * C, C), C)
        la = la_scr[e, rows, :]
        p1 = la.astype(bf16)
        r1 = la - p1.astype(f32)
        p2 = r1.astype(bf16)
        p3 = (r1 - p2.astype(f32)).astype(bf16)
        c3 = jnp.dot(tri, jnp.concatenate([p1, p2, p3], axis=1), preferred_element_type=f32)
        cum = (c3[:, :HK] + c3[:, HK:2 * HK]) + c3[:, 2 * HK:]
        cum_t = cum.T
        tot_t = cum_t[:, C - 1:C]
        qf = jnp.concatenate([q_ref[e, t, rows, :] for t in range(HK // LANE)], axis=1).astype(f32)
        kf = jnp.concatenate([k_ref[e, t, rows, :] for t in range(HK // LANE)], axis=1).astype(f32)
        qd = (qf * (GLA_DK ** -0.5) * jnp.exp(cum)).astype(bf16)
        ki = kf * jnp.exp(-cum)
        kd_t = (kf.T * jnp.exp(tot_t - cum_t)).astype(bf16)
        dec_t = jnp.exp(tot_t)
        heads = range(GLA_HEADS)
        vcols = [v_ref[e, h, rows, :] for h in heads]
        k_st = jnp.concatenate([jnp.where(lane_head == h, ki, 0.0).astype(bf16) for h in heads], axis=0)
        v_bd = jnp.concatenate([jnp.concatenate([vcols[h] if g == h else zero_v for g in heads], axis=1)
                                for h in heads], axis=0)
        s = lax.dot_general(qd, k_st, NT, preferred_element_type=f32)
        s = jnp.where(causal, s, 0.0).astype(bf16)
        o = jnp.dot(s, v_bd, preferred_element_type=f32) + jnp.dot(qd, stb_scr[e], preferred_element_type=f32)
        for h in heads:
            cols = slice(h * GLA_DV, (h + 1) * GLA_DV)
            hr = slice(h * GLA_DK, (h + 1) * GLA_DK)
            st = st_scr[e, h] * dec_t[hr, :] + jnp.dot(kd_t[hr, :], vcols[h], preferred_element_type=f32)
            st_scr[e, h] = st
            stb_scr[e, hr, cols] = st.astype(bf16)
            oh = o[:, cols]
            oh = oh * lax.rsqrt(jnp.mean(oh * oh, axis=-1, keepdims=True) + RMS_EPS) * gain
            g = og_ref[e, h, rows, :].astype(f32)
            o_ref[e, rows, cols] = (oh * (g * jax.nn.sigmoid(g))).astype(bf16)

    lax.fori_loop(0, SEQ // C, chunk, 0, unroll=4 // NE)


def _gla(proj, glow, wup, bg, gain):
    B = proj.shape[0]
    HK = GLA_HEADS * GLA_DK
    HV = GLA_HEADS * GLA_DV
    wup_hi = wup.astype(bf16)
    wup_lo = (wup - wup_hi.astype(f32)).astype(bf16)
    NE = GLA_NE if B % GLA_NE == 0 else 1
    return pl.pallas_call(
        _gla_body,
        grid=(B // NE,),
        in_specs=[
            pl.BlockSpec((NE, HK // LANE, SEQ, LANE), lambda b: (b, Q_OFF // HK, 0, 0)),
            pl.BlockSpec((NE, HK // LANE, SEQ, LANE), lambda b: (b, K_OFF // HK, 0, 0)),
            pl.BlockSpec((NE, HV // LANE, SEQ, LANE), lambda b: (b, V_OFF // HV, 0, 0)),
            pl.BlockSpec((NE, HV // LANE, SEQ, LANE), lambda b: (b, OG_OFF // HV, 0, 0)),
            pl.BlockSpec((NE, SEQ, LANE), lambda b: (b, 0, 0)),
            pl.BlockSpec((LANE, HK), lambda b: (0, 0)),
            pl.BlockSpec((LANE, HK), lambda b: (0, 0)),
            pl.BlockSpec((1, HK), lambda b: (0, 0)),
            pl.BlockSpec((1, GLA_DV), lambda b: (0, 0)),
        ],
        out_specs=pl.BlockSpec((NE, SEQ, HV), lambda b: (b, 0, 0)),
        out_shape=jax.ShapeDtypeStruct((B, SEQ, HV), bf16),
        scratch_shapes=[
            pltpu.VMEM((NE, SEQ, HK), f32),
            pltpu.VMEM((NE, GLA_HEADS, GLA_DK, GLA_DV), f32),
            pltpu.VMEM((NE, HK, HV), bf16),
        ],
        compiler_params=_cparams(("parallel",)),
        name="gla",
    )(proj, proj, proj, proj, glow, wup_hi, wup_lo, bg, gain)


LRU_SEG = 8
LRU_SEGLEN = SEQ // LRU_SEG
LRU_PAD = 8


def _lru_body(xb_ref, yb_ref, cw_ref, cb_ref, w_ref, b_ref, lam_ref, o_ref, a_scr, u_scr, hs_scr, as_scr):
    as_scr[pl.ds(0, LRU_PAD), :] = jnp.zeros((LRU_PAD, LANE), f32)
    as_scr[pl.ds(LRU_PAD, SEQ), :] = xb_ref[0].astype(f32)
    xc = cb_ref[...]
    for j in range(LRU_CONV):
        xc = xc + as_scr[pl.ds(LRU_PAD - j, SEQ), :] * cw_ref[LRU_CONV - 1 - j:LRU_CONV - j, :]
    z = jnp.dot(xc.astype(bf16), w_ref[0], preferred_element_type=f32) + b_ref[0]
    r = jax.nn.sigmoid(z[:, :LANE])
    gi = jax.nn.sigmoid(z[:, LANE:])
    nl = -lam_ref[...]
    sp = jnp.maximum(nl, 0.0) + jnp.log1p(jnp.exp(-jnp.abs(nl)))
    log_a = (-LRU_C) * r * sp
    a = jnp.exp(log_a)
    u = jnp.sqrt(jnp.tanh(-log_a) * (a * a + 1.0)) * gi * xc
    a_scr[...] = a
    u_scr[...] = u

    def step(t, carry):
        h, acc = carry
        idx = pl.ds(t, LRU_SEG, stride=LRU_SEGLEN)
        at = a_scr[idx, :]
        h = at * h + u_scr[idx, :]
        acc = at * acc
        hs_scr[idx, :] = h
        as_scr[idx, :] = acc
        return h, acc

    hfin, afin = lax.fori_loop(0, LRU_SEGLEN, step, (jnp.zeros((LRU_SEG, LANE), f32), jnp.ones((LRU_SEG, LANE), f32)),
                               unroll=8)
    cin = jnp.zeros((1, LANE), f32)
    for s in range(LRU_SEG):
        rows = pl.ds(s * LRU_SEGLEN, LRU_SEGLEN)
        h = hs_scr[rows, :] + as_scr[rows, :] * cin
        o_ref[0, rows, :] = (h * jax.nn.gelu(yb_ref[0, rows, :].astype(f32))).astype(bf16)
        cin = hfin[s:s + 1, :] + afin[s:s + 1, :] * cin


def _lru(proj, cw, cb, w, b, lam):
    B = proj.shape[0]
    nt = LRU_W // LANE
    return pl.pallas_call(
        _lru_body,
        grid=(B, nt),
        in_specs=[
            pl.BlockSpec((None, 1, SEQ, LANE), lambda b, t: (b, XB_OFF // LANE + t, 0, 0)),
            pl.BlockSpec((None, 1, SEQ, LANE), lambda b, t: (b, YB_OFF // LANE + t, 0, 0)),
            pl.BlockSpec((LRU_CONV, LANE), lambda b, t: (0, t)),
            pl.BlockSpec((1, LANE), lambda b, t: (0, t)),
            pl.BlockSpec((1, LANE, 2 * LANE), lambda b, t: (t, 0, 0)),
            pl.BlockSpec((1, 1, 2 * LANE), lambda b, t: (t, 0, 0)),
            pl.BlockSpec((1, LANE), lambda b, t: (0, t)),
        ],
        out_specs=pl.BlockSpec((1, SEQ, LANE), lambda b, t: (b, 0, t)),
        out_shape=jax.ShapeDtypeStruct((B, SEQ, LRU_W), bf16),
        scratch_shapes=[pltpu.VMEM((SEQ, LANE), f32)] * 3 + [pltpu.VMEM((SEQ + LRU_PAD, LANE), f32)],
        compiler_params=_cparams(("parallel", "parallel")),
        name="lru",
    )(proj, proj, cw, cb, w, b, lam)


def _rope_body(pos_ref, freq_ref, cos_ref, ss_ref):
    ang = pos_ref[0] * freq_ref[...]
    lane = lax.broadcasted_iota(i32, (SEQ, LANE), 1)
    sn = jnp.sin(ang)
    cos_ref[0] = jnp.cos(ang)
    ss_ref[0] = jnp.where(lane < ROPE_DIMS // 2, -sn, sn)


def _rope_tables(posf, freq):
    B = posf.shape[0]
    spec = pl.BlockSpec((1, SEQ, LANE), lambda b: (b, 0, 0))
    return pl.pallas_call(
        _rope_body,
        grid=(B,),
        in_specs=[pl.BlockSpec((1, SEQ, 1), lambda b: (b, 0, 0)), pl.BlockSpec((1, LANE), lambda b: (0, 0))],
        out_specs=[spec, spec],
        out_shape=[jax.ShapeDtypeStruct((B, SEQ, LANE), f32)] * 2,
        compiler_params=_cparams(("parallel",)),
        name="rope_tables",
    )(posf, freq)


def _rope_partner():
    half = ROPE_DIMS // 2
    perm = np.arange(LANE)
    perm[:half] += half
    perm[half:ROPE_DIMS] -= half
    return perm


ATTN_PC = 256
ATTN_G = 4


def _attn_body(q0, q1, q2, k0, k1, k2, v0, v1, v2, cos_ref, ss_ref, g_ref, p2_ref, o_ref,
               qd, kd, vd, tq, tk, tv, od, ld, on, ln, sc_a, sc_b):
    q_refs, k_refs, v_refs = (q0, q1, q2), (k0, k1, k2), (v0, v1, v2)
    BLK = ATTN_BLK
    qg = g_ref[0:1, :] * (DIL_DH ** -0.5)
    qgr = g_ref[1:2, :] * (DIL_DH ** -0.5)
    kg = g_ref[2:3, :]
    kgr = g_ref[3:4, :]
    kd[pl.ds(0, BLK), :] = jnp.zeros((BLK, LANE), bf16)
    vd[pl.ds(0, BLK), pl.ds(0, LANE)] = jnp.zeros((BLK, LANE), bf16)
    vd[:, pl.ds(LANE, LANE)] = jnp.ones((SEQ + BLK, LANE), bf16)

    qi = lax.broadcasted_iota(i32, (BLK, 2 * BLK), 0)
    kj = lax.broadcasted_iota(i32, (BLK, 2 * BLK), 1)
    is_cur = kj >= BLK
    band = jnp.where(is_cur, kj - BLK, qi) <= jnp.where(is_cur, qi, kj)

    def norm_rope(raw, partner, gain, gain_p, cs, ss):
        rf = raw.astype(f32)
        r = lax.rsqrt(jnp.mean(rf * rf, axis=-1, keepdims=True) + RMS_EPS)
        return (rf * (gain * cs) + partner * (gain_p * ss)) * r

    for p, (_, dil) in enumerate(DIL_PATTERNS):
        L = SEQ // dil
        nbl = L // BLK

        def prep(c, carry, p=p, dil=dil):
            rows = pl.ds(pl.multiple_of(c * ATTN_PC, ATTN_PC), ATTN_PC)
            cs, ss = cos_ref[0, rows, :], ss_ref[0, rows, :]
            qr = q_refs[p][0, rows, :]
            kr = k_refs[p][0, rows, :]
            partner = jnp.dot(jnp.concatenate([qr, kr], axis=1), p2_ref[...], preferred_element_type=f32)
            q = norm_rope(qr, partner[:, :LANE], qg, qgr, cs, ss)
            k = norm_rope(kr, partner[:, LANE:], kg, kgr, cs, ss)
            if dil == 1:
                orow = pl.ds(pl.multiple_of(c * ATTN_PC, ATTN_PC) + BLK, ATTN_PC)
                qd[rows, :] = q.astype(bf16)
                kd[orow, :] = k.astype(bf16)
                vd[orow, pl.ds(0, LANE)] = v_refs[p][0, rows, :]
            else:
                tq[rows, :] = q
                tk[rows, :] = k
                tv[rows, :] = v_refs[p][0, rows, :].astype(f32)
            return carry

        lax.fori_loop(0, SEQ // ATTN_PC, prep, 0, unroll=2)
        if dil > 1:
            for r in range(dil):
                src = pl.ds(r, L, stride=dil)
                qd[pl.ds(r * L, L), :] = tq[src, :].astype(bf16)
                kd[pl.ds(BLK + r * L, L), :] = tk[src, :].astype(bf16)
                vd[pl.ds(BLK + r * L, L), pl.ds(0, LANE)] = tv[src, :].astype(bf16)

        o_dst, l_dst = (on.at[p], ln.at[p]) if dil == 1 else (od, ld)

        def scores(g, dst):
            for j in range(ATTN_G):
                r0 = pl.multiple_of((g * ATTN_G + j) * BLK, BLK)
                dst[pl.ds(j * BLK, BLK), :] = lax.dot_general(qd[pl.ds(r0, BLK), :], kd[pl.ds(r0, 2 * BLK), :], NT,
                                                              preferred_element_type=f32)

        def finish(g, src, nbl=nbl, o_dst=o_dst, l_dst=l_dst):
            for j in range(ATTN_G):
                b = g * ATTN_G + j
                r0 = pl.multiple_of(b * BLK, BLK)
                first = (b % nbl) == 0
                valid = band & (is_cur | jnp.logical_not(first))
                s = jnp.where(valid, src[pl.ds(j * BLK, BLK), :], -jnp.inf)
                m = jnp.max(s, axis=-1, keepdims=True)
                e = jnp.exp(s - m).astype(bf16)
                ov = jnp.dot(e, vd[pl.ds(r0, 2 * BLK), :], preferred_element_type=f32)
                den = ov[:, LANE:]
                o_dst[pl.ds(r0, BLK), :] = ov[:, :LANE] / den
                l_dst[pl.ds(r0, BLK), :] = m + jnp.log(den)

        n_groups = SEQ // BLK // ATTN_G
        scores(0, sc_a)

        def pair(i, carry, scores=scores, finish=finish):
            scores(2 * i + 1, sc_b)
            finish(2 * i, sc_a)
            scores(jnp.minimum(2 * i + 2, n_groups - 1), sc_a)
            finish(2 * i + 1, sc_b)
            return carry

        lax.fori_loop(0, n_groups // 2, pair, 0)
        if dil > 1:
            for r in range(dil):
                dst = pl.ds(r, L, stride=dil)
                on[p, dst, :] = od[pl.ds(r * L, L), :]
                ln[p, dst, :] = ld[pl.ds(r * L, L), :]

    def combine(c, carry):
        rows = pl.ds(pl.multiple_of(c * ATTN_PC, ATTN_PC), ATTN_PC)
        l0, l1, l2 = ln[0, rows, :], ln[1, rows, :], ln[2, rows, :]
        mx = jnp.maximum(jnp.maximum(l0, l1), l2)
        w0, w1, w2 = jnp.exp(l0 - mx), jnp.exp(l1 - mx), jnp.exp(l2 - mx)
        o = (w0 * on[0, rows, :] + w1 * on[1, rows, :] + w2 * on[2, rows, :]) / (w0 + w1 + w2)
        o_ref[0, rows, :] = o.astype(bf16)
        return carry

    lax.fori_loop(0, SEQ // ATTN_PC, combine, 0)


def _attn(proj, cos, ss, qg, kg, b0):
    B = proj.shape[0]
    base = DIL_OFF // LANE
    perm = _rope_partner()
    gains = jnp.stack([qg, qg[perm], kg, kg[perm]]).astype(f32)
    pm = np.zeros((LANE, LANE), np.float32)
    pm[perm[:ROPE_DIMS], np.arange(ROPE_DIMS)] = 1.0
    p2 = jnp.asarray(np.kron(np.eye(2, dtype=np.float32), pm), dtype=bf16)

    def pspec(qkv, p):
        return pl.BlockSpec((None, 1, SEQ, LANE), lambda b, h, o=base + (qkv * N_PAT + p) * DIL_HEADS: (b, o + h, 0, 0))

    tspec = pl.BlockSpec((1, SEQ, LANE), lambda b, h: (b + b0, 0, 0))
    in_specs = [pspec(qkv, p) for qkv in range(3) for p in range(N_PAT)] + [
        tspec, tspec, pl.BlockSpec((4, LANE), lambda b, h: (0, 0)), pl.BlockSpec((2 * LANE, 2 * LANE), lambda b, h: (0, 0))]
    return pl.pallas_call(
        _attn_body,
        grid=(B, DIL_HEADS),
        in_specs=in_specs,
        out_specs=pl.BlockSpec((1, SEQ, LANE), lambda b, h: (b, 0, h)),
        out_shape=jax.ShapeDtypeStruct((B, SEQ, DIL_HEADS * DIL_DH), bf16),
        scratch_shapes=[
            pltpu.VMEM((SEQ, LANE), bf16),
            pltpu.VMEM((SEQ + ATTN_BLK, LANE), bf16),
            pltpu.VMEM((SEQ + ATTN_BLK, 2 * LANE), bf16),
            pltpu.VMEM((SEQ, LANE), f32),
            pltpu.VMEM((SEQ, LANE), f32),
            pltpu.VMEM((SEQ, LANE), f32),
            pltpu.VMEM((SEQ, LANE), f32),
            pltpu.VMEM((SEQ, LANE), f32),
            pltpu.VMEM((N_PAT, SEQ, LANE), f32),
            pltpu.VMEM((N_PAT, SEQ, LANE), f32),
            pltpu.VMEM((ATTN_G * ATTN_BLK, 2 * ATTN_BLK), f32),
            pltpu.VMEM((ATTN_G * ATTN_BLK, 2 * ATTN_BLK), f32),
        ],
        compiler_params=_cparams(("parallel", "arbitrary")),
        name="dilated_attn",
    )(*([proj] * 9), cos, ss, gains, p2)


def _merge_body(a_ref, b_ref, c_ref, g0_ref, g1_ref, g2_ref, x_ref, wb_ref, wo_ref, n2_ref, rw_ref, rb_ref, tri_ref,
                x1_ref, h2_ref, ei_ref, ew_ref, pos_ref, cnt_ref, cb_ref, cnt_scr):
    for c in range(MERGE_TM // MERGE_RC):
        rows = pl.ds(c * MERGE_RC, MERGE_RC)
        merged = None
        for n, (br, g) in enumerate(((a_ref, g0_ref), (b_ref, g1_ref), (c_ref, g2_ref))):
            logits = jnp.concatenate([g[t, rows, :] for t in range(D_MODEL // LANE)], axis=1)
            gate = 0.5 * jnp.tanh(0.5 * logits.astype(f32)) + 0.5
            t = gate * jnp.dot(br[rows, :], wb_ref[n], preferred_element_type=f32)
            merged = t if merged is None else merged + t
        x1 = x_ref[rows, :] + jnp.dot(merged.astype(bf16), wo_ref[...], preferred_element_type=f32)
        x1_ref[rows, :] = x1
        h2_ref[rows, :] = ((x1 * lax.rsqrt(jnp.mean(x1 * x1, axis=-1, keepdims=True) + RMS_EPS))
                           * n2_ref[...]).astype(bf16)
    h2 = h2_ref[...]

    lg = lax.dot_general(rw_ref[...], h2, NT, preferred_element_type=f32) + rb_ref[...]
    ex = jnp.exp(lg - jnp.max(lg, axis=0, keepdims=True))
    sc = ex / jnp.sum(ex, axis=0, keepdims=True)
    srow = [sc[e:e + 1, :] for e in range(N_EXP)]
    best = bidx = None
    for gidx in range(N_GRP):
        s0, s1, s2, s3 = srow[4 * gidx:4 * gidx + 4]
        hi01, lo01, hi23, lo23 = jnp.maximum(s0, s1), jnp.minimum(s0, s1), jnp.maximum(s2, s3), jnp.minimum(s2, s3)
        gs = jnp.maximum(hi01, hi23) + jnp.maximum(jnp.minimum(hi01, hi23), jnp.maximum(lo01, lo23))
        if best is None:
            best, bidx = gs, jnp.zeros_like(gs, dtype=i32)
        else:
            better = gs > best
            best = jnp.where(better, gs, best)
            bidx = jnp.where(better, gidx, bidx)
    cand = []
    for i in range(EXP_PER_GRP):
        v = srow[i]
        for gidx in range(1, N_GRP):
            v = jnp.where(bidx == gidx, srow[4 * gidx + i], v)
        cand.append(v)

    def argmax4(vals):
        bw, bi = vals[0], jnp.zeros_like(bidx)
        for i in range(1, EXP_PER_GRP):
            better = vals[i] > bw
            bw = jnp.where(better, vals[i], bw)
            bi = jnp.where(better, i, bi)
        return bw, bi

    w1, i1 = argmax4(cand)
    w2, i2 = argmax4([jnp.where(i1 == i, -jnp.inf, cand[i]) for i in range(EXP_PER_GRP)])
    tot = w1 + w2
    e1 = bidx * EXP_PER_GRP + i1
    e2 = bidx * EXP_PER_GRP + i2
    ei_ref[0:1, :] = e1
    ei_ref[1:2, :] = e2
    ew_ref[0:1, :] = w1 / tot
    ew_ref[1:2, :] = w2 / tot

    @pl.when(pl.program_id(0) == 0)
    def _():
        cnt_scr[...] = jnp.zeros_like(cnt_scr)

    erow = lax.broadcasted_iota(i32, (N_EXP, MERGE_TM), 0)
    oh1 = erow == e1
    oh2 = erow == e2
    both = jnp.where(oh1 | oh2, 1.0, 0.0)
    incl = jnp.dot(both.astype(bf16), tri_ref[...], preferred_element_type=f32)
    cb_ref[0] = cnt_scr[...]
    base = cnt_scr[...] + (incl - both)
    pos_ref[0:1, :] = jnp.sum(jnp.where(oh1, base, 0.0), axis=0, keepdims=True).astype(i32)
    pos_ref[1:2, :] = jnp.sum(jnp.where(oh2, base, 0.0), axis=0, keepdims=True).astype(i32)
    cnt = cnt_scr[...] + incl[:, MERGE_TM - 1:MERGE_TM]
    cnt_scr[...] = cnt
    cnt_ref[...] = cnt


def _merge(bra, brb, brc, proj, x2d, x_row0, wb, wo, n2, rw, rb):
    T = bra.shape[0]
    TM = MERGE_TM
    x_blk0 = x_row0 // TM
    per_b = SEQ // TM
    gt = D_MODEL // LANE

    def gspec(n):
        return pl.BlockSpec((None, gt, TM, LANE), lambda i: (i // per_b, GATE_OFF // D_MODEL + n, i % per_b, 0))

    rspec = lambda w: pl.BlockSpec((TM, w), lambda i: (i, 0))
    full = lambda shape: pl.BlockSpec(shape, lambda i: (0,) * len(shape))
    tri = (jnp.arange(TM)[:, None] <= jnp.arange(TM)[None, :]).astype(bf16)
    return pl.pallas_call(
        _merge_body,
        grid=(T // TM,),
        in_specs=[
            rspec(BR_W), rspec(BR_W), rspec(BR_W),
            gspec(0), gspec(1), gspec(2),
            pl.BlockSpec((TM, D_MODEL), lambda i: (i + x_blk0, 0)),
            full((N_BR, BR_W, D_MODEL)), full((D_MODEL, D_MODEL)), full((1, D_MODEL)),
            full((N_EXP, D_MODEL)), full((N_EXP, 1)), full((TM, TM)),
        ],
        out_specs=[
            rspec(D_MODEL), rspec(D_MODEL),
            pl.BlockSpec((TOP_K, TM), lambda i: (0, i)),
            pl.BlockSpec((TOP_K, TM), lambda i: (0, i)),
            pl.BlockSpec((TOP_K, TM), lambda i: (0, i)),
            full((N_EXP, 1)),
            pl.BlockSpec((1, N_EXP, 1), lambda i: (i, 0, 0)),
        ],
        out_shape=[
            jax.ShapeDtypeStruct((T, D_MODEL), f32),
            jax.ShapeDtypeStruct((T, D_MODEL), bf16),
            jax.ShapeDtypeStruct((TOP_K, T), i32),
            jax.ShapeDtypeStruct((TOP_K, T), f32),
            jax.ShapeDtypeStruct((TOP_K, T), i32),
            jax.ShapeDtypeStruct((N_EXP, 1), f32),
            jax.ShapeDtypeStruct((T // TM, N_EXP, 1), f32),
        ],
        scratch_shapes=[pltpu.VMEM((N_EXP, 1), f32)],
        compiler_params=_cparams(("arbitrary",)),
        name="merge_router",
    )(bra, brb, brc, proj, proj, proj, x2d, wb, wo, n2, rw, rb, tri)


def _expert_body(be_ref, nu_ref, xs_ref, wgu_ref, wd_ref, ys_ref):
    i = pl.program_id(0)

    @pl.when(i < nu_ref[0])
    def _():
        gu = jnp.dot(xs_ref[...], wgu_ref[0], preferred_element_type=f32)
        g = gu[:, :EXP_FF]
        act = (g * jax.nn.sigmoid(g)) * gu[:, EXP_FF:]
        ys_ref[...] = jnp.dot(act.astype(bf16), wd_ref[0], preferred_element_type=f32).astype(bf16)

    @pl.when(i >= nu_ref[0])
    def _():
        ys_ref[...] = jnp.zeros_like(ys_ref)


def _experts(block_e, n_used, xs, wgu, wd):
    n_rows = xs.shape[0]
    grid_spec = pltpu.PrefetchScalarGridSpec(
        num_scalar_prefetch=2,
        grid=(n_rows // MOE_BM,),
        in_specs=[
            pl.BlockSpec((MOE_BM, D_MODEL), lambda i, be, nu: (jnp.minimum(i, nu[0] - 1), 0)),
            pl.BlockSpec((1, D_MODEL, 2 * EXP_FF), lambda i, be, nu: (be[i], 0, 0)),
            pl.BlockSpec((1, EXP_FF, D_MODEL), lambda i, be, nu: (be[i], 0, 0)),
        ],
        out_specs=pl.BlockSpec((MOE_BM, D_MODEL), lambda i, be, nu: (i, 0)),
    )
    return pl.pallas_call(
        _expert_body,
        grid_spec=grid_spec,
        out_shape=jax.ShapeDtypeStruct((n_rows, D_MODEL), bf16),
        compiler_params=_cparams(("arbitrary",)),
        name="experts",
    )(block_e, n_used, xs, wgu, wd)


COMB_TM = 1024


def _combine_body(x1_ref, y0_ref, y1_ref, w_ref, *rest):
    o_ref = rest[-1]
    w = w_ref[...]
    o_ref[...] = x1_ref[...] + y0_ref[...].astype(f32) * w[:, 0:1] + y1_ref[...].astype(f32) * w[:, 1:2]


def _combine(x1, y0, y1, w, out_rows=None, row0=0, out_prev=None):
    T = x1.shape[0]
    out_rows = T if out_rows is None else out_rows
    blk0 = row0 // COMB_TM
    rspec = pl.BlockSpec((COMB_TM, D_MODEL), lambda i: (i, 0))
    in_specs = [rspec, rspec, rspec, pl.BlockSpec((COMB_TM, TOP_K), lambda i: (i, 0))]
    args = [x1, y0, y1, w]
    aliases = {}
    if out_prev is not None:
        in_specs.append(pl.BlockSpec(memory_space=pl.ANY))
        args.append(out_prev)
        aliases = {4: 0}
    return pl.pallas_call(
        _combine_body,
        grid=(T // COMB_TM,),
        in_specs=in_specs,
        out_specs=pl.BlockSpec((COMB_TM, D_MODEL), lambda i: (i + blk0, 0)),
        out_shape=jax.ShapeDtypeStruct((out_rows, D_MODEL), f32),
        input_output_aliases=aliases,
        compiler_params=_cparams(("parallel",)),
        name="moe_combine",
    )(*args)


DISP_ALIGN = 16
DISP_SLOTS = MERGE_TM * TOP_K + N_EXP * DISP_ALIGN
DISP_SIZES = tuple(DISP_ALIGN << s for s in range(5, -1, -1))


def _pieces(length, sizes, fn):
    done = jnp.int32(0)
    for sz in sizes:
        take = (length & sz) != 0

        @pl.when(take)
        def _(done=done, sz=sz):
            fn(done, sz)

        done = done + jnp.where(take, sz, 0)


def _dispatch_body(off_ref, n_ref, ls_ref, toff_ref, tlen_ref, h2_ref, ei_ref, pos_ref, adj_ref, xs_hbm,
                   srt, zbuf, sem, zsem):
    i = pl.program_id(0)
    last = pl.num_programs(0) - 1
    cur = i % 2
    TM = MERGE_TM

    def slab_copies(step, buf, start):
        for e in range(N_EXP):
            ls = ls_ref[step * N_EXP + e]
            off = off_ref[step * N_EXP + e]

            def piece(done, sz, ls=ls, off=off):
                cp = pltpu.make_async_copy(srt.at[buf, pl.ds(pl.multiple_of(ls + done, DISP_ALIGN), sz)],
                                           xs_hbm.at[pl.ds(pl.multiple_of(off + done, DISP_ALIGN), sz)], sem.at[buf])
                cp.start() if start else cp.wait()

            _pieces(n_ref[step * N_EXP + e], DISP_SIZES, piece)

    def tail_copies(start):
        for e in range(N_EXP):
            off = toff_ref[e]

            def piece(done, sz, off=off):
                cp = pltpu.make_async_copy(zbuf.at[pl.ds(0, sz)],
                                           xs_hbm.at[pl.ds(pl.multiple_of(off + done, DISP_ALIGN), sz)], zsem)
                cp.start() if start else cp.wait()

            _pieces(tlen_ref[e], DISP_SIZES[1:], piece)

        zrows = zbuf.shape[0]

        def block(c, carry):
            cp = pltpu.make_async_copy(zbuf, xs_hbm.at[pl.ds(pl.multiple_of(toff_ref[N_EXP] + c * zrows, zrows), zrows)],
                                       zsem)
            cp.start() if start else cp.wait()
            return carry

        lax.fori_loop(0, tlen_ref[N_EXP] // zrows, block, 0)

    @pl.when(i == 0)
    def _():
        zbuf[...] = jnp.zeros_like(zbuf)
        tail_copies(True)

    @pl.when(i >= 2)
    def _():
        slab_copies(i - 2, cur, False)

    erow = lax.broadcasted_iota(i32, (N_EXP, TM), 0)
    adj = adj_ref[0]

    def slot(k):
        sel = jnp.sum(jnp.where(erow == ei_ref[k:k + 1, :], adj, 0.0), axis=0, keepdims=True)
        return pos_ref[k:k + 1, :] + sel.astype(i32)

    srow = lax.broadcasted_iota(i32, (DISP_SLOTS, TM), 0)
    perm = jnp.where((srow == slot(0)) | (srow == slot(1)), 1.0, 0.0).astype(bf16)
    srt[cur] = jnp.dot(perm, h2_ref[...], preferred_element_type=f32).astype(bf16)
    slab_copies(i, cur, True)

    @pl.when(i == last)
    def _():
        slab_copies(i, cur, False)

        @pl.when(i >= 1)
        def _():
            slab_copies(i - 1, 1 - cur, False)

        tail_copies(False)


def _dispatch(off, n16, lstart, tail_off, tail_len, h2, eidx, pos, adj, n_rows):
    T = h2.shape[0]
    TM = MERGE_TM
    grid_spec = pltpu.PrefetchScalarGridSpec(
        num_scalar_prefetch=5,
        grid=(T // TM,),
        in_specs=[
            pl.BlockSpec((TM, D_MODEL), lambda i, *_: (i, 0)),
            pl.BlockSpec((TOP_K, TM), lambda i, *_: (0, i)),
            pl.BlockSpec((TOP_K, TM), lambda i, *_: (0, i)),
            pl.BlockSpec((1, N_EXP, 1), lambda i, *_: (i, 0, 0)),
        ],
        out_specs=pl.BlockSpec(memory_space=pl.ANY),
        scratch_shapes=[
            pltpu.VMEM((2, DISP_SLOTS, D_MODEL), bf16),
            pltpu.VMEM((DISP_SIZES[1], D_MODEL), bf16),
            pltpu.SemaphoreType.DMA((2,)),
            pltpu.SemaphoreType.DMA(()),
        ],
    )
    return pl.pallas_call(
        _dispatch_body,
        grid_spec=grid_spec,
        out_shape=jax.ShapeDtypeStruct((n_rows, D_MODEL), bf16),
        compiler_params=_cparams(("arbitrary",)),
        name="moe_dispatch",
    )(off, n16, lstart, tail_off, tail_len, h2, eidx, pos, adj)


def _moe(x1, h2, eidx, ew, pos, cnt, cb, wgu, wd, **out_kw):
    T = x1.shape[0]
    TM = MERGE_TM
    n_tiles = T // TM
    ceil_to = lambda v, m: (v + m - 1) // m * m
    n_rows = ceil_to(T * TOP_K + n_tiles * N_EXP * (DISP_ALIGN - 1) + N_EXP * (MOE_BM - 1), MOE_BM)
    n_blocks = n_rows // MOE_BM
    cbi = cb[:, :, 0].astype(i32)
    tile_cnt = jnp.concatenate([cbi[1:], cnt[:, 0].astype(i32)[None, :]], axis=0) - cbi
    n16 = ceil_to(tile_cnt, DISP_ALIGN)
    lstart = jnp.cumsum(n16, axis=1) - n16
    before = jnp.cumsum(n16, axis=0) - n16
    used = jnp.sum(n16, axis=0)
    padded = ceil_to(used, MOE_BM)
    pad_ends = jnp.cumsum(padded)
    off = (pad_ends - padded)[None, :] + before
    adj = (lstart - cbi).astype(f32)[:, :, None]
    row_adj = jnp.repeat(off - cbi, TM, axis=0).T
    dest = pos
    for e in range(N_EXP):
        dest = dest + jnp.where(eidx == e, row_adj[e][None, :], 0)
    blk_start = jnp.arange(n_blocks, dtype=i32) * MOE_BM
    block_e = jnp.minimum(jnp.sum((blk_start[:, None] >= pad_ends[None, :]).astype(i32), axis=1), N_EXP - 1)
    n_used = (pad_ends[-1:] // MOE_BM).astype(i32)
    tail_off = jnp.concatenate([pad_ends - padded + used, pad_ends[-1:]])
    tail_len = jnp.concatenate([padded - used, n_rows - pad_ends[-1:]])
    xs = _dispatch(off.reshape(-1), n16.reshape(-1), lstart.reshape(-1), tail_off, tail_len, h2, eidx, pos, adj, n_rows)
    ys = _experts(block_e, n_used, xs, wgu, wd)
    return _combine(x1, ys[dest[0]], ys[dest[1]], ew.T, **out_kw)


def _pack_w_in(w):
    cuts = np.cumsum([256, 256, 512, 512, 16, 512, 512, 4608, 3072])[:-1].tolist()
    q, k, v, og, glow, xb, yb, qkv, gates = jnp.split(w, cuts, axis=-1)
    packed = jnp.concatenate([gates, qkv, xb, yb, q, k, v, og], axis=-1).astype(bf16)
    wg = jnp.pad(glow, ((0, 0), (0, LANE - GLA_RANK))).astype(bf16)
    return packed, wg


def _block_diag_lru(w_a, w_x):
    bs = LRU_W // LRU_BLOCKS
    per = LANE // bs
    tiles = []
    for t in range(LRU_W // LANE):
        halves = []
        for w in (w_a, w_x):
            m = jnp.zeros((LANE, LANE), f32)
            for j in range(per):
                m = m.at[j * bs:(j + 1) * bs, j * bs:(j + 1) * bs].set(w[t * per + j])
            halves.append(m)
        tiles.append(jnp.concatenate(halves, axis=1))
    return jnp.stack(tiles).astype(bf16)


def kernel(x, positions, norm1_gain, w_in, gla_gate_up, gla_gate_bias, gla_out_gain, lru_conv_w, lru_conv_b, lru_w_a, lru_b_a, lru_w_x, lru_b_x, lru_lambda, q_norm_gain, k_norm_gain, w_branch, w_out, norm2_gain, router_w, router_b, w_gate, w_up, w_down):
    B, S, D = x.shape
    assert S == SEQ and D == D_MODEL
    T = B * S
    depth = w_in.shape[0]

    half = ROPE_DIMS // 2
    inv_freq = ROPE_THETA ** (-jnp.arange(half, dtype=f32) / half)
    freq = jnp.zeros((1, LANE), f32).at[0, :ROPE_DIMS].set(jnp.concatenate([inv_freq, inv_freq]))
    cos, ss = _rope_tables(positions.astype(f32)[:, :, None], freq)

    rw = router_w.T.astype(bf16)
    rb = router_b.astype(f32)[:, None]
    n_str = N_STREAMS if B % N_STREAMS == 0 else 1
    Bs = B // n_str
    Ts = Bs * S
    xs_ = [x] * n_str
    out = None
    for l in range(depth):
        w_packed, wg = _pack_w_in(w_in[l])
        wup = jnp.pad(gla_gate_up[l], ((0, LANE - GLA_RANK), (0, 0)))
        nt = LRU_W // LANE
        lru_w = _block_diag_lru(lru_w_a[l], lru_w_x[l])
        lru_b = jnp.concatenate([lru_b_a[l].reshape(nt, 1, LANE), lru_b_x[l].reshape(nt, 1, LANE)], axis=-1)
        wb, wo = w_branch[l].astype(bf16), w_out[l].astype(bf16)
        wgu = jnp.concatenate([w_gate[l], w_up[l]], axis=-1).astype(bf16)
        wd = w_down[l].astype(bf16)
        last = l == depth - 1
        for s in range(n_str):
            b0 = s * Bs
            xin = xs_[s]
            shared_in = xin.shape[0] == B and n_str > 1
            proj, glow = _inproj(xin, norm1_gain[l][None, :], w_packed, wg, b0 if shared_in else 0, Bs)
            br_a = _gla(proj, glow, wup, gla_gate_bias[l][None, :], gla_out_gain[l][None, :])
            br_b = _lru(proj, lru_conv_w[l], lru_conv_b[l][None, :], lru_w, lru_b, lru_lambda[l][None, :])
            br_c = _attn(proj, cos, ss, q_norm_gain[l], k_norm_gain[l], b0)
            routed = _merge(br_a.reshape(Ts, BR_W), br_b.reshape(Ts, BR_W), br_c.reshape(Ts, BR_W),
                            proj, xin.reshape(-1, D),
                            b0 * S if shared_in else 0, wb, wo, norm2_gain[l][None, :], rw, rb)
            if last:
                out = _moe(*routed, wgu, wd, out_rows=T, row0=b0 * S, out_prev=out)
            else:
                xs_[s] = _moe(*routed, wgu, wd).reshape(Bs, S, D)
    return out.reshape(B, S, D)
```

```python
import jax
import jax.numpy as jnp
import numpy as np
from jax import lax
from jax.experimental import pallas as pl
from jax.experimental.pallas import tpu as pltpu

f32 = jnp.float32
bf16 = jnp.bfloat16
i32 = jnp.int32

D_MODEL = 1024
SEQ = 2048
RMS_EPS = 1e-6
GLA_HEADS, GLA_DK, GLA_DV, GLA_RANK, GLA_NORM, GLA_CHUNK = 4, 64, 128, 16, 16.0, 64
LRU_W, LRU_BLOCKS, LRU_CONV, LRU_C = 512, 8, 4, 8.0
DIL_PATTERNS = ((128, 1), (512, 4), (2048, 16))
N_PAT, DIL_HEADS, DIL_DH, ATTN_BLK = 3, 4, 128, 128
ROPE_THETA, ROPE_DIMS = 500000.0, 32
N_BR, BR_W = 3, 512
N_EXP, N_GRP, EXP_PER_GRP, TOP_K, EXP_FF = 16, 4, 4, 2, 512

LANE = 128
GATE_OFF, DIL_OFF, XB_OFF, YB_OFF, Q_OFF, K_OFF, V_OFF, OG_OFF = 0, 3072, 7680, 8192, 8704, 8960, 9216, 9728
PROJ_COLS = 10240
PROJ_TN = 1024
PROJ_RC = 512
MERGE_TM = 512
MOE_BM = 512
VMEM_LIMIT = 56 * 1024 * 1024

NT = (((1,), (1,)), ((), ()))


def _cparams(sem):
    return pltpu.CompilerParams(dimension_semantics=sem, vmem_limit_bytes=VMEM_LIMIT)


def _inproj_body(x_ref, g_ref, w_ref, wg_ref, proj_ref, glow_ref, h_scr):
    j = pl.program_id(1)
    nrc = SEQ // PROJ_RC

    @pl.when(j == 0)
    def _():
        for c in range(nrc):
            rows = pl.ds(c * PROJ_RC, PROJ_RC)
            x = x_ref[0, rows, :]
            ms = jnp.mean(x * x, axis=-1, keepdims=True)
            h = ((x * lax.rsqrt(ms + RMS_EPS)) * g_ref[...]).astype(bf16)
            h_scr[rows, :] = h
            glow_ref[0, rows, :] = jnp.dot(h, wg_ref[...], preferred_element_type=f32)

    for c in range(nrc):
        rows = pl.ds(c * PROJ_RC, PROJ_RC)
        res = jnp.dot(h_scr[rows, :], w_ref[...], preferred_element_type=f32).astype(bf16)
        for t in range(PROJ_TN // LANE):
            proj_ref[0, t, rows, :] = res[:, t * LANE:(t + 1) * LANE]


def _inproj(x, gain, w, wg):
    B = x.shape[0]
    nj = PROJ_COLS // PROJ_TN
    return pl.pallas_call(
        _inproj_body,
        grid=(B, nj),
        in_specs=[
            pl.BlockSpec((1, SEQ, D_MODEL), lambda b, j: (b, 0, 0)),
            pl.BlockSpec((1, D_MODEL), lambda b, j: (0, 0)),
            pl.BlockSpec((D_MODEL, PROJ_TN), lambda b, j: (0, j)),
            pl.BlockSpec((D_MODEL, LANE), lambda b, j: (0, 0)),
        ],
        out_specs=[
            pl.BlockSpec((1, PROJ_TN // LANE, SEQ, LANE), lambda b, j: (b, j, 0, 0)),
            pl.BlockSpec((1, SEQ, LANE), lambda b, j: (b, 0, 0)),
        ],
        out_shape=[
            jax.ShapeDtypeStruct((B, PROJ_COLS // LANE, SEQ, LANE), bf16),
            jax.ShapeDtypeStruct((B, SEQ, LANE), f32),
        ],
        scratch_shapes=[pltpu.VMEM((SEQ, D_MODEL), bf16)],
        compiler_params=_cparams(("parallel", "arbitrary")),
        name="inproj",
    )(x, gain, w, wg)


def _gla_body(q_ref, k_ref, v_ref, og_ref, glow_ref, wh_ref, wl_ref, bg_ref, gain_ref, o_ref, la_scr, st_scr, stb_scr):
    C = GLA_CHUNK
    HK = GLA_HEADS * GLA_DK
    PR = 512
    for c in range(SEQ // PR):
        rows = pl.ds(c * PR, PR)
        g = glow_ref[0, rows, :]
        gh = g.astype(bf16)
        gl = (g - gh.astype(f32)).astype(bf16)
        z = (jnp.dot(gh, wh_ref[...], preferred_element_type=f32)
             + jnp.dot(gl, wh_ref[...], preferred_element_type=f32)
             + jnp.dot(gh, wl_ref[...], preferred_element_type=f32)) + bg_ref[...]
        la_scr[rows, :] = (jnp.minimum(z, 0.0) - jnp.log1p(jnp.exp(-jnp.abs(z)))) * (1.0 / GLA_NORM)
    st_scr[...] = jnp.zeros_like(st_scr)
    stb_scr[...] = jnp.zeros_like(stb_scr)

    causal = (lax.broadcasted_iota(i32, (C, GLA_HEADS * C), 0)
              >= lax.broadcasted_iota(i32, (C, GLA_HEADS * C), 1) % C)
    tri = (lax.broadcasted_iota(i32, (C, C), 0) >= lax.broadcasted_iota(i32, (C, C), 1)).astype(bf16)
    lane_head = lax.broadcasted_iota(i32, (1, HK), 1) // GLA_DK
    gain = gain_ref[...]
    zero_v = jnp.zeros((C, GLA_DV), bf16)

    def chunk(n, carry):
        rows = pl.ds(pl.multiple_of(n * C, C), C)
        la = la_scr[rows, :]
        p1 = la.astype(bf16)
        r1 = la - p1.astype(f32)
        p2 = r1.astype(bf16)
        p3 = (r1 - p2.astype(f32)).astype(bf16)
        c3 = jnp.dot(tri, jnp.concatenate([p1, p2, p3], axis=1), preferred_element_type=f32)
        cum = (c3[:, :HK] + c3[:, HK:2 * HK]) + c3[:, 2 * HK:]
        cum_t = cum.T
        tot_t = cum_t[:, C - 1:C]
        qf = jnp.concatenate([q_ref[0, t, rows, :] for t in range(HK // LANE)], axis=1).astype(f32)
        kf = jnp.concatenate([k_ref[0, t, rows, :] for t in range(HK // LANE)], axis=1).astype(f32)
        qd = (qf * (GLA_DK ** -0.5) * jnp.exp(cum)).astype(bf16)
        ki = kf * jnp.exp(-cum)
        kd_t = (kf.T * jnp.exp(tot_t - cum_t)).astype(bf16)
        dec_t = jnp.exp(tot_t)
        heads = range(GLA_HEADS)
        vcols = [v_ref[0, h, rows, :] for h in heads]
        k_st = jnp.concatenate([jnp.where(lane_head == h, ki, 0.0).astype(bf16) for h in heads], axis=0)
        v_bd = jnp.concatenate([jnp.concatenate([vcols[h] if g == h else zero_v for g in heads], axis=1)
                                for h in heads], axis=0)
        s = lax.dot_general(qd, k_st, NT, preferred_element_type=f32)
        s = jnp.where(causal, s, 0.0).astype(bf16)
        o = jnp.dot(s, v_bd, preferred_element_type=f32) + jnp.dot(qd, stb_scr[...], preferred_element_type=f32)
        for h in heads:
            cols = slice(h * GLA_DV, (h + 1) * GLA_DV)
            hr = slice(h * GLA_DK, (h + 1) * GLA_DK)
            st = st_scr[h] * dec_t[hr, :] + jnp.dot(kd_t[hr, :], vcols[h], preferred_element_type=f32)
            st_scr[h] = st
            stb_scr[hr, cols] = st.astype(bf16)
            oh = o[:, cols]
            oh = oh * lax.rsqrt(jnp.mean(oh * oh, axis=-1, keepdims=True) + RMS_EPS) * gain
            g = og_ref[0, h, rows, :].astype(f32)
            o_ref[0, rows, cols] = (oh * (g * jax.nn.sigmoid(g))).astype(bf16)
        return carry

    lax.fori_loop(0, SEQ // C, chunk, 0, unroll=4)


def _gla(proj, glow, wup, bg, gain):
    B = proj.shape[0]
    HK = GLA_HEADS * GLA_DK
    HV = GLA_HEADS * GLA_DV
    wup_hi = wup.astype(bf16)
    wup_lo = (wup - wup_hi.astype(f32)).astype(bf16)
    return pl.pallas_call(
        _gla_body,
        grid=(B,),
        in_specs=[
            pl.BlockSpec((1, HK // LANE, SEQ, LANE), lambda b: (b, Q_OFF // HK, 0, 0)),
            pl.BlockSpec((1, HK // LANE, SEQ, LANE), lambda b: (b, K_OFF // HK, 0, 0)),
            pl.BlockSpec((1, HV // LANE, SEQ, LANE), lambda b: (b, V_OFF // HV, 0, 0)),
            pl.BlockSpec((1, HV // LANE, SEQ, LANE), lambda b: (b, OG_OFF // HV, 0, 0)),
            pl.BlockSpec((1, SEQ, LANE), lambda b: (b, 0, 0)),
            pl.BlockSpec((LANE, HK), lambda b: (0, 0)),
            pl.BlockSpec((LANE, HK), lambda b: (0, 0)),
            pl.BlockSpec((1, HK), lambda b: (0, 0)),
            pl.BlockSpec((1, GLA_DV), lambda b: (0, 0)),
        ],
        out_specs=pl.BlockSpec((1, SEQ, HV), lambda b: (b, 0, 0)),
        out_shape=jax.ShapeDtypeStruct((B, SEQ, HV), bf16),
        scratch_shapes=[
            pltpu.VMEM((SEQ, HK), f32),
            pltpu.VMEM((GLA_HEADS, GLA_DK, GLA_DV), f32),
            pltpu.VMEM((HK, HV), bf16),
        ],
        compiler_params=_cparams(("parallel",)),
        name="gla",
    )(proj, proj, proj, proj, glow, wup_hi, wup_lo, bg, gain)


LRU_SEG = 8
LRU_SEGLEN = SEQ // LRU_SEG
LRU_PAD = 8


def _lru_body(xb_ref, yb_ref, cw_ref, cb_ref, w_ref, b_ref, lam_ref, o_ref, a_scr, u_scr, hs_scr, as_scr):
    as_scr[pl.ds(0, LRU_PAD), :] = jnp.zeros((LRU_PAD, LANE), f32)
    as_scr[pl.ds(LRU_PAD, SEQ), :] = xb_ref[0].astype(f32)
    xc = cb_ref[...]
    for j in range(LRU_CONV):
        xc = xc + as_scr[pl.ds(LRU_PAD - j, SEQ), :] * cw_ref[LRU_CONV - 1 - j:LRU_CONV - j, :]
    z = jnp.dot(xc.astype(bf16), w_ref[0], preferred_element_type=f32) + b_ref[0]
    r = jax.nn.sigmoid(z[:, :LANE])
    gi = jax.nn.sigmoid(z[:, LANE:])
    nl = -lam_ref[...]
    sp = jnp.maximum(nl, 0.0) + jnp.log1p(jnp.exp(-jnp.abs(nl)))
    log_a = (-LRU_C) * r * sp
    a = jnp.exp(log_a)
    u = jnp.sqrt(jnp.tanh(-log_a) * (a * a + 1.0)) * gi * xc
    a_scr[...] = a
    u_scr[...] = u

    def step(t, carry):
        h, acc = carry
        idx = pl.ds(t, LRU_SEG, stride=LRU_SEGLEN)
        at = a_scr[idx, :]
        h = at * h + u_scr[idx, :]
        acc = at * acc
        hs_scr[idx, :] = h
        as_scr[idx, :] = acc
        return h, acc

    hfin, afin = lax.fori_loop(0, LRU_SEGLEN, step, (jnp.zeros((LRU_SEG, LANE), f32), jnp.ones((LRU_SEG, LANE), f32)),
                               unroll=8)
    cin = jnp.zeros((1, LANE), f32)
    for s in range(LRU_SEG):
        rows = pl.ds(s * LRU_SEGLEN, LRU_SEGLEN)
        h = hs_scr[rows, :] + as_scr[rows, :] * cin
        o_ref[0, rows, :] = (h * jax.nn.gelu(yb_ref[0, rows, :].astype(f32))).astype(bf16)
        cin = hfin[s:s + 1, :] + afin[s:s + 1, :] * cin


def _lru(proj, cw, cb, w, b, lam):
    B = proj.shape[0]
    nt = LRU_W // LANE
    return pl.pallas_call(
        _lru_body,
        grid=(B, nt),
        in_specs=[
            pl.BlockSpec((None, 1, SEQ, LANE), lambda b, t: (b, XB_OFF // LANE + t, 0, 0)),
            pl.BlockSpec((None, 1, SEQ, LANE), lambda b, t: (b, YB_OFF // LANE + t, 0, 0)),
            pl.BlockSpec((LRU_CONV, LANE), lambda b, t: (0, t)),
            pl.BlockSpec((1, LANE), lambda b, t: (0, t)),
            pl.BlockSpec((1, LANE, 2 * LANE), lambda b, t: (t, 0, 0)),
            pl.BlockSpec((1, 1, 2 * LANE), lambda b, t: (t, 0, 0)),
            pl.BlockSpec((1, LANE), lambda b, t: (0, t)),
        ],
        out_specs=pl.BlockSpec((1, SEQ, LANE), lambda b, t: (b, 0, t)),
        out_shape=jax.ShapeDtypeStruct((B, SEQ, LRU_W), bf16),
        scratch_shapes=[pltpu.VMEM((SEQ, LANE), f32)] * 3 + [pltpu.VMEM((SEQ + LRU_PAD, LANE), f32)],
        compiler_params=_cparams(("parallel", "parallel")),
        name="lru",
    )(proj, proj, cw, cb, w, b, lam)


def _rope_body(pos_ref, freq_ref, cos_ref, ss_ref):
    ang = pos_ref[0] * freq_ref[...]
    lane = lax.broadcasted_iota(i32, (SEQ, LANE), 1)
    sn = jnp.sin(ang)
    cos_ref[0] = jnp.cos(ang)
    ss_ref[0] = jnp.where(lane < ROPE_DIMS // 2, -sn, sn)


def _rope_tables(posf, freq):
    B = posf.shape[0]
    spec = pl.BlockSpec((1, SEQ, LANE), lambda b: (b, 0, 0))
    return pl.pallas_call(
        _rope_body,
        grid=(B,),
        in_specs=[pl.BlockSpec((1, SEQ, 1), lambda b: (b, 0, 0)), pl.BlockSpec((1, LANE), lambda b: (0, 0))],
        out_specs=[spec, spec],
        out_shape=[jax.ShapeDtypeStruct((B, SEQ, LANE), f32)] * 2,
        compiler_params=_cparams(("parallel",)),
        name="rope_tables",
    )(posf, freq)


def _rope_partner():
    half = ROPE_DIMS // 2
    perm = np.arange(LANE)
    perm[:half] += half
    perm[half:ROPE_DIMS] -= half
    return perm


ATTN_PC = 256
ATTN_G = 4


def _attn_body(q0, q1, q2, k0, k1, k2, v0, v1, v2, cos_ref, ss_ref, g_ref, p2_ref, o_ref,
               qd, kd, vd, tq, tk, tv, od, ld, on, ln, sc_a, sc_b):
    q_refs, k_refs, v_refs = (q0, q1, q2), (k0, k1, k2), (v0, v1, v2)
    BLK = ATTN_BLK
    qg = g_ref[0:1, :] * (DIL_DH ** -0.5)
    qgr = g_ref[1:2, :] * (DIL_DH ** -0.5)
    kg = g_ref[2:3, :]
    kgr = g_ref[3:4, :]
    kd[pl.ds(0, BLK), :] = jnp.zeros((BLK, LANE), bf16)
    vd[pl.ds(0, BLK), pl.ds(0, LANE)] = jnp.zeros((BLK, LANE), bf16)
    vd[:, pl.ds(LANE, LANE)] = jnp.ones((SEQ + BLK, LANE), bf16)

    qi = lax.broadcasted_iota(i32, (BLK, 2 * BLK), 0)
    kj = lax.broadcasted_iota(i32, (BLK, 2 * BLK), 1)
    is_cur = kj >= BLK
    band = jnp.where(is_cur, kj - BLK, qi) <= jnp.where(is_cur, qi, kj)

    def norm_rope(raw, partner, gain, gain_p, cs, ss):
        rf = raw.astype(f32)
        r = lax.rsqrt(jnp.mean(rf * rf, axis=-1, keepdims=True) + RMS_EPS)
        return (rf * (gain * cs) + partner * (gain_p * ss)) * r

    for p, (_, dil) in enumerate(DIL_PATTERNS):
        L = SEQ // dil
        nbl = L // BLK

        def prep(c, carry, p=p, dil=dil):
            rows = pl.ds(pl.multiple_of(c * ATTN_PC, ATTN_PC), ATTN_PC)
            cs, ss = cos_ref[0, rows, :], ss_ref[0, rows, :]
            qr = q_refs[p][0, rows, :]
            kr = k_refs[p][0, rows, :]
            partner = jnp.dot(jnp.concatenate([qr, kr], axis=1), p2_ref[...], preferred_element_type=f32)
            q = norm_rope(qr, partner[:, :LANE], qg, qgr, cs, ss)
            k = norm_rope(kr, partner[:, LANE:], kg, kgr, cs, ss)
            if dil == 1:
                orow = pl.ds(pl.multiple_of(c * ATTN_PC, ATTN_PC) + BLK, ATTN_PC)
                qd[rows, :] = q.astype(bf16)
                kd[orow, :] = k.astype(bf16)
                vd[orow, pl.ds(0, LANE)] = v_refs[p][0, rows, :]
            else:
                tq[rows, :] = q
                tk[rows, :] = k
                tv[rows, :] = v_refs[p][0, rows, :].astype(f32)
            return carry

        lax.fori_loop(0, SEQ // ATTN_PC, prep, 0, unroll=2)
        if dil > 1:
            for r in range(dil):
                src = pl.ds(r, L, stride=dil)
                qd[pl.ds(r * L, L), :] = tq[src, :].astype(bf16)
                kd[pl.ds(BLK + r * L, L), :] = tk[src, :].astype(bf16)
                vd[pl.ds(BLK + r * L, L), pl.ds(0, LANE)] = tv[src, :].astype(bf16)

        o_dst, l_dst = (on.at[p], ln.at[p]) if dil == 1 else (od, ld)

        def scores(g, dst):
            for j in range(ATTN_G):
                r0 = pl.multiple_of((g * ATTN_G + j) * BLK, BLK)
                dst[pl.ds(j * BLK, BLK), :] = lax.dot_general(qd[pl.ds(r0, BLK), :], kd[pl.ds(r0, 2 * BLK), :], NT,
                                                              preferred_element_type=f32)

        def finish(g, src, nbl=nbl, o_dst=o_dst, l_dst=l_dst):
            for j in range(ATTN_G):
                b = g * ATTN_G + j
                r0 = pl.multiple_of(b * BLK, BLK)
                first = (b % nbl) == 0
                valid = band & (is_cur | jnp.logical_not(first))
                s = jnp.where(valid, src[pl.ds(j * BLK, BLK), :], -jnp.inf)
                m = jnp.max(s, axis=-1, keepdims=True)
                e = jnp.exp(s - m).astype(bf16)
                ov = jnp.dot(e, vd[pl.ds(r0, 2 * BLK), :], preferred_element_type=f32)
                den = ov[:, LANE:]
                o_dst[pl.ds(r0, BLK), :] = ov[:, :LANE] / den
                l_dst[pl.ds(r0, BLK), :] = m + jnp.log(den)

        n_groups = SEQ // BLK // ATTN_G
        scores(0, sc_a)

        def pair(i, carry, scores=scores, finish=finish):
            scores(2 * i + 1, sc_b)
            finish(2 * i, sc_a)
            scores(jnp.minimum(2 * i + 2, n_groups - 1), sc_a)
            finish(2 * i + 1, sc_b)
            return carry

        lax.fori_loop(0, n_groups // 2, pair, 0)
        if dil > 1:
            for r in range(dil):
                dst = pl.ds(r, L, stride=dil)
                on[p, dst, :] = od[pl.ds(r * L, L), :]
                ln[p, dst, :] = ld[pl.ds(r * L, L), :]

    def combine(c, carry):
        rows = pl.ds(pl.multiple_of(c * ATTN_PC, ATTN_PC), ATTN_PC)
        l0, l1, l2 = ln[0, rows, :], ln[1, rows, :], ln[2, rows, :]
        mx = jnp.maximum(jnp.maximum(l0, l1), l2)
        w0, w1, w2 = jnp.exp(l0 - mx), jnp.exp(l1 - mx), jnp.exp(l2 - mx)
        o = (w0 * on[0, rows, :] + w1 * on[1, rows, :] + w2 * on[2, rows, :]) / (w0 + w1 + w2)
        o_ref[0, rows, :] = o.astype(bf16)
        return carry

    lax.fori_loop(0, SEQ // ATTN_PC, combine, 0)


def _attn(proj, cos, ss, qg, kg):
    B = proj.shape[0]
    base = DIL_OFF // LANE
    perm = _rope_partner()
    gains = jnp.stack([qg, qg[perm], kg, kg[perm]]).astype(f32)
    pm = np.zeros((LANE, LANE), np.float32)
    pm[perm[:ROPE_DIMS], np.arange(ROPE_DIMS)] = 1.0
    p2 = jnp.asarray(np.kron(np.eye(2, dtype=np.float32), pm), dtype=bf16)

    def pspec(qkv, p):
        return pl.BlockSpec((None, 1, SEQ, LANE), lambda b, h, o=base + (qkv * N_PAT + p) * DIL_HEADS: (b, o + h, 0, 0))

    tspec = pl.BlockSpec((1, SEQ, LANE), lambda b, h: (b, 0, 0))
    in_specs = [pspec(qkv, p) for qkv in range(3) for p in range(N_PAT)] + [
        tspec, tspec, pl.BlockSpec((4, LANE), lambda b, h: (0, 0)), pl.BlockSpec((2 * LANE, 2 * LANE), lambda b, h: (0, 0))]
    return pl.pallas_call(
        _attn_body,
        grid=(B, DIL_HEADS),
        in_specs=in_specs,
        out_specs=pl.BlockSpec((1, SEQ, LANE), lambda b, h: (b, 0, h)),
        out_shape=jax.ShapeDtypeStruct((B, SEQ, DIL_HEADS * DIL_DH), bf16),
        scratch_shapes=[
            pltpu.VMEM((SEQ, LANE), bf16),
            pltpu.VMEM((SEQ + ATTN_BLK, LANE), bf16),
            pltpu.VMEM((SEQ + ATTN_BLK, 2 * LANE), bf16),
            pltpu.VMEM((SEQ, LANE), f32),
            pltpu.VMEM((SEQ, LANE), f32),
            pltpu.VMEM((SEQ, LANE), f32),
            pltpu.VMEM((SEQ, LANE), f32),
            pltpu.VMEM((SEQ, LANE), f32),
            pltpu.VMEM((N_PAT, SEQ, LANE), f32),
            pltpu.VMEM((N_PAT, SEQ, LANE), f32),
            pltpu.VMEM((ATTN_G * ATTN_BLK, 2 * ATTN_BLK), f32),
            pltpu.VMEM((ATTN_G * ATTN_BLK, 2 * ATTN_BLK), f32),
        ],
        compiler_params=_cparams(("parallel", "arbitrary")),
        name="dilated_attn",
    )(*([proj] * 9), cos, ss, gains, p2)


def _merge_body(a_ref, b_ref, c_ref, g0_ref, g1_ref, g2_ref, x_ref, wb_ref, wo_ref, n2_ref, rw_ref, rb_ref, tri_ref,
                x1_ref, h2_ref, ei_ref, ew_ref, pos_ref, cnt_ref, cb_ref, cnt_scr):
    merged = None
    for n, (br, g) in enumerate(((a_ref, g0_ref), (b_ref, g1_ref), (c_ref, g2_ref))):
        logits = jnp.concatenate([g[t] for t in range(D_MODEL // LANE)], axis=1)
        gate = 0.5 * jnp.tanh(0.5 * logits.astype(f32)) + 0.5
        t = gate * jnp.dot(br[...], wb_ref[n], preferred_element_type=f32)
        merged = t if merged is None else merged + t
    x1 = x_ref[...] + jnp.dot(merged.astype(bf16), wo_ref[...], preferred_element_type=f32)
    x1_ref[...] = x1
    h2 = ((x1 * lax.rsqrt(jnp.mean(x1 * x1, axis=-1, keepdims=True) + RMS_EPS)) * n2_ref[...]).astype(bf16)
    h2_ref[...] = h2

    lg = lax.dot_general(rw_ref[...], h2, NT, preferred_element_type=f32) + rb_ref[...]
    ex = jnp.exp(lg - jnp.max(lg, axis=0, keepdims=True))
    sc = ex / jnp.sum(ex, axis=0, keepdims=True)
    srow = [sc[e:e + 1, :] for e in range(N_EXP)]
    best = bidx = None
    for gidx in range(N_GRP):
        s0, s1, s2, s3 = srow[4 * gidx:4 * gidx + 4]
        hi01, lo01, hi23, lo23 = jnp.maximum(s0, s1), jnp.minimum(s0, s1), jnp.maximum(s2, s3), jnp.minimum(s2, s3)
        gs = jnp.maximum(hi01, hi23) + jnp.maximum(jnp.minimum(hi01, hi23), jnp.maximum(lo01, lo23))
        if best is None:
            best, bidx = gs, jnp.zeros_like(gs, dtype=i32)
        else:
            better = gs > best
            best = jnp.where(better, gs, best)
            bidx = jnp.where(better, gidx, bidx)
    cand = []
    for i in range(EXP_PER_GRP):
        v = srow[i]
        for gidx in range(1, N_GRP):
            v = jnp.where(bidx == gidx, srow[4 * gidx + i], v)
        cand.append(v)

    def argmax4(vals):
        bw, bi = vals[0], jnp.zeros_like(bidx)
        for i in range(1, EXP_PER_GRP):
            better = vals[i] > bw
            bw = jnp.where(better, vals[i], bw)
            bi = jnp.where(better, i, bi)
        return bw, bi

    w1, i1 = argmax4(cand)
    w2, i2 = argmax4([jnp.where(i1 == i, -jnp.inf, cand[i]) for i in range(EXP_PER_GRP)])
    tot = w1 + w2
    e1 = bidx * EXP_PER_GRP + i1
    e2 = bidx * EXP_PER_GRP + i2
    ei_ref[0:1, :] = e1
    ei_ref[1:2, :] = e2
    ew_ref[0:1, :] = w1 / tot
    ew_ref[1:2, :] = w2 / tot

    @pl.when(pl.program_id(0) == 0)
    def _():
        cnt_scr[...] = jnp.zeros_like(cnt_scr)

    erow = lax.broadcasted_iota(i32, (N_EXP, MERGE_TM), 0)
    oh1 = erow == e1
    oh2 = erow == e2
    both = jnp.where(oh1 | oh2, 1.0, 0.0)
    incl = jnp.dot(both.astype(bf16), tri_ref[...], preferred_element_type=f32)
    cb_ref[0] = cnt_scr[...]
    base = cnt_scr[...] + (incl - both)
    pos_ref[0:1, :] = jnp.sum(jnp.where(oh1, base, 0.0), axis=0, keepdims=True).astype(i32)
    pos_ref[1:2, :] = jnp.sum(jnp.where(oh2, base, 0.0), axis=0, keepdims=True).astype(i32)
    cnt = cnt_scr[...] + incl[:, MERGE_TM - 1:MERGE_TM]
    cnt_scr[...] = cnt
    cnt_ref[...] = cnt


def _merge(bra, brb, brc, proj, x2d, wb, wo, n2, rw, rb):
    T = bra.shape[0]
    TM = MERGE_TM
    per_b = SEQ // TM
    gt = D_MODEL // LANE

    def gspec(n):
        return pl.BlockSpec((None, gt, TM, LANE), lambda i: (i // per_b, GATE_OFF // D_MODEL + n, i % per_b, 0))

    rspec = lambda w: pl.BlockSpec((TM, w), lambda i: (i, 0))
    full = lambda shape: pl.BlockSpec(shape, lambda i: (0,) * len(shape))
    tri = (jnp.arange(TM)[:, None] <= jnp.arange(TM)[None, :]).astype(bf16)
    return pl.pallas_call(
        _merge_body,
        grid=(T // TM,),
        in_specs=[
            rspec(BR_W), rspec(BR_W), rspec(BR_W),
            gspec(0), gspec(1), gspec(2),
            rspec(D_MODEL),
            full((N_BR, BR_W, D_MODEL)), full((D_MODEL, D_MODEL)), full((1, D_MODEL)),
            full((N_EXP, D_MODEL)), full((N_EXP, 1)), full((TM, TM)),
        ],
        out_specs=[
            rspec(D_MODEL), rspec(D_MODEL),
            pl.BlockSpec((TOP_K, TM), lambda i: (0, i)),
            pl.BlockSpec((TOP_K, TM), lambda i: (0, i)),
            pl.BlockSpec((TOP_K, TM), lambda i: (0, i)),
            full((N_EXP, 1)),
            pl.BlockSpec((1, N_EXP, 1), lambda i: (i, 0, 0)),
        ],
        out_shape=[
            jax.ShapeDtypeStruct((T, D_MODEL), f32),
            jax.ShapeDtypeStruct((T, D_MODEL), bf16),
            jax.ShapeDtypeStruct((TOP_K, T), i32),
            jax.ShapeDtypeStruct((TOP_K, T), f32),
            jax.ShapeDtypeStruct((TOP_K, T), i32),
            jax.ShapeDtypeStruct((N_EXP, 1), f32),
            jax.ShapeDtypeStruct((T // TM, N_EXP, 1), f32),
        ],
        scratch_shapes=[pltpu.VMEM((N_EXP, 1), f32)],
        compiler_params=_cparams(("arbitrary",)),
        name="merge_router",
    )(bra, brb, brc, proj, proj, proj, x2d, wb, wo, n2, rw, rb, tri)


def _expert_body(be_ref, nu_ref, xs_ref, wgu_ref, wd_ref, ys_ref):
    i = pl.program_id(0)

    @pl.when(i < nu_ref[0])
    def _():
        gu = jnp.dot(xs_ref[...], wgu_ref[0], preferred_element_type=f32)
        g = gu[:, :EXP_FF]
        act = (g * jax.nn.sigmoid(g)) * gu[:, EXP_FF:]
        ys_ref[...] = jnp.dot(act.astype(bf16), wd_ref[0], preferred_element_type=f32).astype(bf16)

    @pl.when(i >= nu_ref[0])
    def _():
        ys_ref[...] = jnp.zeros_like(ys_ref)


def _experts(block_e, n_used, xs, wgu, wd):
    n_rows = xs.shape[0]
    grid_spec = pltpu.PrefetchScalarGridSpec(
        num_scalar_prefetch=2,
        grid=(n_rows // MOE_BM,),
        in_specs=[
            pl.BlockSpec((MOE_BM, D_MODEL), lambda i, be, nu: (jnp.minimum(i, nu[0] - 1), 0)),
            pl.BlockSpec((1, D_MODEL, 2 * EXP_FF), lambda i, be, nu: (be[i], 0, 0)),
            pl.BlockSpec((1, EXP_FF, D_MODEL), lambda i, be, nu: (be[i], 0, 0)),
        ],
        out_specs=pl.BlockSpec((MOE_BM, D_MODEL), lambda i, be, nu: (i, 0)),
    )
    return pl.pallas_call(
        _expert_body,
        grid_spec=grid_spec,
        out_shape=jax.ShapeDtypeStruct((n_rows, D_MODEL), bf16),
        compiler_params=_cparams(("arbitrary",)),
        name="experts",
    )(block_e, n_used, xs, wgu, wd)


COMB_TM = 1024


def _combine_body(x1_ref, y0_ref, y1_ref, w_ref, o_ref):
    w = w_ref[...]
    o_ref[...] = x1_ref[...] + y0_ref[...].astype(f32) * w[:, 0:1] + y1_ref[...].astype(f32) * w[:, 1:2]


def _combine(x1, y0, y1, w):
    T = x1.shape[0]
    rspec = pl.BlockSpec((COMB_TM, D_MODEL), lambda i: (i, 0))
    return pl.pallas_call(
        _combine_body,
        grid=(T // COMB_TM,),
        in_specs=[rspec, rspec, rspec, pl.BlockSpec((COMB_TM, TOP_K), lambda i: (i, 0))],
        out_specs=rspec,
        out_shape=jax.ShapeDtypeStruct((T, D_MODEL), f32),
        compiler_params=_cparams(("parallel",)),
        name="moe_combine",
    )(x1, y0, y1, w)


DISP_ALIGN = 16
DISP_SLOTS = MERGE_TM * TOP_K + N_EXP * DISP_ALIGN
DISP_SIZES = tuple(DISP_ALIGN << s for s in range(5, -1, -1))


def _pieces(length, sizes, fn):
    done = jnp.int32(0)
    for sz in sizes:
        take = (length & sz) != 0

        @pl.when(take)
        def _(done=done, sz=sz):
            fn(done, sz)

        done = done + jnp.where(take, sz, 0)


def _dispatch_body(off_ref, n_ref, ls_ref, toff_ref, tlen_ref, h2_ref, ei_ref, pos_ref, adj_ref, xs_hbm,
                   srt, zbuf, sem, zsem):
    i = pl.program_id(0)
    last = pl.num_programs(0) - 1
    cur = i % 2
    TM = MERGE_TM

    def slab_copies(step, buf, start):
        for e in range(N_EXP):
            ls = ls_ref[step * N_EXP + e]
            off = off_ref[step * N_EXP + e]

            def piece(done, sz, ls=ls, off=off):
                cp = pltpu.make_async_copy(srt.at[buf, pl.ds(pl.multiple_of(ls + done, DISP_ALIGN), sz)],
                                           xs_hbm.at[pl.ds(pl.multiple_of(off + done, DISP_ALIGN), sz)], sem.at[buf])
                cp.start() if start else cp.wait()

            _pieces(n_ref[step * N_EXP + e], DISP_SIZES, piece)

    def tail_copies(start):
        for e in range(N_EXP):
            off = toff_ref[e]

            def piece(done, sz, off=off):
                cp = pltpu.make_async_copy(zbuf.at[pl.ds(0, sz)],
                                           xs_hbm.at[pl.ds(pl.multiple_of(off + done, DISP_ALIGN), sz)], zsem)
                cp.start() if start else cp.wait()

            _pieces(tlen_ref[e], DISP_SIZES[1:], piece)

        zrows = zbuf.shape[0]

        def block(c, carry):
            cp = pltpu.make_async_copy(zbuf, xs_hbm.at[pl.ds(pl.multiple_of(toff_ref[N_EXP] + c * zrows, zrows), zrows)],
                                       zsem)
            cp.start() if start else cp.wait()
            return carry

        lax.fori_loop(0, tlen_ref[N_EXP] // zrows, block, 0)

    @pl.when(i == 0)
    def _():
        zbuf[...] = jnp.zeros_like(zbuf)
        tail_copies(True)

    @pl.when(i >= 2)
    def _():
        slab_copies(i - 2, cur, False)

    erow = lax.broadcasted_iota(i32, (N_EXP, TM), 0)
    adj = adj_ref[0]

    def slot(k):
        sel = jnp.sum(jnp.where(erow == ei_ref[k:k + 1, :], adj, 0.0), axis=0, keepdims=True)
        return pos_ref[k:k + 1, :] + sel.astype(i32)

    srow = lax.broadcasted_iota(i32, (DISP_SLOTS, TM), 0)
    perm = jnp.where((srow == slot(0)) | (srow == slot(1)), 1.0, 0.0).astype(bf16)
    srt[cur] = jnp.dot(perm, h2_ref[...], preferred_element_type=f32).astype(bf16)
    slab_copies(i, cur, True)

    @pl.when(i == last)
    def _():
        slab_copies(i, cur, False)

        @pl.when(i >= 1)
        def _():
            slab_copies(i - 1, 1 - cur, False)

        tail_copies(False)


def _dispatch(off, n16, lstart, tail_off, tail_len, h2, eidx, pos, adj, n_rows):
    T = h2.shape[0]
    TM = MERGE_TM
    grid_spec = pltpu.PrefetchScalarGridSpec(
        num_scalar_prefetch=5,
        grid=(T // TM,),
        in_specs=[
            pl.BlockSpec((TM, D_MODEL), lambda i, *_: (i, 0)),
            pl.BlockSpec((TOP_K, TM), lambda i, *_: (0, i)),
            pl.BlockSpec((TOP_K, TM), lambda i, *_: (0, i)),
            pl.BlockSpec((1, N_EXP, 1), lambda i, *_: (i, 0, 0)),
        ],
        out_specs=pl.BlockSpec(memory_space=pl.ANY),
        scratch_shapes=[
            pltpu.VMEM((2, DISP_SLOTS, D_MODEL), bf16),
            pltpu.VMEM((DISP_SIZES[1], D_MODEL), bf16),
            pltpu.SemaphoreType.DMA((2,)),
            pltpu.SemaphoreType.DMA(()),
        ],
    )
    return pl.pallas_call(
        _dispatch_body,
        grid_spec=grid_spec,
        out_shape=jax.ShapeDtypeStruct((n_rows, D_MODEL), bf16),
        compiler_params=_cparams(("arbitrary",)),
        name="moe_dispatch",
    )(off, n16, lstart, tail_off, tail_len, h2, eidx, pos, adj)


def _moe(x1, h2, eidx, ew, pos, cnt, cb, wgu, wd):
    T = x1.shape[0]
    TM = MERGE_TM
    n_tiles = T // TM
    ceil_to = lambda v, m: (v + m - 1) // m * m
    n_rows = ceil_to(T * TOP_K + n_tiles * N_EXP * (DISP_ALIGN - 1) + N_EXP * (MOE_BM - 1), MOE_BM)
    n_blocks = n_rows // MOE_BM
    cbi = cb[:, :, 0].astype(i32)
    tile_cnt = jnp.concatenate([cbi[1:], cnt[:, 0].astype(i32)[None, :]], axis=0) - cbi
    n16 = ceil_to(tile_cnt, DISP_ALIGN)
    lstart = jnp.cumsum(n16, axis=1) - n16
    before = jnp.cumsum(n16, axis=0) - n16
    used = jnp.sum(n16, axis=0)
    padded = ceil_to(used, MOE_BM)
    pad_ends = jnp.cumsum(padded)
    off = (pad_ends - padded)[None, :] + before
    adj = (lstart - cbi).astype(f32)[:, :, None]
    row_adj = jnp.repeat(off - cbi, TM, axis=0).T
    dest = pos
    for e in range(N_EXP):
        dest = dest + jnp.where(eidx == e, row_adj[e][None, :], 0)
    blk_start = jnp.arange(n_blocks, dtype=i32) * MOE_BM
    block_e = jnp.minimum(jnp.sum((blk_start[:, None] >= pad_ends[None, :]).astype(i32), axis=1), N_EXP - 1)
    n_used = (pad_ends[-1:] // MOE_BM).astype(i32)
    tail_off = jnp.concatenate([pad_ends - padded + used, pad_ends[-1:]])
    tail_len = jnp.concatenate([padded - used, n_rows - pad_ends[-1:]])
    xs = _dispatch(off.reshape(-1), n16.reshape(-1), lstart.reshape(-1), tail_off, tail_len, h2, eidx, pos, adj, n_rows)
    ys = _experts(block_e, n_used, xs, wgu, wd)
    return _combine(x1, ys[dest[0]], ys[dest[1]], ew.T)


def _pack_w_in(w):
    cuts = np.cumsum([256, 256, 512, 512, 16, 512, 512, 4608, 3072])[:-1].tolist()
    q, k, v, og, glow, xb, yb, qkv, gates = jnp.split(w, cuts, axis=-1)
    packed = jnp.concatenate([gates, qkv, xb, yb, q, k, v, og], axis=-1).astype(bf16)
    wg = jnp.pad(glow, ((0, 0), (0, LANE - GLA_RANK))).astype(bf16)
    return packed, wg


def _block_diag_lru(w_a, w_x):
    bs = LRU_W // LRU_BLOCKS
    per = LANE // bs
    tiles = []
    for t in range(LRU_W // LANE):
        halves = []
        for w in (w_a, w_x):
            m = jnp.zeros((LANE, LANE), f32)
            for j in range(per):
                m = m.at[j * bs:(j + 1) * bs, j * bs:(j + 1) * bs].set(w[t * per + j])
            halves.append(m)
        tiles.append(jnp.concatenate(halves, axis=1))
    return jnp.stack(tiles).astype(bf16)


def kernel(x, positions, norm1_gain, w_in, gla_gate_up, gla_gate_bias, gla_out_gain, lru_conv_w, lru_conv_b, lru_w_a, lru_b_a, lru_w_x, lru_b_x, lru_lambda, q_norm_gain, k_norm_gain, w_branch, w_out, norm2_gain, router_w, router_b, w_gate, w_up, w_down):
    B, S, D = x.shape
    assert S == SEQ and D == D_MODEL
    T = B * S
    depth = w_in.shape[0]

    half = ROPE_DIMS // 2
    inv_freq = ROPE_THETA ** (-jnp.arange(half, dtype=f32) / half)
    freq = jnp.zeros((1, LANE), f32).at[0, :ROPE_DIMS].set(jnp.concatenate([inv_freq, inv_freq]))
    cos, ss = _rope_tables(positions.astype(f32)[:, :, None], freq)

    rw = router_w.T.astype(bf16)
    rb = router_b.astype(f32)[:, None]
    for l in range(depth):
        w_packed, wg = _pack_w_in(w_in[l])
        wup = jnp.pad(gla_gate_up[l], ((0, LANE - GLA_RANK), (0, 0)))
        nt = LRU_W // LANE
        lru_w = _block_diag_lru(lru_w_a[l], lru_w_x[l])
        lru_b = jnp.concatenate([lru_b_a[l].reshape(nt, 1, LANE), lru_b_x[l].reshape(nt, 1, LANE)], axis=-1)
        wgu = jnp.concatenate([w_gate[l], w_up[l]], axis=-1).astype(bf16)

        proj, glow = _inproj(x, norm1_gain[l][None, :], w_packed, wg)
        br_a = _gla(proj, glow, wup, gla_gate_bias[l][None, :], gla_out_gain[l][None, :])
        br_b = _lru(proj, lru_conv_w[l], lru_conv_b[l][None, :], lru_w, lru_b, lru_lambda[l][None, :])
        br_c = _attn(proj, cos, ss, q_norm_gain[l], k_norm_gain[l])
        routed = _merge(br_a.reshape(T, BR_W), br_b.reshape(T, BR_W), br_c.reshape(T, BR_W), proj, x.reshape(T, D),
                        w_branch[l].astype(bf16), w_out[l].astype(bf16), norm2_gain[l][None, :], rw, rb)
        x = _moe(*routed, wgu, w_down[l].astype(bf16)).reshape(B, S, D)
    return x
```

```python
import jax
import jax.numpy as jnp
import numpy as np
from jax import lax
from jax.experimental import pallas as pl
from jax.experimental.pallas import tpu as pltpu

f32 = jnp.float32
bf16 = jnp.bfloat16
i32 = jnp.int32

D_MODEL = 1024
SEQ = 2048
RMS_EPS = 1e-6
GLA_HEADS, GLA_DK, GLA_DV, GLA_RANK, GLA_NORM, GLA_CHUNK = 4, 64, 128, 16, 16.0, 64
LRU_W, LRU_BLOCKS, LRU_CONV, LRU_C = 512, 8, 4, 8.0
DIL_PATTERNS = ((128, 1), (512, 4), (2048, 16))
N_PAT, DIL_HEADS, DIL_DH, ATTN_BLK = 3, 4, 128, 128
ROPE_THETA, ROPE_DIMS = 500000.0, 32
N_BR, BR_W = 3, 512
N_EXP, N_GRP, EXP_PER_GRP, TOP_K, EXP_FF = 16, 4, 4, 2, 512

LANE = 128
GATE_OFF, DIL_OFF, XB_OFF, YB_OFF, Q_OFF, K_OFF, V_OFF, OG_OFF = 0, 3072, 7680, 8192, 8704, 8960, 9216, 9728
PROJ_COLS = 10240
PROJ_TN = 2048
PROJ_RC = 512
MERGE_TM = 512
MOE_BM = 1024
VMEM_LIMIT = 56 * 1024 * 1024

NT = (((1,), (1,)), ((), ()))


def _cparams(sem):
    return pltpu.CompilerParams(dimension_semantics=sem, vmem_limit_bytes=VMEM_LIMIT)


def _inproj_body(x_ref, g_ref, w_ref, wg_ref, proj_ref, glow_ref, h_scr):
    j = pl.program_id(1)
    nrc = SEQ // PROJ_RC

    @pl.when(j == 0)
    def _():
        for c in range(nrc):
            rows = pl.ds(c * PROJ_RC, PROJ_RC)
            x = x_ref[0, rows, :]
            ms = jnp.mean(x * x, axis=-1, keepdims=True)
            h = ((x * lax.rsqrt(ms + RMS_EPS)) * g_ref[...]).astype(bf16)
            h_scr[rows, :] = h
            glow_ref[0, rows, :] = jnp.dot(h, wg_ref[...], preferred_element_type=f32)

    for c in range(nrc):
        rows = pl.ds(c * PROJ_RC, PROJ_RC)
        res = jnp.dot(h_scr[rows, :], w_ref[...], preferred_element_type=f32).astype(bf16)
        for t in range(PROJ_TN // LANE):
            proj_ref[0, t, rows, :] = res[:, t * LANE:(t + 1) * LANE]


def _inproj(x, gain, w, wg):
    B = x.shape[0]
    nj = PROJ_COLS // PROJ_TN
    return pl.pallas_call(
        _inproj_body,
        grid=(B, nj),
        in_specs=[
            pl.BlockSpec((1, SEQ, D_MODEL), lambda b, j: (b, 0, 0)),
            pl.BlockSpec((1, D_MODEL), lambda b, j: (0, 0)),
            pl.BlockSpec((D_MODEL, PROJ_TN), lambda b, j: (0, j)),
            pl.BlockSpec((D_MODEL, LANE), lambda b, j: (0, 0)),
        ],
        out_specs=[
            pl.BlockSpec((1, PROJ_TN // LANE, SEQ, LANE), lambda b, j: (b, j, 0, 0)),
            pl.BlockSpec((1, SEQ, LANE), lambda b, j: (b, 0, 0)),
        ],
        out_shape=[
            jax.ShapeDtypeStruct((B, PROJ_COLS // LANE, SEQ, LANE), bf16),
            jax.ShapeDtypeStruct((B, SEQ, LANE), f32),
        ],
        scratch_shapes=[pltpu.VMEM((SEQ, D_MODEL), bf16)],
        compiler_params=_cparams(("parallel", "arbitrary")),
        name="inproj",
    )(x, gain, w, wg)


def _gla_body(q_ref, k_ref, v_ref, og_ref, glow_ref, wh_ref, wl_ref, bg_ref, gain_ref, o_ref, la_scr, st_scr, stb_scr):
    C = GLA_CHUNK
    HK = GLA_HEADS * GLA_DK
    PR = 512
    for c in range(SEQ // PR):
        rows = pl.ds(c * PR, PR)
        g = glow_ref[0, rows, :]
        gh = g.astype(bf16)
        gl = (g - gh.astype(f32)).astype(bf16)
        z = (jnp.dot(gh, wh_ref[...], preferred_element_type=f32)
             + jnp.dot(gl, wh_ref[...], preferred_element_type=f32)
             + jnp.dot(gh, wl_ref[...], preferred_element_type=f32)) + bg_ref[...]
        la_scr[rows, :] = (jnp.minimum(z, 0.0) - jnp.log1p(jnp.exp(-jnp.abs(z)))) * (1.0 / GLA_NORM)
    st_scr[...] = jnp.zeros_like(st_scr)
    stb_scr[...] = jnp.zeros_like(stb_scr)

    causal = (lax.broadcasted_iota(i32, (C, GLA_HEADS * C), 0)
              >= lax.broadcasted_iota(i32, (C, GLA_HEADS * C), 1) % C)
    tri = (lax.broadcasted_iota(i32, (C, C), 0) >= lax.broadcasted_iota(i32, (C, C), 1)).astype(bf16)
    lane_head = lax.broadcasted_iota(i32, (1, HK), 1) // GLA_DK
    gain = gain_ref[...]
    zero_v = jnp.zeros((C, GLA_DV), bf16)

    def chunk(n, carry):
        rows = pl.ds(pl.multiple_of(n * C, C), C)
        la = la_scr[rows, :]
        p1 = la.astype(bf16)
        r1 = la - p1.astype(f32)
        p2 = r1.astype(bf16)
        p3 = (r1 - p2.astype(f32)).astype(bf16)
        c3 = jnp.dot(tri, jnp.concatenate([p1, p2, p3], axis=1), preferred_element_type=f32)
        cum = (c3[:, :HK] + c3[:, HK:2 * HK]) + c3[:, 2 * HK:]
        cum_t = cum.T
        tot_t = cum_t[:, C - 1:C]
        qf = jnp.concatenate([q_ref[0, t, rows, :] for t in range(HK // LANE)], axis=1).astype(f32)
        kf = jnp.concatenate([k_ref[0, t, rows, :] for t in range(HK // LANE)], axis=1).astype(f32)
        qd = (qf * (GLA_DK ** -0.5) * jnp.exp(cum)).astype(bf16)
        ki = kf * jnp.exp(-cum)
        kd_t = (kf.T * jnp.exp(tot_t - cum_t)).astype(bf16)
        dec_t = jnp.exp(tot_t)
        heads = range(GLA_HEADS)
        vcols = [v_ref[0, h, rows, :] for h in heads]
        k_st = jnp.concatenate([jnp.where(lane_head == h, ki, 0.0).astype(bf16) for h in heads], axis=0)
        v_bd = jnp.concatenate([jnp.concatenate([vcols[h] if g == h else zero_v for g in heads], axis=1)
                                for h in heads], axis=0)
        s = lax.dot_general(qd, k_st, NT, preferred_element_type=f32)
        s = jnp.where(causal, s, 0.0).astype(bf16)
        o = jnp.dot(s, v_bd, preferred_element_type=f32) + jnp.dot(qd, stb_scr[...], preferred_element_type=f32)
        for h in heads:
            cols = slice(h * GLA_DV, (h + 1) * GLA_DV)
            hr = slice(h * GLA_DK, (h + 1) * GLA_DK)
            st = st_scr[h] * dec_t[hr, :] + jnp.dot(kd_t[hr, :], vcols[h], preferred_element_type=f32)
            st_scr[h] = st
            stb_scr[hr, cols] = st.astype(bf16)
            oh = o[:, cols]
            oh = oh * lax.rsqrt(jnp.mean(oh * oh, axis=-1, keepdims=True) + RMS_EPS) * gain
            g = og_ref[0, h, rows, :].astype(f32)
            o_ref[0, rows, cols] = (oh * (g * jax.nn.sigmoid(g))).astype(bf16)
        return carry

    lax.fori_loop(0, SEQ // C, chunk, 0, unroll=4)


def _gla(proj, glow, wup, bg, gain):
    B = proj.shape[0]
    HK = GLA_HEADS * GLA_DK
    HV = GLA_HEADS * GLA_DV
    wup_hi = wup.astype(bf16)
    wup_lo = (wup - wup_hi.astype(f32)).astype(bf16)
    return pl.pallas_call(
        _gla_body,
        grid=(B,),
        in_specs=[
            pl.BlockSpec((1, HK // LANE, SEQ, LANE), lambda b: (b, Q_OFF // HK, 0, 0)),
            pl.BlockSpec((1, HK // LANE, SEQ, LANE), lambda b: (b, K_OFF // HK, 0, 0)),
            pl.BlockSpec((1, HV // LANE, SEQ, LANE), lambda b: (b, V_OFF // HV, 0, 0)),
            pl.BlockSpec((1, HV // LANE, SEQ, LANE), lambda b: (b, OG_OFF // HV, 0, 0)),
            pl.BlockSpec((1, SEQ, LANE), lambda b: (b, 0, 0)),
            pl.BlockSpec((LANE, HK), lambda b: (0, 0)),
            pl.BlockSpec((LANE, HK), lambda b: (0, 0)),
            pl.BlockSpec((1, HK), lambda b: (0, 0)),
            pl.BlockSpec((1, GLA_DV), lambda b: (0, 0)),
        ],
        out_specs=pl.BlockSpec((1, SEQ, HV), lambda b: (b, 0, 0)),
        out_shape=jax.ShapeDtypeStruct((B, SEQ, HV), bf16),
        scratch_shapes=[
            pltpu.VMEM((SEQ, HK), f32),
            pltpu.VMEM((GLA_HEADS, GLA_DK, GLA_DV), f32),
            pltpu.VMEM((HK, HV), bf16),
        ],
        compiler_params=_cparams(("parallel",)),
        name="gla",
    )(proj, proj, proj, proj, glow, wup_hi, wup_lo, bg, gain)


LRU_SEG = 8
LRU_SEGLEN = SEQ // LRU_SEG
LRU_PAD = 8


def _lru_body(xb_ref, yb_ref, cw_ref, cb_ref, w_ref, b_ref, lam_ref, o_ref, a_scr, u_scr, hs_scr, as_scr):
    as_scr[pl.ds(0, LRU_PAD), :] = jnp.zeros((LRU_PAD, LANE), f32)
    as_scr[pl.ds(LRU_PAD, SEQ), :] = xb_ref[0].astype(f32)
    xc = cb_ref[...]
    for j in range(LRU_CONV):
        xc = xc + as_scr[pl.ds(LRU_PAD - j, SEQ), :] * cw_ref[LRU_CONV - 1 - j:LRU_CONV - j, :]
    z = jnp.dot(xc.astype(bf16), w_ref[0], preferred_element_type=f32) + b_ref[0]
    r = jax.nn.sigmoid(z[:, :LANE])
    gi = jax.nn.sigmoid(z[:, LANE:])
    nl = -lam_ref[...]
    sp = jnp.maximum(nl, 0.0) + jnp.log1p(jnp.exp(-jnp.abs(nl)))
    log_a = (-LRU_C) * r * sp
    a = jnp.exp(log_a)
    u = jnp.sqrt(jnp.tanh(-log_a) * (a * a + 1.0)) * gi * xc
    a_scr[...] = a
    u_scr[...] = u

    def step(t, carry):
        h, acc = carry
        idx = pl.ds(t, LRU_SEG, stride=LRU_SEGLEN)
        at = a_scr[idx, :]
        h = at * h + u_scr[idx, :]
        acc = at * acc
        hs_scr[idx, :] = h
        as_scr[idx, :] = acc
        return h, acc

    hfin, afin = lax.fori_loop(0, LRU_SEGLEN, step, (jnp.zeros((LRU_SEG, LANE), f32), jnp.ones((LRU_SEG, LANE), f32)),
                               unroll=8)
    cin = jnp.zeros((1, LANE), f32)
    for s in range(LRU_SEG):
        rows = pl.ds(s * LRU_SEGLEN, LRU_SEGLEN)
        h = hs_scr[rows, :] + as_scr[rows, :] * cin
        o_ref[0, rows, :] = (h * jax.nn.gelu(yb_ref[0, rows, :].astype(f32))).astype(bf16)
        cin = hfin[s:s + 1, :] + afin[s:s + 1, :] * cin


def _lru(proj, cw, cb, w, b, lam):
    B = proj.shape[0]
    nt = LRU_W // LANE
    return pl.pallas_call(
        _lru_body,
        grid=(B, nt),
        in_specs=[
            pl.BlockSpec((None, 1, SEQ, LANE), lambda b, t: (b, XB_OFF // LANE + t, 0, 0)),
            pl.BlockSpec((None, 1, SEQ, LANE), lambda b, t: (b, YB_OFF // LANE + t, 0, 0)),
            pl.BlockSpec((LRU_CONV, LANE), lambda b, t: (0, t)),
            pl.BlockSpec((1, LANE), lambda b, t: (0, t)),
            pl.BlockSpec((1, LANE, 2 * LANE), lambda b, t: (t, 0, 0)),
            pl.BlockSpec((1, 1, 2 * LANE), lambda b, t: (t, 0, 0)),
            pl.BlockSpec((1, LANE), lambda b, t: (0, t)),
        ],
        out_specs=pl.BlockSpec((1, SEQ, LANE), lambda b, t: (b, 0, t)),
        out_shape=jax.ShapeDtypeStruct((B, SEQ, LRU_W), bf16),
        scratch_shapes=[pltpu.VMEM((SEQ, LANE), f32)] * 3 + [pltpu.VMEM((SEQ + LRU_PAD, LANE), f32)],
        compiler_params=_cparams(("parallel", "parallel")),
        name="lru",
    )(proj, proj, cw, cb, w, b, lam)


def _rope_body(pos_ref, freq_ref, cos_ref, ss_ref):
    ang = pos_ref[0] * freq_ref[...]
    lane = lax.broadcasted_iota(i32, (SEQ, LANE), 1)
    sn = jnp.sin(ang)
    cos_ref[0] = jnp.cos(ang)
    ss_ref[0] = jnp.where(lane < ROPE_DIMS // 2, -sn, sn)


def _rope_tables(posf, freq):
    B = posf.shape[0]
    spec = pl.BlockSpec((1, SEQ, LANE), lambda b: (b, 0, 0))
    return pl.pallas_call(
        _rope_body,
        grid=(B,),
        in_specs=[pl.BlockSpec((1, SEQ, 1), lambda b: (b, 0, 0)), pl.BlockSpec((1, LANE), lambda b: (0, 0))],
        out_specs=[spec, spec],
        out_shape=[jax.ShapeDtypeStruct((B, SEQ, LANE), f32)] * 2,
        compiler_params=_cparams(("parallel",)),
        name="rope_tables",
    )(posf, freq)


def _rope_partner():
    half = ROPE_DIMS // 2
    perm = np.arange(LANE)
    perm[:half] += half
    perm[half:ROPE_DIMS] -= half
    return perm


ATTN_PC = 256
ATTN_G = 4


def _attn_body(q0, q1, q2, k0, k1, k2, v0, v1, v2, cos_ref, ss_ref, g_ref, p2_ref, o_ref,
               qd, kd, vd, tq, tk, tv, od, ld, on, ln, sc_a, sc_b):
    q_refs, k_refs, v_refs = (q0, q1, q2), (k0, k1, k2), (v0, v1, v2)
    BLK = ATTN_BLK
    qg = g_ref[0:1, :] * (DIL_DH ** -0.5)
    qgr = g_ref[1:2, :] * (DIL_DH ** -0.5)
    kg = g_ref[2:3, :]
    kgr = g_ref[3:4, :]
    kd[pl.ds(0, BLK), :] = jnp.zeros((BLK, LANE), bf16)
    vd[pl.ds(0, BLK), pl.ds(0, LANE)] = jnp.zeros((BLK, LANE), bf16)
    vd[:, pl.ds(LANE, LANE)] = jnp.ones((SEQ + BLK, LANE), bf16)

    qi = lax.broadcasted_iota(i32, (BLK, 2 * BLK), 0)
    kj = lax.broadcasted_iota(i32, (BLK, 2 * BLK), 1)
    is_cur = kj >= BLK
    band = jnp.where(is_cur, kj - BLK, qi) <= jnp.where(is_cur, qi, kj)

    def norm_rope(raw, partner, gain, gain_p, cs, ss):
        rf = raw.astype(f32)
        r = lax.rsqrt(jnp.mean(rf * rf, axis=-1, keepdims=True) + RMS_EPS)
        return (rf * (gain * cs) + partner * (gain_p * ss)) * r

    for p, (_, dil) in enumerate(DIL_PATTERNS):
        L = SEQ // dil
        nbl = L // BLK

        def prep(c, carry, p=p, dil=dil):
            rows = pl.ds(pl.multiple_of(c * ATTN_PC, ATTN_PC), ATTN_PC)
            cs, ss = cos_ref[0, rows, :], ss_ref[0, rows, :]
            qr = q_refs[p][0, rows, :]
            kr = k_refs[p][0, rows, :]
            partner = jnp.dot(jnp.concatenate([qr, kr], axis=1), p2_ref[...], preferred_element_type=f32)
            q = norm_rope(qr, partner[:, :LANE], qg, qgr, cs, ss)
            k = norm_rope(kr, partner[:, LANE:], kg, kgr, cs, ss)
            if dil == 1:
                orow = pl.ds(pl.multiple_of(c * ATTN_PC, ATTN_PC) + BLK, ATTN_PC)
                qd[rows, :] = q.astype(bf16)
                kd[orow, :] = k.astype(bf16)
                vd[orow, pl.ds(0, LANE)] = v_refs[p][0, rows, :]
            else:
                tq[rows, :] = q
                tk[rows, :] = k
                tv[rows, :] = v_refs[p][0, rows, :].astype(f32)
            return carry

        lax.fori_loop(0, SEQ // ATTN_PC, prep, 0, unroll=2)
        if dil > 1:
            for r in range(dil):
                src = pl.ds(r, L, stride=dil)
                qd[pl.ds(r * L, L), :] = tq[src, :].astype(bf16)
                kd[pl.ds(BLK + r * L, L), :] = tk[src, :].astype(bf16)
                vd[pl.ds(BLK + r * L, L), pl.ds(0, LANE)] = tv[src, :].astype(bf16)

        o_dst, l_dst = (on.at[p], ln.at[p]) if dil == 1 else (od, ld)

        def scores(g, dst):
            for j in range(ATTN_G):
                r0 = pl.multiple_of((g * ATTN_G + j) * BLK, BLK)
                dst[pl.ds(j * BLK, BLK), :] = lax.dot_general(qd[pl.ds(r0, BLK), :], kd[pl.ds(r0, 2 * BLK), :], NT,
                                                              preferred_element_type=f32)

        def finish(g, src, nbl=nbl, o_dst=o_dst, l_dst=l_dst):
            for j in range(ATTN_G):
                b = g * ATTN_G + j
                r0 = pl.multiple_of(b * BLK, BLK)
                first = (b % nbl) == 0
                valid = band & (is_cur | jnp.logical_not(first))
                s = jnp.where(valid, src[pl.ds(j * BLK, BLK), :], -jnp.inf)
                m = jnp.max(s, axis=-1, keepdims=True)
                e = jnp.exp(s - m).astype(bf16)
                ov = jnp.dot(e, vd[pl.ds(r0, 2 * BLK), :], preferred_element_type=f32)
                den = ov[:, LANE:]
                o_dst[pl.ds(r0, BLK), :] = ov[:, :LANE] / den
                l_dst[pl.ds(r0, BLK), :] = m + jnp.log(den)

        n_groups = SEQ // BLK // ATTN_G
        scores(0, sc_a)

        def pair(i, carry, scores=scores, finish=finish):
            scores(2 * i + 1, sc_b)
            finish(2 * i, sc_a)
            scores(jnp.minimum(2 * i + 2, n_groups - 1), sc_a)
            finish(2 * i + 1, sc_b)
            return carry

        lax.fori_loop(0, n_groups // 2, pair, 0)
        if dil > 1:
            for r in range(dil):
                dst = pl.ds(r, L, stride=dil)
                on[p, dst, :] = od[pl.ds(r * L, L), :]
                ln[p, dst, :] = ld[pl.ds(r * L, L), :]

    def combine(c, carry):
        rows = pl.ds(pl.multiple_of(c * ATTN_PC, ATTN_PC), ATTN_PC)
        l0, l1, l2 = ln[0, rows, :], ln[1, rows, :], ln[2, rows, :]
        mx = jnp.maximum(jnp.maximum(l0, l1), l2)
        w0, w1, w2 = jnp.exp(l0 - mx), jnp.exp(l1 - mx), jnp.exp(l2 - mx)
        o = (w0 * on[0, rows, :] + w1 * on[1, rows, :] + w2 * on[2, rows, :]) / (w0 + w1 + w2)
        o_ref[0, rows, :] = o.astype(bf16)
        return carry

    lax.fori_loop(0, SEQ // ATTN_PC, combine, 0)


def _attn(proj, cos, ss, qg, kg):
    B = proj.shape[0]
    base = DIL_OFF // LANE
    perm = _rope_partner()
    gains = jnp.stack([qg, qg[perm], kg, kg[perm]]).astype(f32)
    pm = np.zeros((LANE, LANE), np.float32)
    pm[perm[:ROPE_DIMS], np.arange(ROPE_DIMS)] = 1.0
    p2 = jnp.asarray(np.kron(np.eye(2, dtype=np.float32), pm), dtype=bf16)

    def pspec(qkv, p):
        return pl.BlockSpec((None, 1, SEQ, LANE), lambda b, h, o=base + (qkv * N_PAT + p) * DIL_HEADS: (b, o + h, 0, 0))

    tspec = pl.BlockSpec((1, SEQ, LANE), lambda b, h: (b, 0, 0))
    in_specs = [pspec(qkv, p) for qkv in range(3) for p in range(N_PAT)] + [
        tspec, tspec, pl.BlockSpec((4, LANE), lambda b, h: (0, 0)), pl.BlockSpec((2 * LANE, 2 * LANE), lambda b, h: (0, 0))]
    return pl.pallas_call(
        _attn_body,
        grid=(B, DIL_HEADS),
        in_specs=in_specs,
        out_specs=pl.BlockSpec((1, SEQ, LANE), lambda b, h: (b, 0, h)),
        out_shape=jax.ShapeDtypeStruct((B, SEQ, DIL_HEADS * DIL_DH), bf16),
        scratch_shapes=[
            pltpu.VMEM((SEQ, LANE), bf16),
            pltpu.VMEM((SEQ + ATTN_BLK, LANE), bf16),
            pltpu.VMEM((SEQ + ATTN_BLK, 2 * LANE), bf16),
            pltpu.VMEM((SEQ, LANE), f32),
            pltpu.VMEM((SEQ, LANE), f32),
            pltpu.VMEM((SEQ, LANE), f32),
            pltpu.VMEM((SEQ, LANE), f32),
            pltpu.VMEM((SEQ, LANE), f32),
            pltpu.VMEM((N_PAT, SEQ, LANE), f32),
            pltpu.VMEM((N_PAT, SEQ, LANE), f32),
            pltpu.VMEM((ATTN_G * ATTN_BLK, 2 * ATTN_BLK), f32),
            pltpu.VMEM((ATTN_G * ATTN_BLK, 2 * ATTN_BLK), f32),
        ],
        compiler_params=_cparams(("parallel", "arbitrary")),
        name="dilated_attn",
    )(*([proj] * 9), cos, ss, gains, p2)


def _merge_body(a_ref, b_ref, c_ref, g0_ref, g1_ref, g2_ref, x_ref, wb_ref, wo_ref, n2_ref, rw_ref, rb_ref, tri_ref,
                x1_ref, h2_ref, ei_ref, ew_ref, pos_ref, cnt_ref, cb_ref, cnt_scr):
    merged = None
    for n, (br, g) in enumerate(((a_ref, g0_ref), (b_ref, g1_ref), (c_ref, g2_ref))):
        logits = jnp.concatenate([g[t] for t in range(D_MODEL // LANE)], axis=1)
        gate = 0.5 * jnp.tanh(0.5 * logits.astype(f32)) + 0.5
        t = gate * jnp.dot(br[...], wb_ref[n], preferred_element_type=f32)
        merged = t if merged is None else merged + t
    x1 = x_ref[...] + jnp.dot(merged.astype(bf16), wo_ref[...], preferred_element_type=f32)
    x1_ref[...] = x1
    h2 = ((x1 * lax.rsqrt(jnp.mean(x1 * x1, axis=-1, keepdims=True) + RMS_EPS)) * n2_ref[...]).astype(bf16)
    h2_ref[...] = h2

    lg = lax.dot_general(rw_ref[...], h2, NT, preferred_element_type=f32) + rb_ref[...]
    ex = jnp.exp(lg - jnp.max(lg, axis=0, keepdims=True))
    sc = ex / jnp.sum(ex, axis=0, keepdims=True)
    srow = [sc[e:e + 1, :] for e in range(N_EXP)]
    best = bidx = None
    for gidx in range(N_GRP):
        s0, s1, s2, s3 = srow[4 * gidx:4 * gidx + 4]
        hi01, lo01, hi23, lo23 = jnp.maximum(s0, s1), jnp.minimum(s0, s1), jnp.maximum(s2, s3), jnp.minimum(s2, s3)
        gs = jnp.maximum(hi01, hi23) + jnp.maximum(jnp.minimum(hi01, hi23), jnp.maximum(lo01, lo23))
        if best is None:
            best, bidx = gs, jnp.zeros_like(gs, dtype=i32)
        else:
            better = gs > best
            best = jnp.where(better, gs, best)
            bidx = jnp.where(better, gidx, bidx)
    cand = []
    for i in range(EXP_PER_GRP):
        v = srow[i]
        for gidx in range(1, N_GRP):
            v = jnp.where(bidx == gidx, srow[4 * gidx + i], v)
        cand.append(v)

    def argmax4(vals):
        bw, bi = vals[0], jnp.zeros_like(bidx)
        for i in range(1, EXP_PER_GRP):
            better = vals[i] > bw
            bw = jnp.where(better, vals[i], bw)
            bi = jnp.where(better, i, bi)
        return bw, bi

    w1, i1 = argmax4(cand)
    w2, i2 = argmax4([jnp.where(i1 == i, -jnp.inf, cand[i]) for i in range(EXP_PER_GRP)])
    tot = w1 + w2
    e1 = bidx * EXP_PER_GRP + i1
    e2 = bidx * EXP_PER_GRP + i2
    ei_ref[0:1, :] = e1
    ei_ref[1:2, :] = e2
    ew_ref[0:1, :] = w1 / tot
    ew_ref[1:2, :] = w2 / tot

    @pl.when(pl.program_id(0) == 0)
    def _():
        cnt_scr[...] = jnp.zeros_like(cnt_scr)

    erow = lax.broadcasted_iota(i32, (N_EXP, MERGE_TM), 0)
    oh1 = erow == e1
    oh2 = erow == e2
    both = jnp.where(oh1 | oh2, 1.0, 0.0)
    incl = jnp.dot(both.astype(bf16), tri_ref[...], preferred_element_type=f32)
    cb_ref[0] = cnt_scr[...]
    base = cnt_scr[...] + (incl - both)
    pos_ref[0:1, :] = jnp.sum(jnp.where(oh1, base, 0.0), axis=0, keepdims=True).astype(i32)
    pos_ref[1:2, :] = jnp.sum(jnp.where(oh2, base, 0.0), axis=0, keepdims=True).astype(i32)
    cnt = cnt_scr[...] + incl[:, MERGE_TM - 1:MERGE_TM]
    cnt_scr[...] = cnt
    cnt_ref[...] = cnt


def _merge(bra, brb, brc, proj, x2d, wb, wo, n2, rw, rb):
    T = bra.shape[0]
    TM = MERGE_TM
    per_b = SEQ // TM
    gt = D_MODEL // LANE

    def gspec(n):
        return pl.BlockSpec((None, gt, TM, LANE), lambda i: (i // per_b, GATE_OFF // D_MODEL + n, i % per_b, 0))

    rspec = lambda w: pl.BlockSpec((TM, w), lambda i: (i, 0))
    full = lambda shape: pl.BlockSpec(shape, lambda i: (0,) * len(shape))
    tri = (jnp.arange(TM)[:, None] <= jnp.arange(TM)[None, :]).astype(bf16)
    return pl.pallas_call(
        _merge_body,
        grid=(T // TM,),
        in_specs=[
            rspec(BR_W), rspec(BR_W), rspec(BR_W),
            gspec(0), gspec(1), gspec(2),
            rspec(D_MODEL),
            full((N_BR, BR_W, D_MODEL)), full((D_MODEL, D_MODEL)), full((1, D_MODEL)),
            full((N_EXP, D_MODEL)), full((N_EXP, 1)), full((TM, TM)),
        ],
        out_specs=[
            rspec(D_MODEL), rspec(D_MODEL),
            pl.BlockSpec((TOP_K, TM), lambda i: (0, i)),
            pl.BlockSpec((TOP_K, TM), lambda i: (0, i)),
            pl.BlockSpec((TOP_K, TM), lambda i: (0, i)),
            full((N_EXP, 1)),
            pl.BlockSpec((1, N_EXP, 1), lambda i: (i, 0, 0)),
        ],
        out_shape=[
            jax.ShapeDtypeStruct((T, D_MODEL), f32),
            jax.ShapeDtypeStruct((T, D_MODEL), bf16),
            jax.ShapeDtypeStruct((TOP_K, T), i32),
            jax.ShapeDtypeStruct((TOP_K, T), f32),
            jax.ShapeDtypeStruct((TOP_K, T), i32),
            jax.ShapeDtypeStruct((N_EXP, 1), f32),
            jax.ShapeDtypeStruct((T // TM, N_EXP, 1), f32),
        ],
        scratch_shapes=[pltpu.VMEM((N_EXP, 1), f32)],
        compiler_params=_cparams(("arbitrary",)),
        name="merge_router",
    )(bra, brb, brc, proj, proj, proj, x2d, wb, wo, n2, rw, rb, tri)


def _expert_body(be_ref, nu_ref, xs_ref, wgu_ref, wd_ref, ys_ref):
    i = pl.program_id(0)

    @pl.when(i < nu_ref[0])
    def _():
        gu = jnp.dot(xs_ref[...], wgu_ref[0], preferred_element_type=f32)
        g = gu[:, :EXP_FF]
        act = (g * jax.nn.sigmoid(g)) * gu[:, EXP_FF:]
        ys_ref[...] = jnp.dot(act.astype(bf16), wd_ref[0], preferred_element_type=f32).astype(bf16)

    @pl.when(i >= nu_ref[0])
    def _():
        ys_ref[...] = jnp.zeros_like(ys_ref)


def _experts(block_e, n_used, xs, wgu, wd):
    n_rows = xs.shape[0]
    grid_spec = pltpu.PrefetchScalarGridSpec(
        num_scalar_prefetch=2,
        grid=(n_rows // MOE_BM,),
        in_specs=[
            pl.BlockSpec((MOE_BM, D_MODEL), lambda i, be, nu: (jnp.minimum(i, nu[0] - 1), 0)),
            pl.BlockSpec((1, D_MODEL, 2 * EXP_FF), lambda i, be, nu: (be[i], 0, 0)),
            pl.BlockSpec((1, EXP_FF, D_MODEL), lambda i, be, nu: (be[i], 0, 0)),
        ],
        out_specs=pl.BlockSpec((MOE_BM, D_MODEL), lambda i, be, nu: (i, 0)),
    )
    return pl.pallas_call(
        _expert_body,
        grid_spec=grid_spec,
        out_shape=jax.ShapeDtypeStruct((n_rows, D_MODEL), bf16),
        compiler_params=_cparams(("arbitrary",)),
        name="experts",
    )(block_e, n_used, xs, wgu, wd)


COMB_TM = 1024


def _combine_body(x1_ref, y0_ref, y1_ref, w_ref, o_ref):
    w = w_ref[...]
    o_ref[...] = x1_ref[...] + y0_ref[...].astype(f32) * w[:, 0:1] + y1_ref[...].astype(f32) * w[:, 1:2]


def _combine(x1, y0, y1, w):
    T = x1.shape[0]
    rspec = pl.BlockSpec((COMB_TM, D_MODEL), lambda i: (i, 0))
    return pl.pallas_call(
        _combine_body,
        grid=(T // COMB_TM,),
        in_specs=[rspec, rspec, rspec, pl.BlockSpec((COMB_TM, TOP_K), lambda i: (i, 0))],
        out_specs=rspec,
        out_shape=jax.ShapeDtypeStruct((T, D_MODEL), f32),
        compiler_params=_cparams(("parallel",)),
        name="moe_combine",
    )(x1, y0, y1, w)


DISP_ALIGN = 16
DISP_SLOTS = MERGE_TM * TOP_K + N_EXP * DISP_ALIGN
DISP_SIZES = tuple(DISP_ALIGN << s for s in range(5, -1, -1))
TAIL_SIZES = tuple(sz for sz in (MOE_BM // 2 >> s for s in range(12)) if sz >= DISP_ALIGN)


def _pieces(length, sizes, fn):
    done = jnp.int32(0)
    for sz in sizes:
        take = (length & sz) != 0

        @pl.when(take)
        def _(done=done, sz=sz):
            fn(done, sz)

        done = done + jnp.where(take, sz, 0)


def _dispatch_body(off_ref, n_ref, ls_ref, toff_ref, tlen_ref, h2_ref, ei_ref, pos_ref, adj_ref, xs_hbm,
                   srt, zbuf, sem, zsem):
    i = pl.program_id(0)
    last = pl.num_programs(0) - 1
    cur = i % 2
    TM = MERGE_TM

    def slab_copies(step, buf, start):
        for e in range(N_EXP):
            ls = ls_ref[step * N_EXP + e]
            off = off_ref[step * N_EXP + e]

            def piece(done, sz, ls=ls, off=off):
                cp = pltpu.make_async_copy(srt.at[buf, pl.ds(pl.multiple_of(ls + done, DISP_ALIGN), sz)],
                                           xs_hbm.at[pl.ds(pl.multiple_of(off + done, DISP_ALIGN), sz)], sem.at[buf])
                cp.start() if start else cp.wait()

            _pieces(n_ref[step * N_EXP + e], DISP_SIZES, piece)

    def tail_copies(start):
        for e in range(N_EXP):
            off = toff_ref[e]

            def piece(done, sz, off=off):
                cp = pltpu.make_async_copy(zbuf.at[pl.ds(0, sz)],
                                           xs_hbm.at[pl.ds(pl.multiple_of(off + done, DISP_ALIGN), sz)], zsem)
                cp.start() if start else cp.wait()

            _pieces(tlen_ref[e], TAIL_SIZES, piece)

        zrows = zbuf.shape[0]

        def block(c, carry):
            cp = pltpu.make_async_copy(zbuf, xs_hbm.at[pl.ds(pl.multiple_of(toff_ref[N_EXP] + c * zrows, zrows), zrows)],
                                       zsem)
            cp.start() if start else cp.wait()
            return carry

        lax.fori_loop(0, tlen_ref[N_EXP] // zrows, block, 0)

    @pl.when(i == 0)
    def _():
        zbuf[...] = jnp.zeros_like(zbuf)
        tail_copies(True)

    @pl.when(i >= 2)
    def _():
        slab_copies(i - 2, cur, False)

    erow = lax.broadcasted_iota(i32, (N_EXP, TM), 0)
    adj = adj_ref[0]

    def slot(k):
        sel = jnp.sum(jnp.where(erow == ei_ref[k:k + 1, :], adj, 0.0), axis=0, keepdims=True)
        return pos_ref[k:k + 1, :] + sel.astype(i32)

    srow = lax.broadcasted_iota(i32, (DISP_SLOTS, TM), 0)
    perm = jnp.where((srow == slot(0)) | (srow == slot(1)), 1.0, 0.0).astype(bf16)
    srt[cur] = jnp.dot(perm, h2_ref[...], preferred_element_type=f32).astype(bf16)
    slab_copies(i, cur, True)

    @pl.when(i == last)
    def _():
        slab_copies(i, cur, False)

        @pl.when(i >= 1)
        def _():
            slab_copies(i - 1, 1 - cur, False)

        tail_copies(False)


def _dispatch(off, n16, lstart, tail_off, tail_len, h2, eidx, pos, adj, n_rows):
    T = h2.shape[0]
    TM = MERGE_TM
    grid_spec = pltpu.PrefetchScalarGridSpec(
        num_scalar_prefetch=5,
        grid=(T // TM,),
        in_specs=[
            pl.BlockSpec((TM, D_MODEL), lambda i, *_: (i, 0)),
            pl.BlockSpec((TOP_K, TM), lambda i, *_: (0, i)),
            pl.BlockSpec((TOP_K, TM), lambda i, *_: (0, i)),
            pl.BlockSpec((1, N_EXP, 1), lambda i, *_: (i, 0, 0)),
        ],
        out_specs=pl.BlockSpec(memory_space=pl.ANY),
        scratch_shapes=[
            pltpu.VMEM((2, DISP_SLOTS, D_MODEL), bf16),
            pltpu.VMEM((TAIL_SIZES[0], D_MODEL), bf16),
            pltpu.SemaphoreType.DMA((2,)),
            pltpu.SemaphoreType.DMA(()),
        ],
    )
    return pl.pallas_call(
        _dispatch_body,
        grid_spec=grid_spec,
        out_shape=jax.ShapeDtypeStruct((n_rows, D_MODEL), bf16),
        compiler_params=_cparams(("arbitrary",)),
        name="moe_dispatch",
    )(off, n16, lstart, tail_off, tail_len, h2, eidx, pos, adj)


def _moe(x1, h2, eidx, ew, pos, cnt, cb, wgu, wd):
    T = x1.shape[0]
    TM = MERGE_TM
    n_tiles = T // TM
    ceil_to = lambda v, m: (v + m - 1) // m * m
    n_rows = ceil_to(T * TOP_K + n_tiles * N_EXP * (DISP_ALIGN - 1) + N_EXP * (MOE_BM - 1), MOE_BM)
    n_blocks = n_rows // MOE_BM
    cbi = cb[:, :, 0].astype(i32)
    tile_cnt = jnp.concatenate([cbi[1:], cnt[:, 0].astype(i32)[None, :]], axis=0) - cbi
    n16 = ceil_to(tile_cnt, DISP_ALIGN)
    lstart = jnp.cumsum(n16, axis=1) - n16
    before = jnp.cumsum(n16, axis=0) - n16
    used = jnp.sum(n16, axis=0)
    padded = ceil_to(used, MOE_BM)
    pad_ends = jnp.cumsum(padded)
    off = (pad_ends - padded)[None, :] + before
    adj = (lstart - cbi).astype(f32)[:, :, None]
    row_adj = jnp.repeat(off - cbi, TM, axis=0).T
    dest = pos
    for e in range(N_EXP):
        dest = dest + jnp.where(eidx == e, row_adj[e][None, :], 0)
    blk_start = jnp.arange(n_blocks, dtype=i32) * MOE_BM
    block_e = jnp.minimum(jnp.sum((blk_start[:, None] >= pad_ends[None, :]).astype(i32), axis=1), N_EXP - 1)
    n_used = (pad_ends[-1:] // MOE_BM).astype(i32)
    tail_off = jnp.concatenate([pad_ends - padded + used, pad_ends[-1:]])
    tail_len = jnp.concatenate([padded - used, n_rows - pad_ends[-1:]])
    xs = _dispatch(off.reshape(-1), n16.reshape(-1), lstart.reshape(-1), tail_off, tail_len, h2, eidx, pos, adj, n_rows)
    ys = _experts(block_e, n_used, xs, wgu, wd)
    return _combine(x1, ys[dest[0]], ys[dest[1]], ew.T)


def _pack_w_in(w):
    cuts = np.cumsum([256, 256, 512, 512, 16, 512, 512, 4608, 3072])[:-1].tolist()
    q, k, v, og, glow, xb, yb, qkv, gates = jnp.split(w, cuts, axis=-1)
    packed = jnp.concatenate([gates, qkv, xb, yb, q, k, v, og], axis=-1).astype(bf16)
    wg = jnp.pad(glow, ((0, 0), (0, LANE - GLA_RANK))).astype(bf16)
    return packed, wg


def _block_diag_lru(w_a, w_x):
    bs = LRU_W // LRU_BLOCKS
    per = LANE // bs
    tiles = []
    for t in range(LRU_W // LANE):
        halves = []
        for w in (w_a, w_x):
            m = jnp.zeros((LANE, LANE), f32)
            for j in range(per):
                m = m.at[j * bs:(j + 1) * bs, j * bs:(j + 1) * bs].set(w[t * per + j])
            halves.append(m)
        tiles.append(jnp.concatenate(halves, axis=1))
    return jnp.stack(tiles).astype(bf16)


def kernel(x, positions, norm1_gain, w_in, gla_gate_up, gla_gate_bias, gla_out_gain, lru_conv_w, lru_conv_b, lru_w_a, lru_b_a, lru_w_x, lru_b_x, lru_lambda, q_norm_gain, k_norm_gain, w_branch, w_out, norm2_gain, router_w, router_b, w_gate, w_up, w_down):
    B, S, D = x.shape
    assert S == SEQ and D == D_MODEL
    T = B * S
    depth = w_in.shape[0]

    half = ROPE_DIMS // 2
    inv_freq = ROPE_THETA ** (-jnp.arange(half, dtype=f32) / half)
    freq = jnp.zeros((1, LANE), f32).at[0, :ROPE_DIMS].set(jnp.concatenate([inv_freq, inv_freq]))
    cos, ss = _rope_tables(positions.astype(f32)[:, :, None], freq)

    rw = router_w.T.astype(bf16)
    rb = router_b.astype(f32)[:, None]
    for l in range(depth):
        w_packed, wg = _pack_w_in(w_in[l])
        wup = jnp.pad(gla_gate_up[l], ((0, LANE - GLA_RANK), (0, 0)))
        nt = LRU_W // LANE
        lru_w = _block_diag_lru(lru_w_a[l], lru_w_x[l])
        lru_b = jnp.concatenate([lru_b_a[l].reshape(nt, 1, LANE), lru_b_x[l].reshape(nt, 1, LANE)], axis=-1)
        wgu = jnp.concatenate([w_gate[l], w_up[l]], axis=-1).astype(bf16)

        proj, glow = _inproj(x, norm1_gain[l][None, :], w_packed, wg)
        br_a = _gla(proj, glow, wup, gla_gate_bias[l][None, :], gla_out_gain[l][None, :])
        br_b = _lru(proj, lru_conv_w[l], lru_conv_b[l][None, :], lru_w, lru_b, lru_lambda[l][None, :])
        br_c = _attn(proj, cos, ss, q_norm_gain[l], k_norm_gain[l])
        routed = _merge(br_a.reshape(T, BR_W), br_b.reshape(T, BR_W), br_c.reshape(T, BR_W), proj, x.reshape(T, D),
                        w_branch[l].astype(bf16), w_out[l].astype(bf16), norm2_gain[l][None, :], rw, rb)
        x = _moe(*routed, wgu, w_down[l].astype(bf16)).reshape(B, S, D)
    return x
```

```python
import jax
import jax.numpy as jnp
import numpy as np
from jax import lax
from jax.experimental import pallas as pl
from jax.experimental.pallas import tpu as pltpu

f32 = jnp.float32
bf16 = jnp.bfloat16
i32 = jnp.int32

D_MODEL = 1024
SEQ = 2048
RMS_EPS = 1e-6
GLA_HEADS, GLA_DK, GLA_DV, GLA_RANK, GLA_NORM, GLA_CHUNK = 4, 64, 128, 16, 16.0, 64
LRU_W, LRU_BLOCKS, LRU_CONV, LRU_C = 512, 8, 4, 8.0
DIL_PATTERNS = ((128, 1), (512, 4), (2048, 16))
N_PAT, DIL_HEADS, DIL_DH, ATTN_BLK = 3, 4, 128, 128
ROPE_THETA, ROPE_DIMS = 500000.0, 32
N_BR, BR_W = 3, 512
N_EXP, N_GRP, EXP_PER_GRP, TOP_K, EXP_FF = 16, 4, 4, 2, 512

LANE = 128
GATE_OFF, DIL_OFF, XB_OFF, YB_OFF, Q_OFF, K_OFF, V_OFF, OG_OFF = 0, 3072, 7680, 8192, 8704, 8960, 9216, 9728
PROJ_COLS = 10240
PROJ_TN = 2048
PROJ_RC = 512
MERGE_TM = 512
MOE_BM = 1024
VMEM_LIMIT = 56 * 1024 * 1024

NT = (((1,), (1,)), ((), ()))


def _cparams(sem):
    return pltpu.CompilerParams(dimension_semantics=sem, vmem_limit_bytes=VMEM_LIMIT)


def _inproj_body(x_ref, g_ref, w_ref, wg_ref, proj_ref, glow_ref, h_scr):
    j = pl.program_id(1)
    nrc = SEQ // PROJ_RC

    @pl.when(j == 0)
    def _():
        for c in range(nrc):
            rows = pl.ds(c * PROJ_RC, PROJ_RC)
            x = x_ref[0, rows, :]
            ms = jnp.mean(x * x, axis=-1, keepdims=True)
            h = ((x * lax.rsqrt(ms + RMS_EPS)) * g_ref[...]).astype(bf16)
            h_scr[rows, :] = h
            glow_ref[0, rows, :] = jnp.dot(h, wg_ref[...], preferred_element_type=f32)

    for c in range(nrc):
        rows = pl.ds(c * PROJ_RC, PROJ_RC)
        res = jnp.dot(h_scr[rows, :], w_ref[...], preferred_element_type=f32).astype(bf16)
        for t in range(PROJ_TN // LANE):
            proj_ref[0, t, rows, :] = res[:, t * LANE:(t + 1) * LANE]


def _inproj(x, gain, w, wg):
    B = x.shape[0]
    nj = PROJ_COLS // PROJ_TN
    return pl.pallas_call(
        _inproj_body,
        grid=(B, nj),
        in_specs=[
            pl.BlockSpec((1, SEQ, D_MODEL), lambda b, j: (b, 0, 0)),
            pl.BlockSpec((1, D_MODEL), lambda b, j: (0, 0)),
            pl.BlockSpec((D_MODEL, PROJ_TN), lambda b, j: (0, j)),
            pl.BlockSpec((D_MODEL, LANE), lambda b, j: (0, 0)),
        ],
        out_specs=[
            pl.BlockSpec((1, PROJ_TN // LANE, SEQ, LANE), lambda b, j: (b, j, 0, 0)),
            pl.BlockSpec((1, SEQ, LANE), lambda b, j: (b, 0, 0)),
        ],
        out_shape=[
            jax.ShapeDtypeStruct((B, PROJ_COLS // LANE, SEQ, LANE), bf16),
            jax.ShapeDtypeStruct((B, SEQ, LANE), f32),
        ],
        scratch_shapes=[pltpu.VMEM((SEQ, D_MODEL), bf16)],
        compiler_params=_cparams(("parallel", "arbitrary")),
        name="inproj",
    )(x, gain, w, wg)


def _gla_body(q_ref, k_ref, v_ref, og_ref, glow_ref, wh_ref, wl_ref, bg_ref, gain_ref, o_ref, la_scr, st_scr, stb_scr):
    C = GLA_CHUNK
    HK = GLA_HEADS * GLA_DK
    PR = 512
    for c in range(SEQ // PR):
        rows = pl.ds(c * PR, PR)
        g = glow_ref[0, rows, :]
        gh = g.astype(bf16)
        gl = (g - gh.astype(f32)).astype(bf16)
        z = (jnp.dot(gh, wh_ref[...], preferred_element_type=f32)
             + jnp.dot(gl, wh_ref[...], preferred_element_type=f32)
             + jnp.dot(gh, wl_ref[...], preferred_element_type=f32)) + bg_ref[...]
        la_scr[rows, :] = (jnp.minimum(z, 0.0) - jnp.log1p(jnp.exp(-jnp.abs(z)))) * (1.0 / GLA_NORM)
    st_scr[...] = jnp.zeros_like(st_scr)
    stb_scr[...] = jnp.zeros_like(stb_scr)

    causal = (lax.broadcasted_iota(i32, (C, GLA_HEADS * C), 0)
              >= lax.broadcasted_iota(i32, (C, GLA_HEADS * C), 1) % C)
    tri = (lax.broadcasted_iota(i32, (C, C), 0) >= lax.broadcasted_iota(i32, (C, C), 1)).astype(bf16)
    lane_head = lax.broadcasted_iota(i32, (1, HK), 1) // GLA_DK
    gain = gain_ref[...]
    zero_v = jnp.zeros((C, GLA_DV), bf16)

    def chunk(n, carry):
        rows = pl.ds(pl.multiple_of(n * C, C), C)
        la = la_scr[rows, :]
        p1 = la.astype(bf16)
        r1 = la - p1.astype(f32)
        p2 = r1.astype(bf16)
        p3 = (r1 - p2.astype(f32)).astype(bf16)
        c3 = jnp.dot(tri, jnp.concatenate([p1, p2, p3], axis=1), preferred_element_type=f32)
        cum = (c3[:, :HK] + c3[:, HK:2 * HK]) + c3[:, 2 * HK:]
        cum_t = cum.T
        tot_t = cum_t[:, C - 1:C]
        qf = jnp.concatenate([q_ref[0, t, rows, :] for t in range(HK // LANE)], axis=1).astype(f32)
        kf = jnp.concatenate([k_ref[0, t, rows, :] for t in range(HK // LANE)], axis=1).astype(f32)
        qd = (qf * (GLA_DK ** -0.5) * jnp.exp(cum)).astype(bf16)
        ki = kf * jnp.exp(-cum)
        kd_t = (kf.T * jnp.exp(tot_t - cum_t)).astype(bf16)
        dec_t = jnp.exp(tot_t)
        heads = range(GLA_HEADS)
        vcols = [v_ref[0, h, rows, :] for h in heads]
        k_st = jnp.concatenate([jnp.where(lane_head == h, ki, 0.0).astype(bf16) for h in heads], axis=0)
        v_bd = jnp.concatenate([jnp.concatenate([vcols[h] if g == h else zero_v for g in heads], axis=1)
                                for h in heads], axis=0)
        s = lax.dot_general(qd, k_st, NT, preferred_element_type=f32)
        s = jnp.where(causal, s, 0.0).astype(bf16)
        o = jnp.dot(s, v_bd, preferred_element_type=f32) + jnp.dot(qd, stb_scr[...], preferred_element_type=f32)
        for h in heads:
            cols = slice(h * GLA_DV, (h + 1) * GLA_DV)
            hr = slice(h * GLA_DK, (h + 1) * GLA_DK)
            st = st_scr[h] * dec_t[hr, :] + jnp.dot(kd_t[hr, :], vcols[h], preferred_element_type=f32)
            st_scr[h] = st
            stb_scr[hr, cols] = st.astype(bf16)
            oh = o[:, cols]
            oh = oh * lax.rsqrt(jnp.mean(oh * oh, axis=-1, keepdims=True) + RMS_EPS) * gain
            g = og_ref[0, h, rows, :].astype(f32)
            o_ref[0, rows, cols] = (oh * (g * jax.nn.sigmoid(g))).astype(bf16)
        return carry

    lax.fori_loop(0, SEQ // C, chunk, 0, unroll=4)


def _gla(proj, glow, wup, bg, gain):
    B = proj.shape[0]
    HK = GLA_HEADS * GLA_DK
    HV = GLA_HEADS * GLA_DV
    wup_hi = wup.astype(bf16)
    wup_lo = (wup - wup_hi.astype(f32)).astype(bf16)
    return pl.pallas_call(
        _gla_body,
        grid=(B,),
        in_specs=[
            pl.BlockSpec((1, HK // LANE, SEQ, LANE), lambda b: (b, Q_OFF // HK, 0, 0)),
            pl.BlockSpec((1, HK // LANE, SEQ, LANE), lambda b: (b, K_OFF // HK, 0, 0)),
            pl.BlockSpec((1, HV // LANE, SEQ, LANE), lambda b: (b, V_OFF // HV, 0, 0)),
            pl.BlockSpec((1, HV // LANE, SEQ, LANE), lambda b: (b, OG_OFF // HV, 0, 0)),
            pl.BlockSpec((1, SEQ, LANE), lambda b: (b, 0, 0)),
            pl.BlockSpec((LANE, HK), lambda b: (0, 0)),
            pl.BlockSpec((LANE, HK), lambda b: (0, 0)),
            pl.BlockSpec((1, HK), lambda b: (0, 0)),
            pl.BlockSpec((1, GLA_DV), lambda b: (0, 0)),
        ],
        out_specs=pl.BlockSpec((1, SEQ, HV), lambda b: (b, 0, 0)),
        out_shape=jax.ShapeDtypeStruct((B, SEQ, HV), bf16),
        scratch_shapes=[
            pltpu.VMEM((SEQ, HK), f32),
            pltpu.VMEM((GLA_HEADS, GLA_DK, GLA_DV), f32),
            pltpu.VMEM((HK, HV), bf16),
        ],
        compiler_params=_cparams(("parallel",)),
        name="gla",
    )(proj, proj, proj, proj, glow, wup_hi, wup_lo, bg, gain)


LRU_GRP = 8
LRU_RC = 512
LRU_PAD = 8
LRU_NT = LRU_W // LANE


def _lru_body(xb_ref, yb_ref, cw_ref, cb_ref, w_ref, b_ref, lam_ref, o_ref, xpad, a_scr, h_scr):
    grp = lax.broadcasted_iota(i32, (LRU_RC // LRU_GRP, LRU_GRP, LANE), 1)
    for t in range(LRU_NT):
        lanes = slice(t * LANE, (t + 1) * LANE)
        xpad[pl.ds(0, LRU_PAD), :] = jnp.zeros((LRU_PAD, LANE), f32)
        xpad[pl.ds(LRU_PAD, SEQ), :] = xb_ref[t].astype(f32)
        nl = -lam_ref[:, lanes]
        sp = jnp.maximum(nl, 0.0) + jnp.log1p(jnp.exp(-jnp.abs(nl)))
        for c in range(SEQ // LRU_RC):
            r0 = c * LRU_RC
            xc = cb_ref[:, lanes]
            for j in range(LRU_CONV):
                xc = xc + xpad[pl.ds(LRU_PAD - j + r0, LRU_RC), :] * cw_ref[LRU_CONV - 1 - j:LRU_CONV - j, lanes]
            z = jnp.dot(xc.astype(bf16), w_ref[t], preferred_element_type=f32) + b_ref[t]
            r = jax.nn.sigmoid(z[:, :LANE])
            gi = jax.nn.sigmoid(z[:, LANE:])
            log_a = (-LRU_C) * r * sp
            a = jnp.exp(log_a)
            u = jnp.sqrt(jnp.tanh(-log_a) * (a * a + 1.0)) * gi * xc
            a = a.reshape(LRU_RC // LRU_GRP, LRU_GRP, LANE)
            u = u.reshape(LRU_RC // LRU_GRP, LRU_GRP, LANE)
            k = 1
            while k < LRU_GRP:
                inside = grp >= k
                u = u + a * jnp.where(inside, pltpu.roll(u, k, 1), 0.0)
                a = a * jnp.where(inside, pltpu.roll(a, k, 1), 1.0)
                k *= 2
            a_scr[t, pl.ds(r0, LRU_RC), :] = a.reshape(LRU_RC, LANE)
            h_scr[t, pl.ds(r0, LRU_RC), :] = u.reshape(LRU_RC, LANE)

    def group(g, carry):
        rows = pl.ds(pl.multiple_of(g * LRU_GRP, LRU_GRP), LRU_GRP)
        out = []
        for t in range(LRU_NT):
            h = h_scr[t, rows, :] + a_scr[t, rows, :] * jnp.broadcast_to(carry[t], (LRU_GRP, LANE))
            h_scr[t, rows, :] = h
            out.append(h[LRU_GRP - 1:LRU_GRP, :])
        return tuple(out)

    lax.fori_loop(0, SEQ // LRU_GRP, group, tuple(jnp.zeros((1, LANE), f32) for _ in range(LRU_NT)), unroll=8)
    for t in range(LRU_NT):
        for c in range(SEQ // LRU_RC):
            rows = pl.ds(c * LRU_RC, LRU_RC)
            o_ref[0, rows, t * LANE:(t + 1) * LANE] = (h_scr[t, rows, :]
                                                       * jax.nn.gelu(yb_ref[t, rows, :].astype(f32))).astype(bf16)


def _lru(proj, cw, cb, w, b, lam):
    B = proj.shape[0]
    full = lambda shape: pl.BlockSpec(shape, lambda b: (0,) * len(shape))
    return pl.pallas_call(
        _lru_body,
        grid=(B,),
        in_specs=[
            pl.BlockSpec((None, LRU_NT, SEQ, LANE), lambda b: (b, XB_OFF // LRU_W, 0, 0)),
            pl.BlockSpec((None, LRU_NT, SEQ, LANE), lambda b: (b, YB_OFF // LRU_W, 0, 0)),
            full((LRU_CONV, LRU_W)), full((1, LRU_W)), full((LRU_NT, LANE, 2 * LANE)), full((LRU_NT, 1, 2 * LANE)),
            full((1, LRU_W)),
        ],
        out_specs=pl.BlockSpec((1, SEQ, LRU_W), lambda b: (b, 0, 0)),
        out_shape=jax.ShapeDtypeStruct((B, SEQ, LRU_W), bf16),
        scratch_shapes=[
            pltpu.VMEM((SEQ + LRU_PAD, LANE), f32),
            pltpu.VMEM((LRU_NT, SEQ, LANE), f32),
            pltpu.VMEM((LRU_NT, SEQ, LANE), f32),
        ],
        compiler_params=_cparams(("parallel",)),
        name="lru",
    )(proj, proj, cw, cb, w, b, lam)


def _rope_body(pos_ref, freq_ref, cos_ref, ss_ref):
    ang = pos_ref[0] * freq_ref[...]
    lane = lax.broadcasted_iota(i32, (SEQ, LANE), 1)
    sn = jnp.sin(ang)
    cos_ref[0] = jnp.cos(ang)
    ss_ref[0] = jnp.where(lane < ROPE_DIMS // 2, -sn, sn)


def _rope_tables(posf, freq):
    B = posf.shape[0]
    spec = pl.BlockSpec((1, SEQ, LANE), lambda b: (b, 0, 0))
    return pl.pallas_call(
        _rope_body,
        grid=(B,),
        in_specs=[pl.BlockSpec((1, SEQ, 1), lambda b: (b, 0, 0)), pl.BlockSpec((1, LANE), lambda b: (0, 0))],
        out_specs=[spec, spec],
        out_shape=[jax.ShapeDtypeStruct((B, SEQ, LANE), f32)] * 2,
        compiler_params=_cparams(("parallel",)),
        name="rope_tables",
    )(posf, freq)


def _rope_partner():
    half = ROPE_DIMS // 2
    perm = np.arange(LANE)
    perm[:half] += half
    perm[half:ROPE_DIMS] -= half
    return perm


ATTN_PC = 256
ATTN_G = 4


def _attn_body(q0, q1, q2, k0, k1, k2, v0, v1, v2, cos_ref, ss_ref, g_ref, p2_ref, o_ref,
               qd, kd, vd, tq, tk, tv, od, ld, on, ln, sc_a, sc_b):
    q_refs, k_refs, v_refs = (q0, q1, q2), (k0, k1, k2), (v0, v1, v2)
    BLK = ATTN_BLK
    qg = g_ref[0:1, :] * (DIL_DH ** -0.5)
    qgr = g_ref[1:2, :] * (DIL_DH ** -0.5)
    kg = g_ref[2:3, :]
    kgr = g_ref[3:4, :]
    kd[pl.ds(0, BLK), :] = jnp.zeros((BLK, LANE), bf16)
    vd[pl.ds(0, BLK), pl.ds(0, LANE)] = jnp.zeros((BLK, LANE), bf16)
    vd[:, pl.ds(LANE, LANE)] = jnp.ones((SEQ + BLK, LANE), bf16)

    qi = lax.broadcasted_iota(i32, (BLK, 2 * BLK), 0)
    kj = lax.broadcasted_iota(i32, (BLK, 2 * BLK), 1)
    is_cur = kj >= BLK
    band = jnp.where(is_cur, kj - BLK, qi) <= jnp.where(is_cur, qi, kj)

    def norm_rope(raw, partner, gain, gain_p, cs, ss):
        rf = raw.astype(f32)
        r = lax.rsqrt(jnp.mean(rf * rf, axis=-1, keepdims=True) + RMS_EPS)
        return (rf * (gain * cs) + partner * (gain_p * ss)) * r

    for p, (_, dil) in enumerate(DIL_PATTERNS):
        L = SEQ // dil
        nbl = L // BLK

        def prep(c, carry, p=p, dil=dil):
            rows = pl.ds(pl.multiple_of(c * ATTN_PC, ATTN_PC), ATTN_PC)
            cs, ss = cos_ref[0, rows, :], ss_ref[0, rows, :]
            qr = q_refs[p][0, rows, :]
            kr = k_refs[p][0, rows, :]
            partner = jnp.dot(jnp.concatenate([qr, kr], axis=1), p2_ref[...], preferred_element_type=f32)
            q = norm_rope(qr, partner[:, :LANE], qg, qgr, cs, ss)
            k = norm_rope(kr, partner[:, LANE:], kg, kgr, cs, ss)
            if dil == 1:
                orow = pl.ds(pl.multiple_of(c * ATTN_PC, ATTN_PC) + BLK, ATTN_PC)
                qd[rows, :] = q.astype(bf16)
                kd[orow, :] = k.astype(bf16)
                vd[orow, pl.ds(0, LANE)] = v_refs[p][0, rows, :]
            else:
                tq[rows, :] = q
                tk[rows, :] = k
                tv[rows, :] = v_refs[p][0, rows, :].astype(f32)
            return carry

        lax.fori_loop(0, SEQ // ATTN_PC, prep, 0, unroll=2)
        if dil > 1:
            for r in range(dil):
                src = pl.ds(r, L, stride=dil)
                qd[pl.ds(r * L, L), :] = tq[src, :].astype(bf16)
                kd[pl.ds(BLK + r * L, L), :] = tk[src, :].astype(bf16)
                vd[pl.ds(BLK + r * L, L), pl.ds(0, LANE)] = tv[src, :].astype(bf16)

        o_dst, l_dst = (on.at[p], ln.at[p]) if dil == 1 else (od, ld)

        def scores(g, dst):
            for j in range(ATTN_G):
                r0 = pl.multiple_of((g * ATTN_G + j) * BLK, BLK)
                dst[pl.ds(j * BLK, BLK), :] = lax.dot_general(qd[pl.ds(r0, BLK), :], kd[pl.ds(r0, 2 * BLK), :], NT,
                                                              preferred_element_type=f32)

        def finish(g, src, nbl=nbl, o_dst=o_dst, l_dst=l_dst):
            for j in range(ATTN_G):
                b = g * ATTN_G + j
                r0 = pl.multiple_of(b * BLK, BLK)
                first = (b % nbl) == 0
                valid = band & (is_cur | jnp.logical_not(first))
                s = jnp.where(valid, src[pl.ds(j * BLK, BLK), :], -jnp.inf)
                m = jnp.max(s, axis=-1, keepdims=True)
                e = jnp.exp(s - m).astype(bf16)
                ov = jnp.dot(e, vd[pl.ds(r0, 2 * BLK), :], preferred_element_type=f32)
                den = ov[:, LANE:]
                o_dst[pl.ds(r0, BLK), :] = ov[:, :LANE] / den
                l_dst[pl.ds(r0, BLK), :] = m + jnp.log(den)

        n_groups = SEQ // BLK // ATTN_G
        scores(0, sc_a)

        def pair(i, carry, scores=scores, finish=finish):
            scores(2 * i + 1, sc_b)
            finish(2 * i, sc_a)
            scores(jnp.minimum(2 * i + 2, n_groups - 1), sc_a)
            finish(2 * i + 1, sc_b)
            return carry

        lax.fori_loop(0, n_groups // 2, pair, 0)
        if dil > 1:
            for r in range(dil):
                dst = pl.ds(r, L, stride=dil)
                on[p, dst, :] = od[pl.ds(r * L, L), :]
                ln[p, dst, :] = ld[pl.ds(r * L, L), :]

    def combine(c, carry):
        rows = pl.ds(pl.multiple_of(c * ATTN_PC, ATTN_PC), ATTN_PC)
        l0, l1, l2 = ln[0, rows, :], ln[1, rows, :], ln[2, rows, :]
        mx = jnp.maximum(jnp.maximum(l0, l1), l2)
        w0, w1, w2 = jnp.exp(l0 - mx), jnp.exp(l1 - mx), jnp.exp(l2 - mx)
        o = (w0 * on[0, rows, :] + w1 * on[1, rows, :] + w2 * on[2, rows, :]) / (w0 + w1 + w2)
        o_ref[0, rows, :] = o.astype(bf16)
        return carry

    lax.fori_loop(0, SEQ // ATTN_PC, combine, 0)


def _attn(proj, cos, ss, qg, kg):
    B = proj.shape[0]
    base = DIL_OFF // LANE
    perm = _rope_partner()
    gains = jnp.stack([qg, qg[perm], kg, kg[perm]]).astype(f32)
    pm = np.zeros((LANE, LANE), np.float32)
    pm[perm[:ROPE_DIMS], np.arange(ROPE_DIMS)] = 1.0
    p2 = jnp.asarray(np.kron(np.eye(2, dtype=np.float32), pm), dtype=bf16)

    def pspec(qkv, p):
        return pl.BlockSpec((None, 1, SEQ, LANE), lambda b, h, o=base + (qkv * N_PAT + p) * DIL_HEADS: (b, o + h, 0, 0))

    tspec = pl.BlockSpec((1, SEQ, LANE), lambda b, h: (b, 0, 0))
    in_specs = [pspec(qkv, p) for qkv in range(3) for p in range(N_PAT)] + [
        tspec, tspec, pl.BlockSpec((4, LANE), lambda b, h: (0, 0)), pl.BlockSpec((2 * LANE, 2 * LANE), lambda b, h: (0, 0))]
    return pl.pallas_call(
        _attn_body,
        grid=(B, DIL_HEADS),
        in_specs=in_specs,
        out_specs=pl.BlockSpec((1, SEQ, LANE), lambda b, h: (b, 0, h)),
        out_shape=jax.ShapeDtypeStruct((B, SEQ, DIL_HEADS * DIL_DH), bf16),
        scratch_shapes=[
            pltpu.VMEM((SEQ, LANE), bf16),
            pltpu.VMEM((SEQ + ATTN_BLK, LANE), bf16),
            pltpu.VMEM((SEQ + ATTN_BLK, 2 * LANE), bf16),
            pltpu.VMEM((SEQ, LANE), f32),
            pltpu.VMEM((SEQ, LANE), f32),
            pltpu.VMEM((SEQ, LANE), f32),
            pltpu.VMEM((SEQ, LANE), f32),
            pltpu.VMEM((SEQ, LANE), f32),
            pltpu.VMEM((N_PAT, SEQ, LANE), f32),
            pltpu.VMEM((N_PAT, SEQ, LANE), f32),
            pltpu.VMEM((ATTN_G * ATTN_BLK, 2 * ATTN_BLK), f32),
            pltpu.VMEM((ATTN_G * ATTN_BLK, 2 * ATTN_BLK), f32),
        ],
        compiler_params=_cparams(("parallel", "arbitrary")),
        name="dilated_attn",
    )(*([proj] * 9), cos, ss, gains, p2)


def _merge_body(a_ref, b_ref, c_ref, g0_ref, g1_ref, g2_ref, x_ref, wb_ref, wo_ref, n2_ref, rw_ref, rb_ref, tri_ref,
                x1_ref, h2_ref, ei_ref, ew_ref, pos_ref, cnt_ref, cb_ref, cnt_scr):
    merged = None
    for n, (br, g) in enumerate(((a_ref, g0_ref), (b_ref, g1_ref), (c_ref, g2_ref))):
        logits = jnp.concatenate([g[t] for t in range(D_MODEL // LANE)], axis=1)
        gate = 0.5 * jnp.tanh(0.5 * logits.astype(f32)) + 0.5
        t = gate * jnp.dot(br[...], wb_ref[n], preferred_element_type=f32)
        merged = t if merged is None else merged + t
    x1 = x_ref[...] + jnp.dot(merged.astype(bf16), wo_ref[...], preferred_element_type=f32)
    x1_ref[...] = x1
    h2 = ((x1 * lax.rsqrt(jnp.mean(x1 * x1, axis=-1, keepdims=True) + RMS_EPS)) * n2_ref[...]).astype(bf16)
    h2_ref[...] = h2

    lg = lax.dot_general(rw_ref[...], h2, NT, preferred_element_type=f32) + rb_ref[...]
    ex = jnp.exp(lg - jnp.max(lg, axis=0, keepdims=True))
    sc = ex / jnp.sum(ex, axis=0, keepdims=True)
    srow = [sc[e:e + 1, :] for e in range(N_EXP)]
    best = bidx = None
    for gidx in range(N_GRP):
        s0, s1, s2, s3 = srow[4 * gidx:4 * gidx + 4]
        hi01, lo01, hi23, lo23 = jnp.maximum(s0, s1), jnp.minimum(s0, s1), jnp.maximum(s2, s3), jnp.minimum(s2, s3)
        gs = jnp.maximum(hi01, hi23) + jnp.maximum(jnp.minimum(hi01, hi23), jnp.maximum(lo01, lo23))
        if best is None:
            best, bidx = gs, jnp.zeros_like(gs, dtype=i32)
        else:
            better = gs > best
            best = jnp.where(better, gs, best)
            bidx = jnp.where(better, gidx, bidx)
    cand = []
    for i in range(EXP_PER_GRP):
        v = srow[i]
        for gidx in range(1, N_GRP):
            v = jnp.where(bidx == gidx, srow[4 * gidx + i], v)
        cand.append(v)

    def argmax4(vals):
        bw, bi = vals[0], jnp.zeros_like(bidx)
        for i in range(1, EXP_PER_GRP):
            better = vals[i] > bw
            bw = jnp.where(better, vals[i], bw)
            bi = jnp.where(better, i, bi)
        return bw, bi

    w1, i1 = argmax4(cand)
    w2, i2 = argmax4([jnp.where(i1 == i, -jnp.inf, cand[i]) for i in range(EXP_PER_GRP)])
    tot = w1 + w2
    e1 = bidx * EXP_PER_GRP + i1
    e2 = bidx * EXP_PER_GRP + i2
    ei_ref[0:1, :] = e1
    ei_ref[1:2, :] = e2
    ew_ref[0:1, :] = w1 / tot
    ew_ref[1:2, :] = w2 / tot

    @pl.when(pl.program_id(0) == 0)
    def _():
        cnt_scr[...] = jnp.zeros_like(cnt_scr)

    erow = lax.broadcasted_iota(i32, (N_EXP, MERGE_TM), 0)
    oh1 = erow == e1
    oh2 = erow == e2
    both = jnp.where(oh1 | oh2, 1.0, 0.0)
    incl = jnp.dot(both.astype(bf16), tri_ref[...], preferred_element_type=f32)
    cb_ref[0] = cnt_scr[...]
    base = cnt_scr[...] + (incl - both)
    pos_ref[0:1, :] = jnp.sum(jnp.where(oh1, base, 0.0), axis=0, keepdims=True).astype(i32)
    pos_ref[1:2, :] = jnp.sum(jnp.where(oh2, base, 0.0), axis=0, keepdims=True).astype(i32)
    cnt = cnt_scr[...] + incl[:, MERGE_TM - 1:MERGE_TM]
    cnt_scr[...] = cnt
    cnt_ref[...] = cnt


def _merge(bra, brb, brc, proj, x2d, wb, wo, n2, rw, rb):
    T = bra.shape[0]
    TM = MERGE_TM
    per_b = SEQ // TM
    gt = D_MODEL // LANE

    def gspec(n):
        return pl.BlockSpec((None, gt, TM, LANE), lambda i: (i // per_b, GATE_OFF // D_MODEL + n, i % per_b, 0))

    rspec = lambda w: pl.BlockSpec((TM, w), lambda i: (i, 0))
    full = lambda shape: pl.BlockSpec(shape, lambda i: (0,) * len(shape))
    tri = (jnp.arange(TM)[:, None] <= jnp.arange(TM)[None, :]).astype(bf16)
    return pl.pallas_call(
        _merge_body,
        grid=(T // TM,),
        in_specs=[
            rspec(BR_W), rspec(BR_W), rspec(BR_W),
            gspec(0), gspec(1), gspec(2),
            rspec(D_MODEL),
            full((N_BR, BR_W, D_MODEL)), full((D_MODEL, D_MODEL)), full((1, D_MODEL)),
            full((N_EXP, D_MODEL)), full((N_EXP, 1)), full((TM, TM)),
        ],
        out_specs=[
            rspec(D_MODEL), rspec(D_MODEL),
            pl.BlockSpec((TOP_K, TM), lambda i: (0, i)),
            pl.BlockSpec((TOP_K, TM), lambda i: (0, i)),
            pl.BlockSpec((TOP_K, TM), lambda i: (0, i)),
            full((N_EXP, 1)),
            pl.BlockSpec((1, N_EXP, 1), lambda i: (i, 0, 0)),
        ],
        out_shape=[
            jax.ShapeDtypeStruct((T, D_MODEL), f32),
            jax.ShapeDtypeStruct((T, D_MODEL), bf16),
            jax.ShapeDtypeStruct((TOP_K, T), i32),
            jax.ShapeDtypeStruct((TOP_K, T), f32),
            jax.ShapeDtypeStruct((TOP_K, T), i32),
            jax.ShapeDtypeStruct((N_EXP, 1), f32),
            jax.ShapeDtypeStruct((T // TM, N_EXP, 1), f32),
        ],
        scratch_shapes=[pltpu.VMEM((N_EXP, 1), f32)],
        compiler_params=_cparams(("arbitrary",)),
        name="merge_router",
    )(bra, brb, brc, proj, proj, proj, x2d, wb, wo, n2, rw, rb, tri)


def _expert_body(be_ref, nu_ref, xs_ref, wgu_ref, wd_ref, ys_ref):
    i = pl.program_id(0)

    @pl.when(i < nu_ref[0])
    def _():
        gu = jnp.dot(xs_ref[...], wgu_ref[0], preferred_element_type=f32)
        g = gu[:, :EXP_FF]
        act = (g * jax.nn.sigmoid(g)) * gu[:, EXP_FF:]
        ys_ref[...] = jnp.dot(act.astype(bf16), wd_ref[0], preferred_element_type=f32).astype(bf16)

    @pl.when(i >= nu_ref[0])
    def _():
        ys_ref[...] = jnp.zeros_like(ys_ref)


def _experts(block_e, n_used, xs, wgu, wd):
    n_rows = xs.shape[0]
    grid_spec = pltpu.PrefetchScalarGridSpec(
        num_scalar_prefetch=2,
        grid=(n_rows // MOE_BM,),
        in_specs=[
            pl.BlockSpec((MOE_BM, D_MODEL), lambda i, be, nu: (jnp.minimum(i, nu[0] - 1), 0)),
            pl.BlockSpec((1, D_MODEL, 2 * EXP_FF), lambda i, be, nu: (be[i], 0, 0)),
            pl.BlockSpec((1, EXP_FF, D_MODEL), lambda i, be, nu: (be[i], 0, 0)),
        ],
        out_specs=pl.BlockSpec((MOE_BM, D_MODEL), lambda i, be, nu: (i, 0)),
    )
    return pl.pallas_call(
        _expert_body,
        grid_spec=grid_spec,
        out_shape=jax.ShapeDtypeStruct((n_rows, D_MODEL), bf16),
        compiler_params=_cparams(("arbitrary",)),
        name="experts",
    )(block_e, n_used, xs, wgu, wd)


COMB_TM = 1024


def _combine_body(x1_ref, y0_ref, y1_ref, w_ref, o_ref):
    w = w_ref[...]
    o_ref[...] = x1_ref[...] + y0_ref[...].astype(f32) * w[:, 0:1] + y1_ref[...].astype(f32) * w[:, 1:2]


def _combine(x1, y0, y1, w):
    T = x1.shape[0]
    rspec = pl.BlockSpec((COMB_TM, D_MODEL), lambda i: (i, 0))
    return pl.pallas_call(
        _combine_body,
        grid=(T // COMB_TM,),
        in_specs=[rspec, rspec, rspec, pl.BlockSpec((COMB_TM, TOP_K), lambda i: (i, 0))],
        out_specs=rspec,
        out_shape=jax.ShapeDtypeStruct((T, D_MODEL), f32),
        compiler_params=_cparams(("parallel",)),
        name="moe_combine",
    )(x1, y0, y1, w)


DISP_ALIGN = 16
DISP_SLOTS = MERGE_TM * TOP_K + N_EXP * DISP_ALIGN
DISP_SIZES = tuple(DISP_ALIGN << s for s in range(5, -1, -1))
TAIL_SIZES = tuple(sz for sz in (MOE_BM // 2 >> s for s in range(12)) if sz >= DISP_ALIGN)


def _pieces(length, sizes, fn):
    done = jnp.int32(0)
    for sz in sizes:
        take = (length & sz) != 0

        @pl.when(take)
        def _(done=done, sz=sz):
            fn(done, sz)

        done = done + jnp.where(take, sz, 0)


def _dispatch_body(off_ref, n_ref, ls_ref, toff_ref, tlen_ref, h2_ref, ei_ref, pos_ref, adj_ref, xs_hbm,
                   srt, zbuf, sem, zsem):
    i = pl.program_id(0)
    last = pl.num_programs(0) - 1
    cur = i % 2
    TM = MERGE_TM

    def slab_copies(step, buf, start):
        for e in range(N_EXP):
            ls = ls_ref[step * N_EXP + e]
            off = off_ref[step * N_EXP + e]

            def piece(done, sz, ls=ls, off=off):
                cp = pltpu.make_async_copy(srt.at[buf, pl.ds(pl.multiple_of(ls + done, DISP_ALIGN), sz)],
                                           xs_hbm.at[pl.ds(pl.multiple_of(off + done, DISP_ALIGN), sz)], sem.at[buf])
                cp.start() if start else cp.wait()

            _pieces(n_ref[step * N_EXP + e], DISP_SIZES, piece)

    def tail_copies(start):
        for e in range(N_EXP):
            off = toff_ref[e]

            def piece(done, sz, off=off):
                cp = pltpu.make_async_copy(zbuf.at[pl.ds(0, sz)],
                                           xs_hbm.at[pl.ds(pl.multiple_of(off + done, DISP_ALIGN), sz)], zsem)
                cp.start() if start else cp.wait()

            _pieces(tlen_ref[e], TAIL_SIZES, piece)

        zrows = zbuf.shape[0]

        def block(c, carry):
            cp = pltpu.make_async_copy(zbuf, xs_hbm.at[pl.ds(pl.multiple_of(toff_ref[N_EXP] + c * zrows, zrows), zrows)],
                                       zsem)
            cp.start() if start else cp.wait()
            return carry

        lax.fori_loop(0, tlen_ref[N_EXP] // zrows, block, 0)

    @pl.when(i == 0)
    def _():
        zbuf[...] = jnp.zeros_like(zbuf)
        tail_copies(True)

    @pl.when(i >= 2)
    def _():
        slab_copies(i - 2, cur, False)

    erow = lax.broadcasted_iota(i32, (N_EXP, TM), 0)
    adj = adj_ref[0]

    def slot(k):
        sel = jnp.sum(jnp.where(erow == ei_ref[k:k + 1, :], adj, 0.0), axis=0, keepdims=True)
        return pos_ref[k:k + 1, :] + sel.astype(i32)

    srow = lax.broadcasted_iota(i32, (DISP_SLOTS, TM), 0)
    perm = jnp.where((srow == slot(0)) | (srow == slot(1)), 1.0, 0.0).astype(bf16)
    srt[cur] = jnp.dot(perm, h2_ref[...], preferred_element_type=f32).astype(bf16)
    slab_copies(i, cur, True)

    @pl.when(i == last)
    def _():
        slab_copies(i, cur, False)

        @pl.when(i >= 1)
        def _():
            slab_copies(i - 1, 1 - cur, False)

        tail_copies(False)


def _dispatch(off, n16, lstart, tail_off, tail_len, h2, eidx, pos, adj, n_rows):
    T = h2.shape[0]
    TM = MERGE_TM
    grid_spec = pltpu.PrefetchScalarGridSpec(
        num_scalar_prefetch=5,
        grid=(T // TM,),
        in_specs=[
            pl.BlockSpec((TM, D_MODEL), lambda i, *_: (i, 0)),
            pl.BlockSpec((TOP_K, TM), lambda i, *_: (0, i)),
            pl.BlockSpec((TOP_K, TM), lambda i, *_: (0, i)),
            pl.BlockSpec((1, N_EXP, 1), lambda i, *_: (i, 0, 0)),
        ],
        out_specs=pl.BlockSpec(memory_space=pl.ANY),
        scratch_shapes=[
            pltpu.VMEM((2, DISP_SLOTS, D_MODEL), bf16),
            pltpu.VMEM((TAIL_SIZES[0], D_MODEL), bf16),
            pltpu.SemaphoreType.DMA((2,)),
            pltpu.SemaphoreType.DMA(()),
        ],
    )
    return pl.pallas_call(
        _dispatch_body,
        grid_spec=grid_spec,
        out_shape=jax.ShapeDtypeStruct((n_rows, D_MODEL), bf16),
        compiler_params=_cparams(("arbitrary",)),
        name="moe_dispatch",
    )(off, n16, lstart, tail_off, tail_len, h2, eidx, pos, adj)


def _moe(x1, h2, eidx, ew, pos, cnt, cb, wgu, wd):
    T = x1.shape[0]
    TM = MERGE_TM
    n_tiles = T // TM
    ceil_to = lambda v, m: (v + m - 1) // m * m
    n_rows = ceil_to(T * TOP_K + n_tiles * N_EXP * (DISP_ALIGN - 1) + N_EXP * (MOE_BM - 1), MOE_BM)
    n_blocks = n_rows // MOE_BM
    cbi = cb[:, :, 0].astype(i32)
    tile_cnt = jnp.concatenate([cbi[1:], cnt[:, 0].astype(i32)[None, :]], axis=0) - cbi
    n16 = ceil_to(tile_cnt, DISP_ALIGN)
    lstart = jnp.cumsum(n16, axis=1) - n16
    before = jnp.cumsum(n16, axis=0) - n16
    used = jnp.sum(n16, axis=0)
    padded = ceil_to(used, MOE_BM)
    pad_ends = jnp.cumsum(padded)
    off = (pad_ends - padded)[None, :] + before
    adj = (lstart - cbi).astype(f32)[:, :, None]
    row_adj = jnp.repeat(off - cbi, TM, axis=0).T
    dest = pos
    for e in range(N_EXP):
        dest = dest + jnp.where(eidx == e, row_adj[e][None, :], 0)
    blk_start = jnp.arange(n_blocks, dtype=i32) * MOE_BM
    block_e = jnp.minimum(jnp.sum((blk_start[:, None] >= pad_ends[None, :]).astype(i32), axis=1), N_EXP - 1)
    n_used = (pad_ends[-1:] // MOE_BM).astype(i32)
    tail_off = jnp.concatenate([pad_ends - padded + used, pad_ends[-1:]])
    tail_len = jnp.concatenate([padded - used, n_rows - pad_ends[-1:]])
    xs = _dispatch(off.reshape(-1), n16.reshape(-1), lstart.reshape(-1), tail_off, tail_len, h2, eidx, pos, adj, n_rows)
    ys = _experts(block_e, n_used, xs, wgu, wd)
    return _combine(x1, ys[dest[0]], ys[dest[1]], ew.T)


def _pack_w_in(w):
    cuts = np.cumsum([256, 256, 512, 512, 16, 512, 512, 4608, 3072])[:-1].tolist()
    q, k, v, og, glow, xb, yb, qkv, gates = jnp.split(w, cuts, axis=-1)
    packed = jnp.concatenate([gates, qkv, xb, yb, q, k, v, og], axis=-1).astype(bf16)
    wg = jnp.pad(glow, ((0, 0), (0, LANE - GLA_RANK))).astype(bf16)
    return packed, wg


def _block_diag_lru(w_a, w_x):
    bs = LRU_W // LRU_BLOCKS
    per = LANE // bs
    tiles = []
    for t in range(LRU_W // LANE):
        halves = []
        for w in (w_a, w_x):
            m = jnp.zeros((LANE, LANE), f32)
            for j in range(per):
                m = m.at[j * bs:(j + 1) * bs, j * bs:(j + 1) * bs].set(w[t * per + j])
            halves.append(m)
        tiles.append(jnp.concatenate(halves, axis=1))
    return jnp.stack(tiles).astype(bf16)


def kernel(x, positions, norm1_gain, w_in, gla_gate_up, gla_gate_bias, gla_out_gain, lru_conv_w, lru_conv_b, lru_w_a, lru_b_a, lru_w_x, lru_b_x, lru_lambda, q_norm_gain, k_norm_gain, w_branch, w_out, norm2_gain, router_w, router_b, w_gate, w_up, w_down):
    B, S, D = x.shape
    assert S == SEQ and D == D_MODEL
    T = B * S
    depth = w_in.shape[0]

    half = ROPE_DIMS // 2
    inv_freq = ROPE_THETA ** (-jnp.arange(half, dtype=f32) / half)
    freq = jnp.zeros((1, LANE), f32).at[0, :ROPE_DIMS].set(jnp.concatenate([inv_freq, inv_freq]))
    cos, ss = _rope_tables(positions.astype(f32)[:, :, None], freq)

    rw = router_w.T.astype(bf16)
    rb = router_b.astype(f32)[:, None]
    for l in range(depth):
        w_packed, wg = _pack_w_in(w_in[l])
        wup = jnp.pad(gla_gate_up[l], ((0, LANE - GLA_RANK), (0, 0)))
        nt = LRU_W // LANE
        lru_w = _block_diag_lru(lru_w_a[l], lru_w_x[l])
        lru_b = jnp.concatenate([lru_b_a[l].reshape(nt, 1, LANE), lru_b_x[l].reshape(nt, 1, LANE)], axis=-1)
        wgu = jnp.concatenate([w_gate[l], w_up[l]], axis=-1).astype(bf16)

        proj, glow = _inproj(x, norm1_gain[l][None, :], w_packed, wg)
        br_a = _gla(proj, glow, wup, gla_gate_bias[l][None, :], gla_out_gain[l][None, :])
        br_b = _lru(proj, lru_conv_w[l], lru_conv_b[l][None, :], lru_w, lru_b, lru_lambda[l][None, :])
        br_c = _attn(proj, cos, ss, q_norm_gain[l], k_norm_gain[l])
        routed = _merge(br_a.reshape(T, BR_W), br_b.reshape(T, BR_W), br_c.reshape(T, BR_W), proj, x.reshape(T, D),
                        w_branch[l].astype(bf16), w_out[l].astype(bf16), norm2_gain[l][None, :], rw, rb)
        x = _moe(*routed, wgu, w_down[l].astype(bf16)).reshape(B, S, D)
    return x
```

```python
import jax
import jax.numpy as jnp
import numpy as np
from jax import lax
from jax.experimental import pallas as pl
from jax.experimental.pallas import tpu as pltpu

f32 = jnp.float32
bf16 = jnp.bfloat16
i32 = jnp.int32

D_MODEL = 1024
SEQ = 2048
RMS_EPS = 1e-6
GLA_HEADS, GLA_DK, GLA_DV, GLA_RANK, GLA_NORM, GLA_CHUNK = 4, 64, 128, 16, 16.0, 64
LRU_W, LRU_BLOCKS, LRU_CONV, LRU_C = 512, 8, 4, 8.0
DIL_PATTERNS = ((128, 1), (512, 4), (2048, 16))
N_PAT, DIL_HEADS, DIL_DH, ATTN_BLK = 3, 4, 128, 128
ROPE_THETA, ROPE_DIMS = 500000.0, 32
N_BR, BR_W = 3, 512
N_EXP, N_GRP, EXP_PER_GRP, TOP_K, EXP_FF = 16, 4, 4, 2, 512

LANE = 128
GATE_OFF, DIL_OFF, XB_OFF, YB_OFF, Q_OFF, K_OFF, V_OFF, OG_OFF = 0, 3072, 7680, 8192, 8704, 8960, 9216, 9728
PROJ_COLS = 10240
PROJ_TN = 2048
PROJ_RC = 512
MERGE_TM = 512
MOE_BM = 1024
VMEM_LIMIT = 56 * 1024 * 1024

NT = (((1,), (1,)), ((), ()))


def _cparams(sem):
    return pltpu.CompilerParams(dimension_semantics=sem, vmem_limit_bytes=VMEM_LIMIT)


def _inproj_body(x_ref, g_ref, w_ref, wg_ref, proj_ref, glow_ref, h_scr):
    j = pl.program_id(1)
    nrc = SEQ // PROJ_RC

    @pl.when(j == 0)
    def _():
        for c in range(nrc):
            rows = pl.ds(c * PROJ_RC, PROJ_RC)
            x = x_ref[0, rows, :]
            ms = jnp.mean(x * x, axis=-1, keepdims=True)
            h = ((x * lax.rsqrt(ms + RMS_EPS)) * g_ref[...]).astype(bf16)
            h_scr[rows, :] = h
            glow_ref[0, rows, :] = jnp.dot(h, wg_ref[...], preferred_element_type=f32)

    for c in range(nrc):
        rows = pl.ds(c * PROJ_RC, PROJ_RC)
        res = jnp.dot(h_scr[rows, :], w_ref[...], preferred_element_type=f32).astype(bf16)
        for t in range(PROJ_TN // LANE):
            proj_ref[0, t, rows, :] = res[:, t * LANE:(t + 1) * LANE]


def _inproj(x, gain, w, wg):
    B = x.shape[0]
    nj = PROJ_COLS // PROJ_TN
    return pl.pallas_call(
        _inproj_body,
        grid=(B, nj),
        in_specs=[
            pl.BlockSpec((1, SEQ, D_MODEL), lambda b, j: (b, 0, 0)),
            pl.BlockSpec((1, D_MODEL), lambda b, j: (0, 0)),
            pl.BlockSpec((D_MODEL, PROJ_TN), lambda b, j: (0, j)),
            pl.BlockSpec((D_MODEL, LANE), lambda b, j: (0, 0)),
        ],
        out_specs=[
            pl.BlockSpec((1, PROJ_TN // LANE, SEQ, LANE), lambda b, j: (b, j, 0, 0)),
            pl.BlockSpec((1, SEQ, LANE), lambda b, j: (b, 0, 0)),
        ],
        out_shape=[
            jax.ShapeDtypeStruct((B, PROJ_COLS // LANE, SEQ, LANE), bf16),
            jax.ShapeDtypeStruct((B, SEQ, LANE), f32),
        ],
        scratch_shapes=[pltpu.VMEM((SEQ, D_MODEL), bf16)],
        compiler_params=_cparams(("parallel", "arbitrary")),
        name="inproj",
    )(x, gain, w, wg)


def _gla_body(q_ref, k_ref, v_ref, og_ref, glow_ref, wh_ref, wl_ref, bg_ref, gain_ref, o_ref, la_scr, st_scr, stb_scr):
    C = GLA_CHUNK
    HK = GLA_HEADS * GLA_DK
    PR = 512
    for c in range(SEQ // PR):
        rows = pl.ds(c * PR, PR)
        g = glow_ref[0, rows, :]
        gh = g.astype(bf16)
        gl = (g - gh.astype(f32)).astype(bf16)
        z = (jnp.dot(gh, wh_ref[...], preferred_element_type=f32)
             + jnp.dot(gl, wh_ref[...], preferred_element_type=f32)
             + jnp.dot(gh, wl_ref[...], preferred_element_type=f32)) + bg_ref[...]
        la_scr[rows, :] = (jnp.minimum(z, 0.0) - jnp.log1p(jnp.exp(-jnp.abs(z)))) * (1.0 / GLA_NORM)
    st_scr[...] = jnp.zeros_like(st_scr)
    stb_scr[...] = jnp.zeros_like(stb_scr)

    causal = (lax.broadcasted_iota(i32, (C, GLA_HEADS * C), 0)
              >= lax.broadcasted_iota(i32, (C, GLA_HEADS * C), 1) % C)
    tri = (lax.broadcasted_iota(i32, (C, C), 0) >= lax.broadcasted_iota(i32, (C, C), 1)).astype(bf16)
    lane_head = lax.broadcasted_iota(i32, (1, HK), 1) // GLA_DK
    gain = gain_ref[...]
    zero_v = jnp.zeros((C, GLA_DV), bf16)

    def chunk(n, carry):
        rows = pl.ds(pl.multiple_of(n * C, C), C)
        la = la_scr[rows, :]
        p1 = la.astype(bf16)
        r1 = la - p1.astype(f32)
        p2 = r1.astype(bf16)
        p3 = (r1 - p2.astype(f32)).astype(bf16)
        c3 = jnp.dot(tri, jnp.concatenate([p1, p2, p3], axis=1), preferred_element_type=f32)
        cum = (c3[:, :HK] + c3[:, HK:2 * HK]) + c3[:, 2 * HK:]
        cum_t = cum.T
        tot_t = cum_t[:, C - 1:C]
        qf = jnp.concatenate([q_ref[0, t, rows, :] for t in range(HK // LANE)], axis=1).astype(f32)
        kf = jnp.concatenate([k_ref[0, t, rows, :] for t in range(HK // LANE)], axis=1).astype(f32)
        qd = (qf * (GLA_DK ** -0.5) * jnp.exp(cum)).astype(bf16)
        ki = kf * jnp.exp(-cum)
        kd_t = (kf.T * jnp.exp(tot_t - cum_t)).astype(bf16)
        dec_t = jnp.exp(tot_t)
        heads = range(GLA_HEADS)
        vcols = [v_ref[0, h, rows, :] for h in heads]
        k_st = jnp.concatenate([jnp.where(lane_head == h, ki, 0.0).astype(bf16) for h in heads], axis=0)
        v_bd = jnp.concatenate([jnp.concatenate([vcols[h] if g == h else zero_v for g in heads], axis=1)
                                for h in heads], axis=0)
        s = lax.dot_general(qd, k_st, NT, preferred_element_type=f32)
        s = jnp.where(causal, s, 0.0).astype(bf16)
        o = jnp.dot(s, v_bd, preferred_element_type=f32) + jnp.dot(qd, stb_scr[...], preferred_element_type=f32)
        for h in heads:
            cols = slice(h * GLA_DV, (h + 1) * GLA_DV)
            hr = slice(h * GLA_DK, (h + 1) * GLA_DK)
            st = st_scr[h] * dec_t[hr, :] + jnp.dot(kd_t[hr, :], vcols[h], preferred_element_type=f32)
            st_scr[h] = st
            stb_scr[hr, cols] = st.astype(bf16)
            oh = o[:, cols]
            oh = oh * lax.rsqrt(jnp.mean(oh * oh, axis=-1, keepdims=True) + RMS_EPS) * gain
            g = og_ref[0, h, rows, :].astype(f32)
            o_ref[0, rows, cols] = (oh * (g * jax.nn.sigmoid(g))).astype(bf16)
        return carry

    lax.fori_loop(0, SEQ // C, chunk, 0, unroll=4)


def _gla(proj, glow, wup, bg, gain):
    B = proj.shape[0]
    HK = GLA_HEADS * GLA_DK
    HV = GLA_HEADS * GLA_DV
    wup_hi = wup.astype(bf16)
    wup_lo = (wup - wup_hi.astype(f32)).astype(bf16)
    return pl.pallas_call(
        _gla_body,
        grid=(B,),
        in_specs=[
            pl.BlockSpec((1, HK // LANE, SEQ, LANE), lambda b: (b, Q_OFF // HK, 0, 0)),
            pl.BlockSpec((1, HK // LANE, SEQ, LANE), lambda b: (b, K_OFF // HK, 0, 0)),
            pl.BlockSpec((1, HV // LANE, SEQ, LANE), lambda b: (b, V_OFF // HV, 0, 0)),
            pl.BlockSpec((1, HV // LANE, SEQ, LANE), lambda b: (b, OG_OFF // HV, 0, 0)),
            pl.BlockSpec((1, SEQ, LANE), lambda b: (b, 0, 0)),
            pl.BlockSpec((LANE, HK), lambda b: (0, 0)),
            pl.BlockSpec((LANE, HK), lambda b: (0, 0)),
            pl.BlockSpec((1, HK), lambda b: (0, 0)),
            pl.BlockSpec((1, GLA_DV), lambda b: (0, 0)),
        ],
        out_specs=pl.BlockSpec((1, SEQ, HV), lambda b: (b, 0, 0)),
        out_shape=jax.ShapeDtypeStruct((B, SEQ, HV), bf16),
        scratch_shapes=[
            pltpu.VMEM((SEQ, HK), f32),
            pltpu.VMEM((GLA_HEADS, GLA_DK, GLA_DV), f32),
            pltpu.VMEM((HK, HV), bf16),
        ],
        compiler_params=_cparams(("parallel",)),
        name="gla",
    )(proj, proj, proj, proj, glow, wup_hi, wup_lo, bg, gain)


LRU_GRP = 8
LRU_RC = 512
LRU_PAD = 8
LRU_NT = LRU_W // LANE


def _lru_body(xb_ref, yb_ref, cw_ref, cb_ref, w_ref, b_ref, lam_ref, o_ref, xpad, a_scr, h_scr):
    grp = lax.broadcasted_iota(i32, (LRU_RC // LRU_GRP, LRU_GRP, LANE), 1)
    for t in range(LRU_NT):
        lanes = slice(t * LANE, (t + 1) * LANE)
        xpad[pl.ds(0, LRU_PAD), :] = jnp.zeros((LRU_PAD, LANE), f32)
        xpad[pl.ds(LRU_PAD, SEQ), :] = xb_ref[t].astype(f32)
        nl = -lam_ref[:, lanes]
        sp = jnp.maximum(nl, 0.0) + jnp.log1p(jnp.exp(-jnp.abs(nl)))
        for c in range(SEQ // LRU_RC):
            r0 = c * LRU_RC
            xc = cb_ref[:, lanes]
            for j in range(LRU_CONV):
                xc = xc + xpad[pl.ds(LRU_PAD - j + r0, LRU_RC), :] * cw_ref[LRU_CONV - 1 - j:LRU_CONV - j, lanes]
            z = jnp.dot(xc.astype(bf16), w_ref[t], preferred_element_type=f32) + b_ref[t]
            r = jax.nn.sigmoid(z[:, :LANE])
            gi = jax.nn.sigmoid(z[:, LANE:])
            log_a = (-LRU_C) * r * sp
            a = jnp.exp(log_a)
            u = jnp.sqrt(jnp.tanh(-log_a) * (a * a + 1.0)) * gi * xc
            a = a.reshape(LRU_RC // LRU_GRP, LRU_GRP, LANE)
            u = u.reshape(LRU_RC // LRU_GRP, LRU_GRP, LANE)
            k = 1
            while k < LRU_GRP:
                inside = grp >= k
                u = u + a * jnp.where(inside, pltpu.roll(u, k, 1), 0.0)
                a = a * jnp.where(inside, pltpu.roll(a, k, 1), 1.0)
                k *= 2
            a_scr[t, pl.ds(r0, LRU_RC), :] = a.reshape(LRU_RC, LANE)
            h_scr[t, pl.ds(r0, LRU_RC), :] = u.reshape(LRU_RC, LANE)

    def group(g, carry):
        rows = pl.ds(pl.multiple_of(g * LRU_GRP, LRU_GRP), LRU_GRP)
        out = []
        for t in range(LRU_NT):
            h = h_scr[t, rows, :] + a_scr[t, rows, :] * jnp.broadcast_to(carry[t], (LRU_GRP, LANE))
            h_scr[t, rows, :] = h
            out.append(h[LRU_GRP - 1:LRU_GRP, :])
        return tuple(out)

    lax.fori_loop(0, SEQ // LRU_GRP, group, tuple(jnp.zeros((1, LANE), f32) for _ in range(LRU_NT)), unroll=8)
    for t in range(LRU_NT):
        for c in range(SEQ // LRU_RC):
            rows = pl.ds(c * LRU_RC, LRU_RC)
            o_ref[0, rows, t * LANE:(t + 1) * LANE] = (h_scr[t, rows, :]
                                                       * jax.nn.gelu(yb_ref[t, rows, :].astype(f32))).astype(bf16)


def _lru(proj, cw, cb, w, b, lam):
    B = proj.shape[0]
    full = lambda shape: pl.BlockSpec(shape, lambda b: (0,) * len(shape))
    return pl.pallas_call(
        _lru_body,
        grid=(B,),
        in_specs=[
            pl.BlockSpec((None, LRU_NT, SEQ, LANE), lambda b: (b, XB_OFF // LRU_W, 0, 0)),
            pl.BlockSpec((None, LRU_NT, SEQ, LANE), lambda b: (b, YB_OFF // LRU_W, 0, 0)),
            full((LRU_CONV, LRU_W)), full((1, LRU_W)), full((LRU_NT, LANE, 2 * LANE)), full((LRU_NT, 1, 2 * LANE)),
            full((1, LRU_W)),
        ],
        out_specs=pl.BlockSpec((1, SEQ, LRU_W), lambda b: (b, 0, 0)),
        out_shape=jax.ShapeDtypeStruct((B, SEQ, LRU_W), bf16),
        scratch_shapes=[
            pltpu.VMEM((SEQ + LRU_PAD, LANE), f32),
            pltpu.VMEM((LRU_NT, SEQ, LANE), f32),
            pltpu.VMEM((LRU_NT, SEQ, LANE), f32),
        ],
        compiler_params=_cparams(("parallel",)),
        name="lru",
    )(proj, proj, cw, cb, w, b, lam)


def _rope_body(pos_ref, freq_ref, cos_ref, ss_ref):
    ang = pos_ref[0] * freq_ref[...]
    lane = lax.broadcasted_iota(i32, (SEQ, LANE), 1)
    sn = jnp.sin(ang)
    cos_ref[0] = jnp.cos(ang)
    ss_ref[0] = jnp.where(lane < ROPE_DIMS // 2, -sn, sn)


def _rope_tables(posf, freq):
    B = posf.shape[0]
    spec = pl.BlockSpec((1, SEQ, LANE), lambda b: (b, 0, 0))
    return pl.pallas_call(
        _rope_body,
        grid=(B,),
        in_specs=[pl.BlockSpec((1, SEQ, 1), lambda b: (b, 0, 0)), pl.BlockSpec((1, LANE), lambda b: (0, 0))],
        out_specs=[spec, spec],
        out_shape=[jax.ShapeDtypeStruct((B, SEQ, LANE), f32)] * 2,
        compiler_params=_cparams(("parallel",)),
        name="rope_tables",
    )(posf, freq)


def _rope_partner():
    half = ROPE_DIMS // 2
    perm = np.arange(LANE)
    perm[:half] += half
    perm[half:ROPE_DIMS] -= half
    return perm


ATTN_PC = 256
ATTN_G = 8


def _attn_body(q0, q1, q2, k0, k1, k2, v0, v1, v2, cos_ref, ss_ref, g_ref, p2_ref, o_ref,
               qd, kd, vd, tq, tk, tv, od, ld, on, ln, sc_a, sc_b):
    q_refs, k_refs, v_refs = (q0, q1, q2), (k0, k1, k2), (v0, v1, v2)
    BLK = ATTN_BLK
    qg = g_ref[0:1, :] * (DIL_DH ** -0.5)
    qgr = g_ref[1:2, :] * (DIL_DH ** -0.5)
    kg = g_ref[2:3, :]
    kgr = g_ref[3:4, :]
    kd[pl.ds(0, BLK), :] = jnp.zeros((BLK, LANE), bf16)
    vd[pl.ds(0, BLK), pl.ds(0, LANE)] = jnp.zeros((BLK, LANE), bf16)
    vd[:, pl.ds(LANE, LANE)] = jnp.ones((SEQ + BLK, LANE), bf16)

    qi = lax.broadcasted_iota(i32, (BLK, 2 * BLK), 0)
    kj = lax.broadcasted_iota(i32, (BLK, 2 * BLK), 1)
    is_cur = kj >= BLK
    band = jnp.where(is_cur, kj - BLK, qi) <= jnp.where(is_cur, qi, kj)

    def norm_rope(raw, partner, gain, gain_p, cs, ss):
        rf = raw.astype(f32)
        r = lax.rsqrt(jnp.mean(rf * rf, axis=-1, keepdims=True) + RMS_EPS)
        return (rf * (gain * cs) + partner * (gain_p * ss)) * r

    for p, (_, dil) in enumerate(DIL_PATTERNS):
        L = SEQ // dil
        nbl = L // BLK

        def prep(c, carry, p=p, dil=dil):
            rows = pl.ds(pl.multiple_of(c * ATTN_PC, ATTN_PC), ATTN_PC)
            cs, ss = cos_ref[0, rows, :], ss_ref[0, rows, :]
            qr = q_refs[p][0, rows, :]
            kr = k_refs[p][0, rows, :]
            partner = jnp.dot(jnp.concatenate([qr, kr], axis=1), p2_ref[...], preferred_element_type=f32)
            q = norm_rope(qr, partner[:, :LANE], qg, qgr, cs, ss)
            k = norm_rope(kr, partner[:, LANE:], kg, kgr, cs, ss)
            if dil == 1:
                orow = pl.ds(pl.multiple_of(c * ATTN_PC, ATTN_PC) + BLK, ATTN_PC)
                qd[rows, :] = q.astype(bf16)
                kd[orow, :] = k.astype(bf16)
                vd[orow, pl.ds(0, LANE)] = v_refs[p][0, rows, :]
            else:
                tq[rows, :] = q
                tk[rows, :] = k
                tv[rows, :] = v_refs[p][0, rows, :].astype(f32)
            return carry

        lax.fori_loop(0, SEQ // ATTN_PC, prep, 0, unroll=2)
        if dil > 1:
            for r in range(dil):
                src = pl.ds(r, L, stride=dil)
                qd[pl.ds(r * L, L), :] = tq[src, :].astype(bf16)
                kd[pl.ds(BLK + r * L, L), :] = tk[src, :].astype(bf16)
                vd[pl.ds(BLK + r * L, L), pl.ds(0, LANE)] = tv[src, :].astype(bf16)

        o_dst, l_dst = (on.at[p], ln.at[p]) if dil == 1 else (od, ld)

        def scores(g, dst):
            for j in range(ATTN_G):
                r0 = pl.multiple_of((g * ATTN_G + j) * BLK, BLK)
                dst[pl.ds(j * BLK, BLK), :] = lax.dot_general(qd[pl.ds(r0, BLK), :], kd[pl.ds(r0, 2 * BLK), :], NT,
                                                              preferred_element_type=f32)

        def finish(g, src, nbl=nbl, o_dst=o_dst, l_dst=l_dst):
            for j in range(ATTN_G):
                b = g * ATTN_G + j
                r0 = pl.multiple_of(b * BLK, BLK)
                first = (b % nbl) == 0
                valid = band & (is_cur | jnp.logical_not(first))
                s = jnp.where(valid, src[pl.ds(j * BLK, BLK), :], -jnp.inf)
                m = jnp.max(s, axis=-1, keepdims=True)
                e = jnp.exp(s - m).astype(bf16)
                ov = jnp.dot(e, vd[pl.ds(r0, 2 * BLK), :], preferred_element_type=f32)
                den = ov[:, LANE:]
                o_dst[pl.ds(r0, BLK), :] = ov[:, :LANE] / den
                l_dst[pl.ds(r0, BLK), :] = m + jnp.log(den)

        n_groups = SEQ // BLK // ATTN_G
        scores(0, sc_a)

        def pair(i, carry, scores=scores, finish=finish):
            scores(2 * i + 1, sc_b)
            finish(2 * i, sc_a)
            if n_groups > 2:
                scores(jnp.minimum(2 * i + 2, n_groups - 1), sc_a)
            finish(2 * i + 1, sc_b)
            return carry

        lax.fori_loop(0, n_groups // 2, pair, 0)
        if dil > 1:
            for r in range(dil):
                dst = pl.ds(r, L, stride=dil)
                on[p, dst, :] = od[pl.ds(r * L, L), :]
                ln[p, dst, :] = ld[pl.ds(r * L, L), :]

    def combine(c, carry):
        rows = pl.ds(pl.multiple_of(c * ATTN_PC, ATTN_PC), ATTN_PC)
        l0, l1, l2 = ln[0, rows, :], ln[1, rows, :], ln[2, rows, :]
        mx = jnp.maximum(jnp.maximum(l0, l1), l2)
        w0, w1, w2 = jnp.exp(l0 - mx), jnp.exp(l1 - mx), jnp.exp(l2 - mx)
        o = (w0 * on[0, rows, :] + w1 * on[1, rows, :] + w2 * on[2, rows, :]) / (w0 + w1 + w2)
        o_ref[0, rows, :] = o.astype(bf16)
        return carry

    lax.fori_loop(0, SEQ // ATTN_PC, combine, 0)


def _attn(proj, cos, ss, qg, kg):
    B = proj.shape[0]
    base = DIL_OFF // LANE
    perm = _rope_partner()
    gains = jnp.stack([qg, qg[perm], kg, kg[perm]]).astype(f32)
    pm = np.zeros((LANE, LANE), np.float32)
    pm[perm[:ROPE_DIMS], np.arange(ROPE_DIMS)] = 1.0
    p2 = jnp.asarray(np.kron(np.eye(2, dtype=np.float32), pm), dtype=bf16)

    def pspec(qkv, p):
        return pl.BlockSpec((None, 1, SEQ, LANE), lambda b, h, o=base + (qkv * N_PAT + p) * DIL_HEADS: (b, o + h, 0, 0))

    tspec = pl.BlockSpec((1, SEQ, LANE), lambda b, h: (b, 0, 0))
    in_specs = [pspec(qkv, p) for qkv in range(3) for p in range(N_PAT)] + [
        tspec, tspec, pl.BlockSpec((4, LANE), lambda b, h: (0, 0)), pl.BlockSpec((2 * LANE, 2 * LANE), lambda b, h: (0, 0))]
    return pl.pallas_call(
        _attn_body,
        grid=(B, DIL_HEADS),
        in_specs=in_specs,
        out_specs=pl.BlockSpec((1, SEQ, LANE), lambda b, h: (b, 0, h)),
        out_shape=jax.ShapeDtypeStruct((B, SEQ, DIL_HEADS * DIL_DH), bf16),
        scratch_shapes=[
            pltpu.VMEM((SEQ, LANE), bf16),
            pltpu.VMEM((SEQ + ATTN_BLK, LANE), bf16),
            pltpu.VMEM((SEQ + ATTN_BLK, 2 * LANE), bf16),
            pltpu.VMEM((SEQ, LANE), f32),
            pltpu.VMEM((SEQ, LANE), f32),
            pltpu.VMEM((SEQ, LANE), f32),
            pltpu.VMEM((SEQ, LANE), f32),
            pltpu.VMEM((SEQ, LANE), f32),
            pltpu.VMEM((N_PAT, SEQ, LANE), f32),
            pltpu.VMEM((N_PAT, SEQ, LANE), f32),
            pltpu.VMEM((ATTN_G * ATTN_BLK, 2 * ATTN_BLK), f32),
            pltpu.VMEM((ATTN_G * ATTN_BLK, 2 * ATTN_BLK), f32),
        ],
        compiler_params=_cparams(("parallel", "arbitrary")),
        name="dilated_attn",
    )(*([proj] * 9), cos, ss, gains, p2)


def _merge_body(a_ref, b_ref, c_ref, g0_ref, g1_ref, g2_ref, x_ref, wb_ref, wo_ref, n2_ref, rw_ref, rb_ref, tri_ref,
                x1_ref, h2_ref, ei_ref, ew_ref, pos_ref, cnt_ref, cb_ref, cnt_scr):
    merged = None
    for n, (br, g) in enumerate(((a_ref, g0_ref), (b_ref, g1_ref), (c_ref, g2_ref))):
        logits = jnp.concatenate([g[t] for t in range(D_MODEL // LANE)], axis=1)
        gate = 0.5 * jnp.tanh(0.5 * logits.astype(f32)) + 0.5
        t = gate * jnp.dot(br[...], wb_ref[n], preferred_element_type=f32)
        merged = t if merged is None else merged + t
    x1 = x_ref[...] + jnp.dot(merged.astype(bf16), wo_ref[...], preferred_element_type=f32)
    x1_ref[...] = x1
    h2 = ((x1 * lax.rsqrt(jnp.mean(x1 * x1, axis=-1, keepdims=True) + RMS_EPS)) * n2_ref[...]).astype(bf16)
    h2_ref[...] = h2

    lg = lax.dot_general(rw_ref[...], h2, NT, preferred_element_type=f32) + rb_ref[...]
    ex = jnp.exp(lg - jnp.max(lg, axis=0, keepdims=True))
    sc = ex / jnp.sum(ex, axis=0, keepdims=True)
    srow = [sc[e:e + 1, :] for e in range(N_EXP)]
    best = bidx = None
    for gidx in range(N_GRP):
        s0, s1, s2, s3 = srow[4 * gidx:4 * gidx + 4]
        hi01, lo01, hi23, lo23 = jnp.maximum(s0, s1), jnp.minimum(s0, s1), jnp.maximum(s2, s3), jnp.minimum(s2, s3)
        gs = jnp.maximum(hi01, hi23) + jnp.maximum(jnp.minimum(hi01, hi23), jnp.maximum(lo01, lo23))
        if best is None:
            best, bidx = gs, jnp.zeros_like(gs, dtype=i32)
        else:
            better = gs > best
            best = jnp.where(better, gs, best)
            bidx = jnp.where(better, gidx, bidx)
    cand = []
    for i in range(EXP_PER_GRP):
        v = srow[i]
        for gidx in range(1, N_GRP):
            v = jnp.where(bidx == gidx, srow[4 * gidx + i], v)
        cand.append(v)

    def argmax4(vals):
        bw, bi = vals[0], jnp.zeros_like(bidx)
        for i in range(1, EXP_PER_GRP):
            better = vals[i] > bw
            bw = jnp.where(better, vals[i], bw)
            bi = jnp.where(better, i, bi)
        return bw, bi

    w1, i1 = argmax4(cand)
    w2, i2 = argmax4([jnp.where(i1 == i, -jnp.inf, cand[i]) for i in range(EXP_PER_GRP)])
    tot = w1 + w2
    e1 = bidx * EXP_PER_GRP + i1
    e2 = bidx * EXP_PER_GRP + i2
    ei_ref[0:1, :] = e1
    ei_ref[1:2, :] = e2
    ew_ref[0:1, :] = w1 / tot
    ew_ref[1:2, :] = w2 / tot

    @pl.when(pl.program_id(0) == 0)
    def _():
        cnt_scr[...] = jnp.zeros_like(cnt_scr)

    erow = lax.broadcasted_iota(i32, (N_EXP, MERGE_TM), 0)
    oh1 = erow == e1
    oh2 = erow == e2
    both = jnp.where(oh1 | oh2, 1.0, 0.0)
    incl = jnp.dot(both.astype(bf16), tri_ref[...], preferred_element_type=f32)
    cb_ref[0] = cnt_scr[...]
    base = cnt_scr[...] + (incl - both)
    pos_ref[0:1, :] = jnp.sum(jnp.where(oh1, base, 0.0), axis=0, keepdims=True).astype(i32)
    pos_ref[1:2, :] = jnp.sum(jnp.where(oh2, base, 0.0), axis=0, keepdims=True).astype(i32)
    cnt = cnt_scr[...] + incl[:, MERGE_TM - 1:MERGE_TM]
    cnt_scr[...] = cnt
    cnt_ref[...] = cnt


def _merge(bra, brb, brc, proj, x2d, wb, wo, n2, rw, rb):
    T = bra.shape[0]
    TM = MERGE_TM
    per_b = SEQ // TM
    gt = D_MODEL // LANE

    def gspec(n):
        return pl.BlockSpec((None, gt, TM, LANE), lambda i: (i // per_b, GATE_OFF // D_MODEL + n, i % per_b, 0))

    rspec = lambda w: pl.BlockSpec((TM, w), lambda i: (i, 0))
    full = lambda shape: pl.BlockSpec(shape, lambda i: (0,) * len(shape))
    tri = (jnp.arange(TM)[:, None] <= jnp.arange(TM)[None, :]).astype(bf16)
    return pl.pallas_call(
        _merge_body,
        grid=(T // TM,),
        in_specs=[
            rspec(BR_W), rspec(BR_W), rspec(BR_W),
            gspec(0), gspec(1), gspec(2),
            rspec(D_MODEL),
            full((N_BR, BR_W, D_MODEL)), full((D_MODEL, D_MODEL)), full((1, D_MODEL)),
            full((N_EXP, D_MODEL)), full((N_EXP, 1)), full((TM, TM)),
        ],
        out_specs=[
            rspec(D_MODEL), rspec(D_MODEL),
            pl.BlockSpec((TOP_K, TM), lambda i: (0, i)),
            pl.BlockSpec((TOP_K, TM), lambda i: (0, i)),
            pl.BlockSpec((TOP_K, TM), lambda i: (0, i)),
            full((N_EXP, 1)),
            pl.BlockSpec((1, N_EXP, 1), lambda i: (i, 0, 0)),
        ],
        out_shape=[
            jax.ShapeDtypeStruct((T, D_MODEL), f32),
            jax.ShapeDtypeStruct((T, D_MODEL), bf16),
            jax.ShapeDtypeStruct((TOP_K, T), i32),
            jax.ShapeDtypeStruct((TOP_K, T), f32),
            jax.ShapeDtypeStruct((TOP_K, T), i32),
            jax.ShapeDtypeStruct((N_EXP, 1), f32),
            jax.ShapeDtypeStruct((T // TM, N_EXP, 1), f32),
        ],
        scratch_shapes=[pltpu.VMEM((N_EXP, 1), f32)],
        compiler_params=_cparams(("arbitrary",)),
        name="merge_router",
    )(bra, brb, brc, proj, proj, proj, x2d, wb, wo, n2, rw, rb, tri)


def _expert_body(be_ref, nu_ref, xs_ref, wgu_ref, wd_ref, ys_ref):
    i = pl.program_id(0)

    @pl.when(i < nu_ref[0])
    def _():
        gu = jnp.dot(xs_ref[...], wgu_ref[0], preferred_element_type=f32)
        g = gu[:, :EXP_FF]
        act = (g * jax.nn.sigmoid(g)) * gu[:, EXP_FF:]
        ys_ref[...] = jnp.dot(act.astype(bf16), wd_ref[0], preferred_element_type=f32).astype(bf16)

    @pl.when(i >= nu_ref[0])
    def _():
        ys_ref[...] = jnp.zeros_like(ys_ref)


def _experts(block_e, n_used, xs, wgu, wd):
    n_rows = xs.shape[0]
    grid_spec = pltpu.PrefetchScalarGridSpec(
        num_scalar_prefetch=2,
        grid=(n_rows // MOE_BM,),
        in_specs=[
            pl.BlockSpec((MOE_BM, D_MODEL), lambda i, be, nu: (jnp.minimum(i, nu[0] - 1), 0)),
            pl.BlockSpec((1, D_MODEL, 2 * EXP_FF), lambda i, be, nu: (be[i], 0, 0)),
            pl.BlockSpec((1, EXP_FF, D_MODEL), lambda i, be, nu: (be[i], 0, 0)),
        ],
        out_specs=pl.BlockSpec((MOE_BM, D_MODEL), lambda i, be, nu: (i, 0)),
    )
    return pl.pallas_call(
        _expert_body,
        grid_spec=grid_spec,
        out_shape=jax.ShapeDtypeStruct((n_rows, D_MODEL), bf16),
        compiler_params=_cparams(("arbitrary",)),
        name="experts",
    )(block_e, n_used, xs, wgu, wd)


COMB_TM = 1024


def _combine_body(x1_ref, y0_ref, y1_ref, w_ref, o_ref):
    w = w_ref[...]
    o_ref[...] = x1_ref[...] + y0_ref[...].astype(f32) * w[:, 0:1] + y1_ref[...].astype(f32) * w[:, 1:2]


def _combine(x1, y0, y1, w):
    T = x1.shape[0]
    rspec = pl.BlockSpec((COMB_TM, D_MODEL), lambda i: (i, 0))
    return pl.pallas_call(
        _combine_body,
        grid=(T // COMB_TM,),
        in_specs=[rspec, rspec, rspec, pl.BlockSpec((COMB_TM, TOP_K), lambda i: (i, 0))],
        out_specs=rspec,
        out_shape=jax.ShapeDtypeStruct((T, D_MODEL), f32),
        compiler_params=_cparams(("parallel",)),
        name="moe_combine",
    )(x1, y0, y1, w)


DISP_ALIGN = 16
DISP_SLOTS = MERGE_TM * TOP_K + N_EXP * DISP_ALIGN
DISP_SIZES = tuple(DISP_ALIGN << s for s in range(5, -1, -1))
TAIL_SIZES = tuple(sz for sz in (MOE_BM // 2 >> s for s in range(12)) if sz >= DISP_ALIGN)


def _pieces(length, sizes, fn):
    done = jnp.int32(0)
    for sz in sizes:
        take = (length & sz) != 0

        @pl.when(take)
        def _(done=done, sz=sz):
            fn(done, sz)

        done = done + jnp.where(take, sz, 0)


def _dispatch_body(off_ref, n_ref, ls_ref, toff_ref, tlen_ref, h2_ref, ei_ref, pos_ref, adj_ref, xs_hbm,
                   srt, zbuf, sem, zsem):
    i = pl.program_id(0)
    last = pl.num_programs(0) - 1
    cur = i % 2
    TM = MERGE_TM

    def slab_copies(step, buf, start):
        for e in range(N_EXP):
            ls = ls_ref[step * N_EXP + e]
            off = off_ref[step * N_EXP + e]

            def piece(done, sz, ls=ls, off=off):
                cp = pltpu.make_async_copy(srt.at[buf, pl.ds(pl.multiple_of(ls + done, DISP_ALIGN), sz)],
                                           xs_hbm.at[pl.ds(pl.multiple_of(off + done, DISP_ALIGN), sz)], sem.at[buf])
                cp.start() if start else cp.wait()

            _pieces(n_ref[step * N_EXP + e], DISP_SIZES, piece)

    def tail_copies(start):
        for e in range(N_EXP):
            off = toff_ref[e]

            def piece(done, sz, off=off):
                cp = pltpu.make_async_copy(zbuf.at[pl.ds(0, sz)],
                                           xs_hbm.at[pl.ds(pl.multiple_of(off + done, DISP_ALIGN), sz)], zsem)
                cp.start() if start else cp.wait()

            _pieces(tlen_ref[e], TAIL_SIZES, piece)

        zrows = zbuf.shape[0]

        def block(c, carry):
            cp = pltpu.make_async_copy(zbuf, xs_hbm.at[pl.ds(pl.multiple_of(toff_ref[N_EXP] + c * zrows, zrows), zrows)],
                                       zsem)
            cp.start() if start else cp.wait()
            return carry

        lax.fori_loop(0, tlen_ref[N_EXP] // zrows, block, 0)

    @pl.when(i == 0)
    def _():
        zbuf[...] = jnp.zeros_like(zbuf)
        tail_copies(True)

    @pl.when(i >= 2)
    def _():
        slab_copies(i - 2, cur, False)

    erow = lax.broadcasted_iota(i32, (N_EXP, TM), 0)
    adj = adj_ref[0]

    def slot(k):
        sel = jnp.sum(jnp.where(erow == ei_ref[k:k + 1, :], adj, 0.0), axis=0, keepdims=True)
        return pos_ref[k:k + 1, :] + sel.astype(i32)

    srow = lax.broadcasted_iota(i32, (DISP_SLOTS, TM), 0)
    perm = jnp.where((srow == slot(0)) | (srow == slot(1)), 1.0, 0.0).astype(bf16)
    srt[cur] = jnp.dot(perm, h2_ref[...], preferred_element_type=f32).astype(bf16)
    slab_copies(i, cur, True)

    @pl.when(i == last)
    def _():
        slab_copies(i, cur, False)

        @pl.when(i >= 1)
        def _():
            slab_copies(i - 1, 1 - cur, False)

        tail_copies(False)


def _dispatch(off, n16, lstart, tail_off, tail_len, h2, eidx, pos, adj, n_rows):
    T = h2.shape[0]
    TM = MERGE_TM
    grid_spec = pltpu.PrefetchScalarGridSpec(
        num_scalar_prefetch=5,
        grid=(T // TM,),
        in_specs=[
            pl.BlockSpec((TM, D_MODEL), lambda i, *_: (i, 0)),
            pl.BlockSpec((TOP_K, TM), lambda i, *_: (0, i)),
            pl.BlockSpec((TOP_K, TM), lambda i, *_: (0, i)),
            pl.BlockSpec((1, N_EXP, 1), lambda i, *_: (i, 0, 0)),
        ],
        out_specs=pl.BlockSpec(memory_space=pl.ANY),
        scratch_shapes=[
            pltpu.VMEM((2, DISP_SLOTS, D_MODEL), bf16),
            pltpu.VMEM((TAIL_SIZES[0], D_MODEL), bf16),
            pltpu.SemaphoreType.DMA((2,)),
            pltpu.SemaphoreType.DMA(()),
        ],
    )
    return pl.pallas_call(
        _dispatch_body,
        grid_spec=grid_spec,
        out_shape=jax.ShapeDtypeStruct((n_rows, D_MODEL), bf16),
        compiler_params=_cparams(("arbitrary",)),
        name="moe_dispatch",
    )(off, n16, lstart, tail_off, tail_len, h2, eidx, pos, adj)


def _moe(x1, h2, eidx, ew, pos, cnt, cb, wgu, wd):
    T = x1.shape[0]
    TM = MERGE_TM
    n_tiles = T // TM
    ceil_to = lambda v, m: (v + m - 1) // m * m
    n_rows = ceil_to(T * TOP_K + n_tiles * N_EXP * (DISP_ALIGN - 1) + N_EXP * (MOE_BM - 1), MOE_BM)
    n_blocks = n_rows // MOE_BM
    cbi = cb[:, :, 0].astype(i32)
    tile_cnt = jnp.concatenate([cbi[1:], cnt[:, 0].astype(i32)[None, :]], axis=0) - cbi
    n16 = ceil_to(tile_cnt, DISP_ALIGN)
    lstart = jnp.cumsum(n16, axis=1) - n16
    before = jnp.cumsum(n16, axis=0) - n16
    used = jnp.sum(n16, axis=0)
    padded = ceil_to(used, MOE_BM)
    pad_ends = jnp.cumsum(padded)
    off = (pad_ends - padded)[None, :] + before
    adj = (lstart - cbi).astype(f32)[:, :, None]
    row_adj = jnp.repeat(off - cbi, TM, axis=0).T
    dest = pos
    for e in range(N_EXP):
        dest = dest + jnp.where(eidx == e, row_adj[e][None, :], 0)
    blk_start = jnp.arange(n_blocks, dtype=i32) * MOE_BM
    block_e = jnp.minimum(jnp.sum((blk_start[:, None] >= pad_ends[None, :]).astype(i32), axis=1), N_EXP - 1)
    n_used = (pad_ends[-1:] // MOE_BM).astype(i32)
    tail_off = jnp.concatenate([pad_ends - padded + used, pad_ends[-1:]])
    tail_len = jnp.concatenate([padded - used, n_rows - pad_ends[-1:]])
    xs = _dispatch(off.reshape(-1), n16.reshape(-1), lstart.reshape(-1), tail_off, tail_len, h2, eidx, pos, adj, n_rows)
    ys = _experts(block_e, n_used, xs, wgu, wd)
    return _combine(x1, ys[dest[0]], ys[dest[1]], ew.T)


def _pack_w_in(w):
    cuts = np.cumsum([256, 256, 512, 512, 16, 512, 512, 4608, 3072])[:-1].tolist()
    q, k, v, og, glow, xb, yb, qkv, gates = jnp.split(w, cuts, axis=-1)
    packed = jnp.concatenate([gates, qkv, xb, yb, q, k, v, og], axis=-1).astype(bf16)
    wg = jnp.pad(glow, ((0, 0), (0, LANE - GLA_RANK))).astype(bf16)
    return packed, wg


def _block_diag_lru(w_a, w_x):
    bs = LRU_W // LRU_BLOCKS
    per = LANE // bs
    tiles = []
    for t in range(LRU_W // LANE):
        halves = []
        for w in (w_a, w_x):
            m = jnp.zeros((LANE, LANE), f32)
            for j in range(per):
                m = m.at[j * bs:(j + 1) * bs, j * bs:(j + 1) * bs].set(w[t * per + j])
            halves.append(m)
        tiles.append(jnp.concatenate(halves, axis=1))
    return jnp.stack(tiles).astype(bf16)


def kernel(x, positions, norm1_gain, w_in, gla_gate_up, gla_gate_bias, gla_out_gain, lru_conv_w, lru_conv_b, lru_w_a, lru_b_a, lru_w_x, lru_b_x, lru_lambda, q_norm_gain, k_norm_gain, w_branch, w_out, norm2_gain, router_w, router_b, w_gate, w_up, w_down):
    B, S, D = x.shape
    assert S == SEQ and D == D_MODEL
    T = B * S
    depth = w_in.shape[0]

    half = ROPE_DIMS // 2
    inv_freq = ROPE_THETA ** (-jnp.arange(half, dtype=f32) / half)
    freq = jnp.zeros((1, LANE), f32).at[0, :ROPE_DIMS].set(jnp.concatenate([inv_freq, inv_freq]))
    cos, ss = _rope_tables(positions.astype(f32)[:, :, None], freq)

    rw = router_w.T.astype(bf16)
    rb = router_b.astype(f32)[:, None]
    for l in range(depth):
        w_packed, wg = _pack_w_in(w_in[l])
        wup = jnp.pad(gla_gate_up[l], ((0, LANE - GLA_RANK), (0, 0)))
        nt = LRU_W // LANE
        lru_w = _block_diag_lru(lru_w_a[l], lru_w_x[l])
        lru_b = jnp.concatenate([lru_b_a[l].reshape(nt, 1, LANE), lru_b_x[l].reshape(nt, 1, LANE)], axis=-1)
        wgu = jnp.concatenate([w_gate[l], w_up[l]], axis=-1).astype(bf16)

        proj, glow = _inproj(x, norm1_gain[l][None, :], w_packed, wg)
        br_a = _gla(proj, glow, wup, gla_gate_bias[l][None, :], gla_out_gain[l][None, :])
        br_b = _lru(proj, lru_conv_w[l], lru_conv_b[l][None, :], lru_w, lru_b, lru_lambda[l][None, :])
        br_c = _attn(proj, cos, ss, q_norm_gain[l], k_norm_gain[l])
        routed = _merge(br_a.reshape(T, BR_W), br_b.reshape(T, BR_W), br_c.reshape(T, BR_W), proj, x.reshape(T, D),
                        w_branch[l].astype(bf16), w_out[l].astype(bf16), norm2_gain[l][None, :], rw, rb)
        x = _moe(*routed, wgu, w_down[l].astype(bf16)).reshape(B, S, D)
    return x
```

```python
import jax
import jax.numpy as jnp
import numpy as np
from jax import lax
from jax.experimental import pallas as pl
from jax.experimental.pallas import tpu as pltpu

f32 = jnp.float32
bf16 = jnp.bfloat16
i32 = jnp.int32

D_MODEL = 1024
SEQ = 2048
RMS_EPS = 1e-6
GLA_HEADS, GLA_DK, GLA_DV, GLA_RANK, GLA_NORM, GLA_CHUNK = 4, 64, 128, 16, 16.0, 64
LRU_W, LRU_BLOCKS, LRU_CONV, LRU_C = 512, 8, 4, 8.0
DIL_PATTERNS = ((128, 1), (512, 4), (2048, 16))
N_PAT, DIL_HEADS, DIL_DH, ATTN_BLK = 3, 4, 128, 128
ROPE_THETA, ROPE_DIMS = 500000.0, 32
N_BR, BR_W = 3, 512
N_EXP, N_GRP, EXP_PER_GRP, TOP_K, EXP_FF = 16, 4, 4, 2, 512

LANE = 128
GATE_OFF, DIL_OFF, XB_OFF, YB_OFF, Q_OFF, K_OFF, V_OFF, OG_OFF = 0, 3072, 7680, 8192, 8704, 8960, 9216, 9728
PROJ_COLS = 10240
PROJ_TN = 2048
PROJ_RC = 512
MERGE_TM = 512
MOE_BM = 1024
VMEM_LIMIT = 56 * 1024 * 1024

NT = (((1,), (1,)), ((), ()))


def _cparams(sem):
    return pltpu.CompilerParams(dimension_semantics=sem, vmem_limit_bytes=VMEM_LIMIT)


def _inproj_body(x_ref, g_ref, w_ref, wg_ref, proj_ref, glow_ref, h_scr):
    j = pl.program_id(1)
    nrc = SEQ // PROJ_RC

    @pl.when(j == 0)
    def _():
        for c in range(nrc):
            rows = pl.ds(c * PROJ_RC, PROJ_RC)
            x = x_ref[0, rows, :]
            ms = jnp.mean(x * x, axis=-1, keepdims=True)
            h = ((x * lax.rsqrt(ms + RMS_EPS)) * g_ref[...]).astype(bf16)
            h_scr[rows, :] = h
            glow_ref[0, rows, :] = jnp.dot(h, wg_ref[...], preferred_element_type=f32)

    for c in range(nrc):
        rows = pl.ds(c * PROJ_RC, PROJ_RC)
        res = jnp.dot(h_scr[rows, :], w_ref[...], preferred_element_type=f32).astype(bf16)
        for t in range(PROJ_TN // LANE):
            proj_ref[0, t, rows, :] = res[:, t * LANE:(t + 1) * LANE]


def _inproj(x, gain, w, wg):
    B = x.shape[0]
    nj = PROJ_COLS // PROJ_TN
    return pl.pallas_call(
        _inproj_body,
        grid=(B, nj),
        in_specs=[
            pl.BlockSpec((1, SEQ, D_MODEL), lambda b, j: (b, 0, 0)),
            pl.BlockSpec((1, D_MODEL), lambda b, j: (0, 0)),
            pl.BlockSpec((D_MODEL, PROJ_TN), lambda b, j: (0, j)),
            pl.BlockSpec((D_MODEL, LANE), lambda b, j: (0, 0)),
        ],
        out_specs=[
            pl.BlockSpec((1, PROJ_TN // LANE, SEQ, LANE), lambda b, j: (b, j, 0, 0)),
            pl.BlockSpec((1, SEQ, LANE), lambda b, j: (b, 0, 0)),
        ],
        out_shape=[
            jax.ShapeDtypeStruct((B, PROJ_COLS // LANE, SEQ, LANE), bf16),
            jax.ShapeDtypeStruct((B, SEQ, LANE), f32),
        ],
        scratch_shapes=[pltpu.VMEM((SEQ, D_MODEL), bf16)],
        compiler_params=_cparams(("parallel", "arbitrary")),
        name="inproj",
    )(x, gain, w, wg)


def _gla_body(q_ref, k_ref, v_ref, og_ref, glow_ref, wh_ref, wl_ref, bg_ref, gain_ref, o_ref, la_scr, st_scr, stb_scr):
    C = GLA_CHUNK
    HK = GLA_HEADS * GLA_DK
    PR = 512
    for c in range(SEQ // PR):
        rows = pl.ds(c * PR, PR)
        g = glow_ref[0, rows, :]
        gh = g.astype(bf16)
        gl = (g - gh.astype(f32)).astype(bf16)
        z = (jnp.dot(gh, wh_ref[...], preferred_element_type=f32)
             + jnp.dot(gl, wh_ref[...], preferred_element_type=f32)
             + jnp.dot(gh, wl_ref[...], preferred_element_type=f32)) + bg_ref[...]
        la_scr[rows, :] = (jnp.minimum(z, 0.0) - jnp.log1p(jnp.exp(-jnp.abs(z)))) * (1.0 / GLA_NORM)
    st_scr[...] = jnp.zeros_like(st_scr)
    stb_scr[...] = jnp.zeros_like(stb_scr)

    causal = (lax.broadcasted_iota(i32, (C, GLA_HEADS * C), 0)
              >= lax.broadcasted_iota(i32, (C, GLA_HEADS * C), 1) % C)
    tri = (lax.broadcasted_iota(i32, (C, C), 0) >= lax.broadcasted_iota(i32, (C, C), 1)).astype(bf16)
    lane_head = lax.broadcasted_iota(i32, (1, HK), 1) // GLA_DK
    gain = gain_ref[...]
    zero_v = jnp.zeros((C, GLA_DV), bf16)

    def chunk(n, carry):
        rows = pl.ds(pl.multiple_of(n * C, C), C)
        la = la_scr[rows, :]
        p1 = la.astype(bf16)
        r1 = la - p1.astype(f32)
        p2 = r1.astype(bf16)
        p3 = (r1 - p2.astype(f32)).astype(bf16)
        c3 = jnp.dot(tri, jnp.concatenate([p1, p2, p3], axis=1), preferred_element_type=f32)
        cum = (c3[:, :HK] + c3[:, HK:2 * HK]) + c3[:, 2 * HK:]
        cum_t = cum.T
        tot_t = cum_t[:, C - 1:C]
        qf = jnp.concatenate([q_ref[0, t, rows, :] for t in range(HK // LANE)], axis=1).astype(f32)
        kf = jnp.concatenate([k_ref[0, t, rows, :] for t in range(HK // LANE)], axis=1).astype(f32)
        qd = (qf * (GLA_DK ** -0.5) * jnp.exp(cum)).astype(bf16)
        ki = kf * jnp.exp(-cum)
        kd_t = (kf.T * jnp.exp(tot_t - cum_t)).astype(bf16)
        dec_t = jnp.exp(tot_t)
        heads = range(GLA_HEADS)
        vcols = [v_ref[0, h, rows, :] for h in heads]
        k_st = jnp.concatenate([jnp.where(lane_head == h, ki, 0.0).astype(bf16) for h in heads], axis=0)
        v_bd = jnp.concatenate([jnp.concatenate([vcols[h] if g == h else zero_v for g in heads], axis=1)
                                for h in heads], axis=0)
        s = lax.dot_general(qd, k_st, NT, preferred_element_type=f32)
        s = jnp.where(causal, s, 0.0).astype(bf16)
        o = jnp.dot(s, v_bd, preferred_element_type=f32) + jnp.dot(qd, stb_scr[...], preferred_element_type=f32)
        for h in heads:
            cols = slice(h * GLA_DV, (h + 1) * GLA_DV)
            hr = slice(h * GLA_DK, (h + 1) * GLA_DK)
            st = st_scr[h] * dec_t[hr, :] + jnp.dot(kd_t[hr, :], vcols[h], preferred_element_type=f32)
            st_scr[h] = st
            stb_scr[hr, cols] = st.astype(bf16)
            oh = o[:, cols]
            oh = oh * lax.rsqrt(jnp.mean(oh * oh, axis=-1, keepdims=True) + RMS_EPS) * gain
            g = og_ref[0, h, rows, :].astype(f32)
            o_ref[0, rows, cols] = (oh * (g * jax.nn.sigmoid(g))).astype(bf16)
        return carry

    lax.fori_loop(0, SEQ // C, chunk, 0, unroll=4)


def _gla(proj, glow, wup, bg, gain):
    B = proj.shape[0]
    HK = GLA_HEADS * GLA_DK
    HV = GLA_HEADS * GLA_DV
    wup_hi = wup.astype(bf16)
    wup_lo = (wup - wup_hi.astype(f32)).astype(bf16)
    return pl.pallas_call(
        _gla_body,
        grid=(B,),
        in_specs=[
            pl.BlockSpec((1, HK // LANE, SEQ, LANE), lambda b: (b, Q_OFF // HK, 0, 0)),
            pl.BlockSpec((1, HK // LANE, SEQ, LANE), lambda b: (b, K_OFF // HK, 0, 0)),
            pl.BlockSpec((1, HV // LANE, SEQ, LANE), lambda b: (b, V_OFF // HV, 0, 0)),
            pl.BlockSpec((1, HV // LANE, SEQ, LANE), lambda b: (b, OG_OFF // HV, 0, 0)),
            pl.BlockSpec((1, SEQ, LANE), lambda b: (b, 0, 0)),
            pl.BlockSpec((LANE, HK), lambda b: (0, 0)),
            pl.BlockSpec((LANE, HK), lambda b: (0, 0)),
            pl.BlockSpec((1, HK), lambda b: (0, 0)),
            pl.BlockSpec((1, GLA_DV), lambda b: (0, 0)),
        ],
        out_specs=pl.BlockSpec((1, SEQ, HV), lambda b: (b, 0, 0)),
        out_shape=jax.ShapeDtypeStruct((B, SEQ, HV), bf16),
        scratch_shapes=[
            pltpu.VMEM((SEQ, HK), f32),
            pltpu.VMEM((GLA_HEADS, GLA_DK, GLA_DV), f32),
            pltpu.VMEM((HK, HV), bf16),
        ],
        compiler_params=_cparams(("parallel",)),
        name="gla",
    )(proj, proj, proj, proj, glow, wup_hi, wup_lo, bg, gain)


LRU_GRP = 8
LRU_RC = 512
LRU_PAD = 8
LRU_NT = LRU_W // LANE


def _lru_body(xb_ref, yb_ref, cw_ref, cb_ref, w_ref, b_ref, lam_ref, o_ref, xpad, a_scr, h_scr):
    grp = lax.broadcasted_iota(i32, (LRU_RC // LRU_GRP, LRU_GRP, LANE), 1)
    for t in range(LRU_NT):
        lanes = slice(t * LANE, (t + 1) * LANE)
        xpad[pl.ds(0, LRU_PAD), :] = jnp.zeros((LRU_PAD, LANE), f32)
        xpad[pl.ds(LRU_PAD, SEQ), :] = xb_ref[t].astype(f32)
        nl = -lam_ref[:, lanes]
        sp = jnp.maximum(nl, 0.0) + jnp.log1p(jnp.exp(-jnp.abs(nl)))
        for c in range(SEQ // LRU_RC):
            r0 = c * LRU_RC
            xc = cb_ref[:, lanes]
            for j in range(LRU_CONV):
                xc = xc + xpad[pl.ds(LRU_PAD - j + r0, LRU_RC), :] * cw_ref[LRU_CONV - 1 - j:LRU_CONV - j, lanes]
            z = jnp.dot(xc.astype(bf16), w_ref[t], preferred_element_type=f32) + b_ref[t]
            r = jax.nn.sigmoid(z[:, :LANE])
            gi = jax.nn.sigmoid(z[:, LANE:])
            log_a = (-LRU_C) * r * sp
            a = jnp.exp(log_a)
            u = jnp.sqrt(jnp.tanh(-log_a) * (a * a + 1.0)) * gi * xc
            a = a.reshape(LRU_RC // LRU_GRP, LRU_GRP, LANE)
            u = u.reshape(LRU_RC // LRU_GRP, LRU_GRP, LANE)
            k = 1
            while k < LRU_GRP:
                inside = grp >= k
                u = u + a * jnp.where(inside, pltpu.roll(u, k, 1), 0.0)
                a = a * jnp.where(inside, pltpu.roll(a, k, 1), 1.0)
                k *= 2
            a_scr[t, pl.ds(r0, LRU_RC), :] = a.reshape(LRU_RC, LANE)
            h_scr[t, pl.ds(r0, LRU_RC), :] = u.reshape(LRU_RC, LANE)

    def group(g, carry):
        rows = pl.ds(pl.multiple_of(g * LRU_GRP, LRU_GRP), LRU_GRP)
        out = []
        for t in range(LRU_NT):
            h = h_scr[t, rows, :] + a_scr[t, rows, :] * jnp.broadcast_to(carry[t], (LRU_GRP, LANE))
            h_scr[t, rows, :] = h
            out.append(h[LRU_GRP - 1:LRU_GRP, :])
        return tuple(out)

    lax.fori_loop(0, SEQ // LRU_GRP, group, tuple(jnp.zeros((1, LANE), f32) for _ in range(LRU_NT)), unroll=8)
    for t in range(LRU_NT):
        for c in range(SEQ // LRU_RC):
            rows = pl.ds(c * LRU_RC, LRU_RC)
            o_ref[0, rows, t * LANE:(t + 1) * LANE] = (h_scr[t, rows, :]
                                                       * jax.nn.gelu(yb_ref[t, rows, :].astype(f32))).astype(bf16)


def _lru(proj, cw, cb, w, b, lam):
    B = proj.shape[0]
    full = lambda shape: pl.BlockSpec(shape, lambda b: (0,) * len(shape))
    return pl.pallas_call(
        _lru_body,
        grid=(B,),
        in_specs=[
            pl.BlockSpec((None, LRU_NT, SEQ, LANE), lambda b: (b, XB_OFF // LRU_W, 0, 0)),
            pl.BlockSpec((None, LRU_NT, SEQ, LANE), lambda b: (b, YB_OFF // LRU_W, 0, 0)),
            full((LRU_CONV, LRU_W)), full((1, LRU_W)), full((LRU_NT, LANE, 2 * LANE)), full((LRU_NT, 1, 2 * LANE)),
            full((1, LRU_W)),
        ],
        out_specs=pl.BlockSpec((1, SEQ, LRU_W), lambda b: (b, 0, 0)),
        out_shape=jax.ShapeDtypeStruct((B, SEQ, LRU_W), bf16),
        scratch_shapes=[
            pltpu.VMEM((SEQ + LRU_PAD, LANE), f32),
            pltpu.VMEM((LRU_NT, SEQ, LANE), f32),
            pltpu.VMEM((LRU_NT, SEQ, LANE), f32),
        ],
        compiler_params=_cparams(("parallel",)),
        name="lru",
    )(proj, proj, cw, cb, w, b, lam)


def _rope_body(pos_ref, freq_ref, cos_ref, ss_ref):
    ang = pos_ref[0] * freq_ref[...]
    lane = lax.broadcasted_iota(i32, (SEQ, LANE), 1)
    sn = jnp.sin(ang)
    cos_ref[0] = jnp.cos(ang)
    ss_ref[0] = jnp.where(lane < ROPE_DIMS // 2, -sn, sn)


def _rope_tables(posf, freq):
    B = posf.shape[0]
    spec = pl.BlockSpec((1, SEQ, LANE), lambda b: (b, 0, 0))
    return pl.pallas_call(
        _rope_body,
        grid=(B,),
        in_specs=[pl.BlockSpec((1, SEQ, 1), lambda b: (b, 0, 0)), pl.BlockSpec((1, LANE), lambda b: (0, 0))],
        out_specs=[spec, spec],
        out_shape=[jax.ShapeDtypeStruct((B, SEQ, LANE), f32)] * 2,
        compiler_params=_cparams(("parallel",)),
        name="rope_tables",
    )(posf, freq)


def _rope_partner():
    half = ROPE_DIMS // 2
    perm = np.arange(LANE)
    perm[:half] += half
    perm[half:ROPE_DIMS] -= half
    return perm


ATTN_PC = 256
ATTN_G = 8


def _attn_body(q0, q1, q2, k0, k1, k2, v0, v1, v2, cos_ref, ss_ref, g_ref, p2_ref, o_ref,
               qd, kd, vd, tq, tk, tv, od, ld, on, ln, sc):
    q_refs, k_refs, v_refs = (q0, q1, q2), (k0, k1, k2), (v0, v1, v2)
    BLK = ATTN_BLK
    qg = g_ref[0:1, :] * (DIL_DH ** -0.5)
    qgr = g_ref[1:2, :] * (DIL_DH ** -0.5)
    kg = g_ref[2:3, :]
    kgr = g_ref[3:4, :]
    for p in range(N_PAT):
        kd[p, pl.ds(0, BLK), :] = jnp.zeros((BLK, LANE), bf16)
        vd[p, pl.ds(0, BLK), pl.ds(0, LANE)] = jnp.zeros((BLK, LANE), bf16)
        vd[p, :, pl.ds(LANE, LANE)] = jnp.ones((SEQ + BLK, LANE), bf16)

    qi = lax.broadcasted_iota(i32, (BLK, 2 * BLK), 0)
    kj = lax.broadcasted_iota(i32, (BLK, 2 * BLK), 1)
    is_cur = kj >= BLK
    band = jnp.where(is_cur, kj - BLK, qi) <= jnp.where(is_cur, qi, kj)

    def norm_rope(raw, partner, gain, gain_p, cs, ss):
        rf = raw.astype(f32)
        r = lax.rsqrt(jnp.mean(rf * rf, axis=-1, keepdims=True) + RMS_EPS)
        return (rf * (gain * cs) + partner * (gain_p * ss)) * r

    for p, (_, dil) in enumerate(DIL_PATTERNS):
        L = SEQ // dil
        nbl = L // BLK
        qd_p, kd_p, vd_p, sc_a, sc_b = qd.at[p], kd.at[p], vd.at[p], sc.at[2 * p], sc.at[2 * p + 1]
        if dil > 1:
            tq_p, tk_p, tv_p, od_p, ld_p = tq.at[p - 1], tk.at[p - 1], tv.at[p - 1], od.at[p - 1], ld.at[p - 1]

        for c in range(SEQ // ATTN_PC):
            rows = pl.ds(c * ATTN_PC, ATTN_PC)
            cs, ss = cos_ref[0, rows, :], ss_ref[0, rows, :]
            qr = q_refs[p][0, rows, :]
            kr = k_refs[p][0, rows, :]
            partner = jnp.dot(jnp.concatenate([qr, kr], axis=1), p2_ref[...], preferred_element_type=f32)
            q = norm_rope(qr, partner[:, :LANE], qg, qgr, cs, ss)
            k = norm_rope(kr, partner[:, LANE:], kg, kgr, cs, ss)
            if dil == 1:
                orow = pl.ds(c * ATTN_PC + BLK, ATTN_PC)
                qd_p[rows, :] = q.astype(bf16)
                kd_p[orow, :] = k.astype(bf16)
                vd_p[orow, pl.ds(0, LANE)] = v_refs[p][0, rows, :]
            else:
                tq_p[rows, :] = q
                tk_p[rows, :] = k
                tv_p[rows, :] = v_refs[p][0, rows, :].astype(f32)
        if dil > 1:
            for r in range(dil):
                src = pl.ds(r, L, stride=dil)
                qd_p[pl.ds(r * L, L), :] = tq_p[src, :].astype(bf16)
                kd_p[pl.ds(BLK + r * L, L), :] = tk_p[src, :].astype(bf16)
                vd_p[pl.ds(BLK + r * L, L), pl.ds(0, LANE)] = tv_p[src, :].astype(bf16)

        o_dst, l_dst = (on.at[p], ln.at[p]) if dil == 1 else (od_p, ld_p)

        def scores(g, dst, qd_p=qd_p, kd_p=kd_p):
            for j in range(ATTN_G):
                r0 = pl.multiple_of((g * ATTN_G + j) * BLK, BLK)
                dst[pl.ds(j * BLK, BLK), :] = lax.dot_general(qd_p[pl.ds(r0, BLK), :], kd_p[pl.ds(r0, 2 * BLK), :], NT,
                                                              preferred_element_type=f32)

        def finish(g, src, nbl=nbl, o_dst=o_dst, l_dst=l_dst, vd=vd_p):
            for j in range(ATTN_G):
                b = g * ATTN_G + j
                r0 = pl.multiple_of(b * BLK, BLK)
                first = (b % nbl) == 0
                valid = band & (is_cur | jnp.logical_not(first))
                s = jnp.where(valid, src[pl.ds(j * BLK, BLK), :], -jnp.inf)
                m = jnp.max(s, axis=-1, keepdims=True)
                e = jnp.exp(s - m).astype(bf16)
                ov = jnp.dot(e, vd[pl.ds(r0, 2 * BLK), :], preferred_element_type=f32)
                den = ov[:, LANE:]
                o_dst[pl.ds(r0, BLK), :] = ov[:, :LANE] / den
                l_dst[pl.ds(r0, BLK), :] = m + jnp.log(den)

        n_groups = SEQ // BLK // ATTN_G
        scores(0, sc_a)

        def pair(i, carry, scores=scores, finish=finish, sc_a=sc_a, sc_b=sc_b):
            scores(2 * i + 1, sc_b)
            finish(2 * i, sc_a)
            if n_groups > 2:
                scores(jnp.minimum(2 * i + 2, n_groups - 1), sc_a)
            finish(2 * i + 1, sc_b)
            return carry

        lax.fori_loop(0, n_groups // 2, pair, 0)
        if dil > 1:
            for r in range(dil):
                dst = pl.ds(r, L, stride=dil)
                on[p, dst, :] = od_p[pl.ds(r * L, L), :]
                ln[p, dst, :] = ld_p[pl.ds(r * L, L), :]

    def combine(c, carry):
        rows = pl.ds(pl.multiple_of(c * ATTN_PC, ATTN_PC), ATTN_PC)
        l0, l1, l2 = ln[0, rows, :], ln[1, rows, :], ln[2, rows, :]
        mx = jnp.maximum(jnp.maximum(l0, l1), l2)
        w0, w1, w2 = jnp.exp(l0 - mx), jnp.exp(l1 - mx), jnp.exp(l2 - mx)
        o = (w0 * on[0, rows, :] + w1 * on[1, rows, :] + w2 * on[2, rows, :]) / (w0 + w1 + w2)
        o_ref[0, rows, :] = o.astype(bf16)
        return carry

    lax.fori_loop(0, SEQ // ATTN_PC, combine, 0)


def _attn(proj, cos, ss, qg, kg):
    B = proj.shape[0]
    base = DIL_OFF // LANE
    perm = _rope_partner()
    gains = jnp.stack([qg, qg[perm], kg, kg[perm]]).astype(f32)
    pm = np.zeros((LANE, LANE), np.float32)
    pm[perm[:ROPE_DIMS], np.arange(ROPE_DIMS)] = 1.0
    p2 = jnp.asarray(np.kron(np.eye(2, dtype=np.float32), pm), dtype=bf16)

    def pspec(qkv, p):
        return pl.BlockSpec((None, 1, SEQ, LANE), lambda b, h, o=base + (qkv * N_PAT + p) * DIL_HEADS: (b, o + h, 0, 0))

    tspec = pl.BlockSpec((1, SEQ, LANE), lambda b, h: (b, 0, 0))
    in_specs = [pspec(qkv, p) for qkv in range(3) for p in range(N_PAT)] + [
        tspec, tspec, pl.BlockSpec((4, LANE), lambda b, h: (0, 0)), pl.BlockSpec((2 * LANE, 2 * LANE), lambda b, h: (0, 0))]
    return pl.pallas_call(
        _attn_body,
        grid=(B, DIL_HEADS),
        in_specs=in_specs,
        out_specs=pl.BlockSpec((1, SEQ, LANE), lambda b, h: (b, 0, h)),
        out_shape=jax.ShapeDtypeStruct((B, SEQ, DIL_HEADS * DIL_DH), bf16),
        scratch_shapes=[
            pltpu.VMEM((N_PAT, SEQ, LANE), bf16),
            pltpu.VMEM((N_PAT, SEQ + ATTN_BLK, LANE), bf16),
            pltpu.VMEM((N_PAT, SEQ + ATTN_BLK, 2 * LANE), bf16),
            pltpu.VMEM((N_PAT - 1, SEQ, LANE), f32),
            pltpu.VMEM((N_PAT - 1, SEQ, LANE), f32),
            pltpu.VMEM((N_PAT - 1, SEQ, LANE), f32),
            pltpu.VMEM((N_PAT - 1, SEQ, LANE), f32),
            pltpu.VMEM((N_PAT - 1, SEQ, LANE), f32),
            pltpu.VMEM((N_PAT, SEQ, LANE), f32),
            pltpu.VMEM((N_PAT, SEQ, LANE), f32),
            pltpu.VMEM((2 * N_PAT, ATTN_G * ATTN_BLK, 2 * ATTN_BLK), f32),
        ],
        compiler_params=_cparams(("parallel", "arbitrary")),
        name="dilated_attn",
    )(*([proj] * 9), cos, ss, gains, p2)


def _merge_body(a_ref, b_ref, c_ref, g0_ref, g1_ref, g2_ref, x_ref, wb_ref, wo_ref, n2_ref, rw_ref, rb_ref, tri_ref,
                x1_ref, h2_ref, ei_ref, ew_ref, pos_ref, cnt_ref, cb_ref, cnt_scr):
    merged = None
    for n, (br, g) in enumerate(((a_ref, g0_ref), (b_ref, g1_ref), (c_ref, g2_ref))):
        logits = jnp.concatenate([g[t] for t in range(D_MODEL // LANE)], axis=1)
        gate = 0.5 * jnp.tanh(0.5 * logits.astype(f32)) + 0.5
        t = gate * jnp.dot(br[...], wb_ref[n], preferred_element_type=f32)
        merged = t if merged is None else merged + t
    x1 = x_ref[...] + jnp.dot(merged.astype(bf16), wo_ref[...], preferred_element_type=f32)
    x1_ref[...] = x1
    h2 = ((x1 * lax.rsqrt(jnp.mean(x1 * x1, axis=-1, keepdims=True) + RMS_EPS)) * n2_ref[...]).astype(bf16)
    h2_ref[...] = h2

    lg = lax.dot_general(rw_ref[...], h2, NT, preferred_element_type=f32) + rb_ref[...]
    ex = jnp.exp(lg - jnp.max(lg, axis=0, keepdims=True))
    sc = ex / jnp.sum(ex, axis=0, keepdims=True)
    srow = [sc[e:e + 1, :] for e in range(N_EXP)]
    best = bidx = None
    for gidx in range(N_GRP):
        s0, s1, s2, s3 = srow[4 * gidx:4 * gidx + 4]
        hi01, lo01, hi23, lo23 = jnp.maximum(s0, s1), jnp.minimum(s0, s1), jnp.maximum(s2, s3), jnp.minimum(s2, s3)
        gs = jnp.maximum(hi01, hi23) + jnp.maximum(jnp.minimum(hi01, hi23), jnp.maximum(lo01, lo23))
        if best is None:
            best, bidx = gs, jnp.zeros_like(gs, dtype=i32)
        else:
            better = gs > best
            best = jnp.where(better, gs, best)
            bidx = jnp.where(better, gidx, bidx)
    cand = []
    for i in range(EXP_PER_GRP):
        v = srow[i]
        for gidx in range(1, N_GRP):
            v = jnp.where(bidx == gidx, srow[4 * gidx + i], v)
        cand.append(v)

    def argmax4(vals):
        bw, bi = vals[0], jnp.zeros_like(bidx)
        for i in range(1, EXP_PER_GRP):
            better = vals[i] > bw
            bw = jnp.where(better, vals[i], bw)
            bi = jnp.where(better, i, bi)
        return bw, bi

    w1, i1 = argmax4(cand)
    w2, i2 = argmax4([jnp.where(i1 == i, -jnp.inf, cand[i]) for i in range(EXP_PER_GRP)])
    tot = w1 + w2
    e1 = bidx * EXP_PER_GRP + i1
    e2 = bidx * EXP_PER_GRP + i2
    ei_ref[0:1, :] = e1
    ei_ref[1:2, :] = e2
    ew_ref[0:1, :] = w1 / tot
    ew_ref[1:2, :] = w2 / tot

    @pl.when(pl.program_id(0) == 0)
    def _():
        cnt_scr[...] = jnp.zeros_like(cnt_scr)

    erow = lax.broadcasted_iota(i32, (N_EXP, MERGE_TM), 0)
    oh1 = erow == e1
    oh2 = erow == e2
    both = jnp.where(oh1 | oh2, 1.0, 0.0)
    incl = jnp.dot(both.astype(bf16), tri_ref[...], preferred_element_type=f32)
    cb_ref[0] = cnt_scr[...]
    base = cnt_scr[...] + (incl - both)
    pos_ref[0:1, :] = jnp.sum(jnp.where(oh1, base, 0.0), axis=0, keepdims=True).astype(i32)
    pos_ref[1:2, :] = jnp.sum(jnp.where(oh2, base, 0.0), axis=0, keepdims=True).astype(i32)
    cnt = cnt_scr[...] + incl[:, MERGE_TM - 1:MERGE_TM]
    cnt_scr[...] = cnt
    cnt_ref[...] = cnt


def _merge(bra, brb, brc, proj, x2d, wb, wo, n2, rw, rb):
    T = bra.shape[0]
    TM = MERGE_TM
    per_b = SEQ // TM
    gt = D_MODEL // LANE

    def gspec(n):
        return pl.BlockSpec((None, gt, TM, LANE), lambda i: (i // per_b, GATE_OFF // D_MODEL + n, i % per_b, 0))

    rspec = lambda w: pl.BlockSpec((TM, w), lambda i: (i, 0))
    full = lambda shape: pl.BlockSpec(shape, lambda i: (0,) * len(shape))
    tri = (jnp.arange(TM)[:, None] <= jnp.arange(TM)[None, :]).astype(bf16)
    return pl.pallas_call(
        _merge_body,
        grid=(T // TM,),
        in_specs=[
            rspec(BR_W), rspec(BR_W), rspec(BR_W),
            gspec(0), gspec(1), gspec(2),
            rspec(D_MODEL),
            full((N_BR, BR_W, D_MODEL)), full((D_MODEL, D_MODEL)), full((1, D_MODEL)),
            full((N_EXP, D_MODEL)), full((N_EXP, 1)), full((TM, TM)),
        ],
        out_specs=[
            rspec(D_MODEL), rspec(D_MODEL),
            pl.BlockSpec((TOP_K, TM), lambda i: (0, i)),
            pl.BlockSpec((TOP_K, TM), lambda i: (0, i)),
            pl.BlockSpec((TOP_K, TM), lambda i: (0, i)),
            full((N_EXP, 1)),
            pl.BlockSpec((1, N_EXP, 1), lambda i: (i, 0, 0)),
        ],
        out_shape=[
            jax.ShapeDtypeStruct((T, D_MODEL), f32),
            jax.ShapeDtypeStruct((T, D_MODEL), bf16),
            jax.ShapeDtypeStruct((TOP_K, T), i32),
            jax.ShapeDtypeStruct((TOP_K, T), f32),
            jax.ShapeDtypeStruct((TOP_K, T), i32),
            jax.ShapeDtypeStruct((N_EXP, 1), f32),
            jax.ShapeDtypeStruct((T // TM, N_EXP, 1), f32),
        ],
        scratch_shapes=[pltpu.VMEM((N_EXP, 1), f32)],
        compiler_params=_cparams(("arbitrary",)),
        name="merge_router",
    )(bra, brb, brc, proj, proj, proj, x2d, wb, wo, n2, rw, rb, tri)


def _expert_body(be_ref, nu_ref, xs_ref, wgu_ref, wd_ref, ys_ref):
    i = pl.program_id(0)

    @pl.when(i < nu_ref[0])
    def _():
        gu = jnp.dot(xs_ref[...], wgu_ref[0], preferred_element_type=f32)
        g = gu[:, :EXP_FF]
        act = (g * jax.nn.sigmoid(g)) * gu[:, EXP_FF:]
        ys_ref[...] = jnp.dot(act.astype(bf16), wd_ref[0], preferred_element_type=f32).astype(bf16)

    @pl.when(i >= nu_ref[0])
    def _():
        ys_ref[...] = jnp.zeros_like(ys_ref)


def _experts(block_e, n_used, xs, wgu, wd):
    n_rows = xs.shape[0]
    grid_spec = pltpu.PrefetchScalarGridSpec(
        num_scalar_prefetch=2,
        grid=(n_rows // MOE_BM,),
        in_specs=[
            pl.BlockSpec((MOE_BM, D_MODEL), lambda i, be, nu: (jnp.minimum(i, nu[0] - 1), 0)),
            pl.BlockSpec((1, D_MODEL, 2 * EXP_FF), lambda i, be, nu: (be[i], 0, 0)),
            pl.BlockSpec((1, EXP_FF, D_MODEL), lambda i, be, nu: (be[i], 0, 0)),
        ],
        out_specs=pl.BlockSpec((MOE_BM, D_MODEL), lambda i, be, nu: (i, 0)),
    )
    return pl.pallas_call(
        _expert_body,
        grid_spec=grid_spec,
        out_shape=jax.ShapeDtypeStruct((n_rows, D_MODEL), bf16),
        compiler_params=_cparams(("arbitrary",)),
        name="experts",
    )(block_e, n_used, xs, wgu, wd)


COMB_TM = 1024


def _combine_body(x1_ref, y0_ref, y1_ref, w_ref, o_ref):
    w = w_ref[...]
    o_ref[...] = x1_ref[...] + y0_ref[...].astype(f32) * w[:, 0:1] + y1_ref[...].astype(f32) * w[:, 1:2]


def _combine(x1, y0, y1, w):
    T = x1.shape[0]
    rspec = pl.BlockSpec((COMB_TM, D_MODEL), lambda i: (i, 0))
    return pl.pallas_call(
        _combine_body,
        grid=(T // COMB_TM,),
        in_specs=[rspec, rspec, rspec, pl.BlockSpec((COMB_TM, TOP_K), lambda i: (i, 0))],
        out_specs=rspec,
        out_shape=jax.ShapeDtypeStruct((T, D_MODEL), f32),
        compiler_params=_cparams(("parallel",)),
        name="moe_combine",
    )(x1, y0, y1, w)


DISP_ALIGN = 16
DISP_SLOTS = MERGE_TM * TOP_K + N_EXP * DISP_ALIGN
DISP_SIZES = tuple(DISP_ALIGN << s for s in range(5, -1, -1))
TAIL_SIZES = tuple(sz for sz in (MOE_BM // 2 >> s for s in range(12)) if sz >= DISP_ALIGN)


def _pieces(length, sizes, fn):
    done = jnp.int32(0)
    for sz in sizes:
        take = (length & sz) != 0

        @pl.when(take)
        def _(done=done, sz=sz):
            fn(done, sz)

        done = done + jnp.where(take, sz, 0)


def _dispatch_body(off_ref, n_ref, ls_ref, toff_ref, tlen_ref, h2_ref, ei_ref, pos_ref, adj_ref, xs_hbm,
                   srt, zbuf, sem, zsem):
    i = pl.program_id(0)
    last = pl.num_programs(0) - 1
    cur = i % 2
    TM = MERGE_TM

    def slab_copies(step, buf, start):
        for e in range(N_EXP):
            ls = ls_ref[step * N_EXP + e]
            off = off_ref[step * N_EXP + e]

            def piece(done, sz, ls=ls, off=off):
                cp = pltpu.make_async_copy(srt.at[buf, pl.ds(pl.multiple_of(ls + done, DISP_ALIGN), sz)],
                                           xs_hbm.at[pl.ds(pl.multiple_of(off + done, DISP_ALIGN), sz)], sem.at[buf])
                cp.start() if start else cp.wait()

            _pieces(n_ref[step * N_EXP + e], DISP_SIZES, piece)

    def tail_copies(start):
        for e in range(N_EXP):
            off = toff_ref[e]

            def piece(done, sz, off=off):
                cp = pltpu.make_async_copy(zbuf.at[pl.ds(0, sz)],
                                           xs_hbm.at[pl.ds(pl.multiple_of(off + done, DISP_ALIGN), sz)], zsem)
                cp.start() if start else cp.wait()

            _pieces(tlen_ref[e], TAIL_SIZES, piece)

        zrows = zbuf.shape[0]

        def block(c, carry):
            cp = pltpu.make_async_copy(zbuf, xs_hbm.at[pl.ds(pl.multiple_of(toff_ref[N_EXP] + c * zrows, zrows), zrows)],
                                       zsem)
            cp.start() if start else cp.wait()
            return carry

        lax.fori_loop(0, tlen_ref[N_EXP] // zrows, block, 0)

    @pl.when(i == 0)
    def _():
        zbuf[...] = jnp.zeros_like(zbuf)
        tail_copies(True)

    @pl.when(i >= 2)
    def _():
        slab_copies(i - 2, cur, False)

    erow = lax.broadcasted_iota(i32, (N_EXP, TM), 0)
    adj = adj_ref[0]

    def slot(k):
        sel = jnp.sum(jnp.where(erow == ei_ref[k:k + 1, :], adj, 0.0), axis=0, keepdims=True)
        return pos_ref[k:k + 1, :] + sel.astype(i32)

    srow = lax.broadcasted_iota(i32, (DISP_SLOTS, TM), 0)
    perm = jnp.where((srow == slot(0)) | (srow == slot(1)), 1.0, 0.0).astype(bf16)
    srt[cur] = jnp.dot(perm, h2_ref[...], preferred_element_type=f32).astype(bf16)
    slab_copies(i, cur, True)

    @pl.when(i == last)
    def _():
        slab_copies(i, cur, False)

        @pl.when(i >= 1)
        def _():
            slab_copies(i - 1, 1 - cur, False)

        tail_copies(False)


def _dispatch(off, n16, lstart, tail_off, tail_len, h2, eidx, pos, adj, n_rows):
    T = h2.shape[0]
    TM = MERGE_TM
    grid_spec = pltpu.PrefetchScalarGridSpec(
        num_scalar_prefetch=5,
        grid=(T // TM,),
        in_specs=[
            pl.BlockSpec((TM, D_MODEL), lambda i, *_: (i, 0)),
            pl.BlockSpec((TOP_K, TM), lambda i, *_: (0, i)),
            pl.BlockSpec((TOP_K, TM), lambda i, *_: (0, i)),
            pl.BlockSpec((1, N_EXP, 1), lambda i, *_: (i, 0, 0)),
        ],
        out_specs=pl.BlockSpec(memory_space=pl.ANY),
        scratch_shapes=[
            pltpu.VMEM((2, DISP_SLOTS, D_MODEL), bf16),
            pltpu.VMEM((TAIL_SIZES[0], D_MODEL), bf16),
            pltpu.SemaphoreType.DMA((2,)),
            pltpu.SemaphoreType.DMA(()),
        ],
    )
    return pl.pallas_call(
        _dispatch_body,
        grid_spec=grid_spec,
        out_shape=jax.ShapeDtypeStruct((n_rows, D_MODEL), bf16),
        compiler_params=_cparams(("arbitrary",)),
        name="moe_dispatch",
    )(off, n16, lstart, tail_off, tail_len, h2, eidx, pos, adj)


def _moe(x1, h2, eidx, ew, pos, cnt, cb, wgu, wd):
    T = x1.shape[0]
    TM = MERGE_TM
    n_tiles = T // TM
    ceil_to = lambda v, m: (v + m - 1) // m * m
    n_rows = ceil_to(T * TOP_K + n_tiles * N_EXP * (DISP_ALIGN - 1) + N_EXP * (MOE_BM - 1), MOE_BM)
    n_blocks = n_rows // MOE_BM
    cbi = cb[:, :, 0].astype(i32)
    tile_cnt = jnp.concatenate([cbi[1:], cnt[:, 0].astype(i32)[None, :]], axis=0) - cbi
    n16 = ceil_to(tile_cnt, DISP_ALIGN)
    lstart = jnp.cumsum(n16, axis=1) - n16
    before = jnp.cumsum(n16, axis=0) - n16
    used = jnp.sum(n16, axis=0)
    padded = ceil_to(used, MOE_BM)
    pad_ends = jnp.cumsum(padded)
    off = (pad_ends - padded)[None, :] + before
    adj = (lstart - cbi).astype(f32)[:, :, None]
    row_adj = jnp.repeat(off - cbi, TM, axis=0).T
    dest = pos
    for e in range(N_EXP):
        dest = dest + jnp.where(eidx == e, row_adj[e][None, :], 0)
    blk_start = jnp.arange(n_blocks, dtype=i32) * MOE_BM
    block_e = jnp.minimum(jnp.sum((blk_start[:, None] >= pad_ends[None, :]).astype(i32), axis=1), N_EXP - 1)
    n_used = (pad_ends[-1:] // MOE_BM).astype(i32)
    tail_off = jnp.concatenate([pad_ends - padded + used, pad_ends[-1:]])
    tail_len = jnp.concatenate([padded - used, n_rows - pad_ends[-1:]])
    xs = _dispatch(off.reshape(-1), n16.reshape(-1), lstart.reshape(-1), tail_off, tail_len, h2, eidx, pos, adj, n_rows)
    ys = _experts(block_e, n_used, xs, wgu, wd)
    return _combine(x1, ys[dest[0]], ys[dest[1]], ew.T)


def _pack_w_in(w):
    cuts = np.cumsum([256, 256, 512, 512, 16, 512, 512, 4608, 3072])[:-1].tolist()
    q, k, v, og, glow, xb, yb, qkv, gates = jnp.split(w, cuts, axis=-1)
    packed = jnp.concatenate([gates, qkv, xb, yb, q, k, v, og], axis=-1).astype(bf16)
    wg = jnp.pad(glow, ((0, 0), (0, LANE - GLA_RANK))).astype(bf16)
    return packed, wg


def _block_diag_lru(w_a, w_x):
    bs = LRU_W // LRU_BLOCKS
    per = LANE // bs
    tiles = []
    for t in range(LRU_W // LANE):
        halves = []
        for w in (w_a, w_x):
            m = jnp.zeros((LANE, LANE), f32)
            for j in range(per):
                m = m.at[j * bs:(j + 1) * bs, j * bs:(j + 1) * bs].set(w[t * per + j])
            halves.append(m)
        tiles.append(jnp.concatenate(halves, axis=1))
    return jnp.stack(tiles).astype(bf16)


def kernel(x, positions, norm1_gain, w_in, gla_gate_up, gla_gate_bias, gla_out_gain, lru_conv_w, lru_conv_b, lru_w_a, lru_b_a, lru_w_x, lru_b_x, lru_lambda, q_norm_gain, k_norm_gain, w_branch, w_out, norm2_gain, router_w, router_b, w_gate, w_up, w_down):
    B, S, D = x.shape
    assert S == SEQ and D == D_MODEL
    T = B * S
    depth = w_in.shape[0]

    half = ROPE_DIMS // 2
    inv_freq = ROPE_THETA ** (-jnp.arange(half, dtype=f32) / half)
    freq = jnp.zeros((1, LANE), f32).at[0, :ROPE_DIMS].set(jnp.concatenate([inv_freq, inv_freq]))
    cos, ss = _rope_tables(positions.astype(f32)[:, :, None], freq)

    rw = router_w.T.astype(bf16)
    rb = router_b.astype(f32)[:, None]
    for l in range(depth):
        w_packed, wg = _pack_w_in(w_in[l])
        wup = jnp.pad(gla_gate_up[l], ((0, LANE - GLA_RANK), (0, 0)))
        nt = LRU_W // LANE
        lru_w = _block_diag_lru(lru_w_a[l], lru_w_x[l])
        lru_b = jnp.concatenate([lru_b_a[l].reshape(nt, 1, LANE), lru_b_x[l].reshape(nt, 1, LANE)], axis=-1)
        wgu = jnp.concatenate([w_gate[l], w_up[l]], axis=-1).astype(bf16)

        proj, glow = _inproj(x, norm1_gain[l][None, :], w_packed, wg)
        br_a = _gla(proj, glow, wup, gla_gate_bias[l][None, :], gla_out_gain[l][None, :])
        br_b = _lru(proj, lru_conv_w[l], lru_conv_b[l][None, :], lru_w, lru_b, lru_lambda[l][None, :])
        br_c = _attn(proj, cos, ss, q_norm_gain[l], k_norm_gain[l])
        routed = _merge(br_a.reshape(T, BR_W), br_b.reshape(T, BR_W), br_c.reshape(T, BR_W), proj, x.reshape(T, D),
                        w_branch[l].astype(bf16), w_out[l].astype(bf16), norm2_gain[l][None, :], rw, rb)
        x = _moe(*routed, wgu, w_down[l].astype(bf16)).reshape(B, S, D)
    return x
```

```python
import jax
import jax.numpy as jnp
import numpy as np
from jax import lax
from jax.experimental import pallas as pl
from jax.experimental.pallas import tpu as pltpu

f32 = jnp.float32
bf16 = jnp.bfloat16
i32 = jnp.int32

D_MODEL = 1024
SEQ = 2048
RMS_EPS = 1e-6
GLA_HEADS, GLA_DK, GLA_DV, GLA_RANK, GLA_NORM, GLA_CHUNK = 4, 64, 128, 16, 16.0, 64
LRU_W, LRU_BLOCKS, LRU_CONV, LRU_C = 512, 8, 4, 8.0
DIL_PATTERNS = ((128, 1), (512, 4), (2048, 16))
N_PAT, DIL_HEADS, DIL_DH, ATTN_BLK = 3, 4, 128, 128
ROPE_THETA, ROPE_DIMS = 500000.0, 32
N_BR, BR_W = 3, 512
N_EXP, N_GRP, EXP_PER_GRP, TOP_K, EXP_FF = 16, 4, 4, 2, 512

LANE = 128
GATE_OFF, DIL_OFF, XB_OFF, YB_OFF, Q_OFF, K_OFF, V_OFF, OG_OFF = 0, 3072, 7680, 8192, 8704, 8960, 9216, 9728
PROJ_COLS = 10240
PROJ_TN = 2048
PROJ_RC = 512
MERGE_TM = 512
MOE_BM = 1024
VMEM_LIMIT = 56 * 1024 * 1024

NT = (((1,), (1,)), ((), ()))


def _cparams(sem):
    return pltpu.CompilerParams(dimension_semantics=sem, vmem_limit_bytes=VMEM_LIMIT)


def _inproj_body(x_ref, g_ref, w_ref, wg_ref, proj_ref, glow_ref, h_scr):
    j = pl.program_id(1)
    nrc = SEQ // PROJ_RC

    @pl.when(j == 0)
    def _():
        for c in range(nrc):
            rows = pl.ds(c * PROJ_RC, PROJ_RC)
            x = x_ref[0, rows, :]
            ms = jnp.mean(x * x, axis=-1, keepdims=True)
            h = ((x * lax.rsqrt(ms + RMS_EPS)) * g_ref[...]).astype(bf16)
            h_scr[rows, :] = h
            glow_ref[0, rows, :] = jnp.dot(h, wg_ref[...], preferred_element_type=f32)

    for c in range(nrc):
        rows = pl.ds(c * PROJ_RC, PROJ_RC)
        res = jnp.dot(h_scr[rows, :], w_ref[...], preferred_element_type=f32).astype(bf16)
        for t in range(PROJ_TN // LANE):
            proj_ref[0, t, rows, :] = res[:, t * LANE:(t + 1) * LANE]


def _inproj(x, gain, w, wg):
    B = x.shape[0]
    nj = PROJ_COLS // PROJ_TN
    return pl.pallas_call(
        _inproj_body,
        grid=(B, nj),
        in_specs=[
            pl.BlockSpec((1, SEQ, D_MODEL), lambda b, j: (b, 0, 0)),
            pl.BlockSpec((1, D_MODEL), lambda b, j: (0, 0)),
            pl.BlockSpec((D_MODEL, PROJ_TN), lambda b, j: (0, j)),
            pl.BlockSpec((D_MODEL, LANE), lambda b, j: (0, 0)),
        ],
        out_specs=[
            pl.BlockSpec((1, PROJ_TN // LANE, SEQ, LANE), lambda b, j: (b, j, 0, 0)),
            pl.BlockSpec((1, SEQ, LANE), lambda b, j: (b, 0, 0)),
        ],
        out_shape=[
            jax.ShapeDtypeStruct((B, PROJ_COLS // LANE, SEQ, LANE), bf16),
            jax.ShapeDtypeStruct((B, SEQ, LANE), f32),
        ],
        scratch_shapes=[pltpu.VMEM((SEQ, D_MODEL), bf16)],
        compiler_params=_cparams(("parallel", "arbitrary")),
        name="inproj",
    )(x, gain, w, wg)


def _gla_body(q_ref, k_ref, v_ref, og_ref, glow_ref, wh_ref, wl_ref, bg_ref, gain_ref, o_ref, la_scr):
    C = GLA_CHUNK
    HK = GLA_HEADS * GLA_DK
    PR = 512
    for c in range(SEQ // PR):
        rows = pl.ds(c * PR, PR)
        g = glow_ref[0, rows, :]
        gh = g.astype(bf16)
        gl = (g - gh.astype(f32)).astype(bf16)
        z = (jnp.dot(gh, wh_ref[...], preferred_element_type=f32)
             + jnp.dot(gl, wh_ref[...], preferred_element_type=f32)
             + jnp.dot(gh, wl_ref[...], preferred_element_type=f32)) + bg_ref[...]
        la_scr[rows, :] = (jnp.minimum(z, 0.0) - jnp.log1p(jnp.exp(-jnp.abs(z)))) * (1.0 / GLA_NORM)

    causal = (lax.broadcasted_iota(i32, (C, GLA_HEADS * C), 0)
              >= lax.broadcasted_iota(i32, (C, GLA_HEADS * C), 1) % C)
    tri = (lax.broadcasted_iota(i32, (C, C), 0) >= lax.broadcasted_iota(i32, (C, C), 1)).astype(bf16)
    lane_head = lax.broadcasted_iota(i32, (1, HK), 1) // GLA_DK
    gain = gain_ref[...]
    zero_v = jnp.zeros((C, GLA_DV), bf16)

    def chunk(n, carry):
        rows = pl.ds(pl.multiple_of(n * C, C), C)
        la = la_scr[rows, :]
        p1 = la.astype(bf16)
        r1 = la - p1.astype(f32)
        p2 = r1.astype(bf16)
        p3 = (r1 - p2.astype(f32)).astype(bf16)
        c3 = jnp.dot(tri, jnp.concatenate([p1, p2, p3], axis=1), preferred_element_type=f32)
        cum = (c3[:, :HK] + c3[:, HK:2 * HK]) + c3[:, 2 * HK:]
        cum_t = cum.T
        tot_t = cum_t[:, C - 1:C]
        qf = jnp.concatenate([q_ref[0, t, rows, :] for t in range(HK // LANE)], axis=1).astype(f32)
        kf = jnp.concatenate([k_ref[0, t, rows, :] for t in range(HK // LANE)], axis=1).astype(f32)
        qd = (qf * (GLA_DK ** -0.5) * jnp.exp(cum)).astype(bf16)
        ki = kf * jnp.exp(-cum)
        kd_t = (kf.T * jnp.exp(tot_t - cum_t)).astype(bf16)
        dec_t = jnp.exp(tot_t)
        heads = range(GLA_HEADS)
        vcols = [v_ref[0, h, rows, :] for h in heads]
        k_st = jnp.concatenate([jnp.where(lane_head == h, ki, 0.0).astype(bf16) for h in heads], axis=0)
        v_bd = jnp.concatenate([jnp.concatenate([vcols[h] if g == h else zero_v for g in heads], axis=1)
                                for h in heads], axis=0)
        s = lax.dot_general(qd, k_st, NT, preferred_element_type=f32)
        s = jnp.where(causal, s, 0.0).astype(bf16)
        zero_s = jnp.zeros((GLA_DK, GLA_DV), bf16)
        st_bd = jnp.concatenate([jnp.concatenate([carry[h].astype(bf16) if g == h else zero_s for g in heads], axis=1)
                                 for h in heads], axis=0)
        o = jnp.dot(s, v_bd, preferred_element_type=f32) + jnp.dot(qd, st_bd, preferred_element_type=f32)
        new = []
        for h in heads:
            cols = slice(h * GLA_DV, (h + 1) * GLA_DV)
            hr = slice(h * GLA_DK, (h + 1) * GLA_DK)
            new.append(carry[h] * dec_t[hr, :] + jnp.dot(kd_t[hr, :], vcols[h], preferred_element_type=f32))
            oh = o[:, cols]
            oh = oh * lax.rsqrt(jnp.mean(oh * oh, axis=-1, keepdims=True) + RMS_EPS) * gain
            g = og_ref[0, h, rows, :].astype(f32)
            o_ref[0, rows, cols] = (oh * (g * jax.nn.sigmoid(g))).astype(bf16)
        return tuple(new)

    lax.fori_loop(0, SEQ // C, chunk, tuple(jnp.zeros((GLA_DK, GLA_DV), f32) for _ in range(GLA_HEADS)), unroll=8)


def _gla(proj, glow, wup, bg, gain):
    B = proj.shape[0]
    HK = GLA_HEADS * GLA_DK
    HV = GLA_HEADS * GLA_DV
    wup_hi = wup.astype(bf16)
    wup_lo = (wup - wup_hi.astype(f32)).astype(bf16)
    return pl.pallas_call(
        _gla_body,
        grid=(B,),
        in_specs=[
            pl.BlockSpec((1, HK // LANE, SEQ, LANE), lambda b: (b, Q_OFF // HK, 0, 0)),
            pl.BlockSpec((1, HK // LANE, SEQ, LANE), lambda b: (b, K_OFF // HK, 0, 0)),
            pl.BlockSpec((1, HV // LANE, SEQ, LANE), lambda b: (b, V_OFF // HV, 0, 0)),
            pl.BlockSpec((1, HV // LANE, SEQ, LANE), lambda b: (b, OG_OFF // HV, 0, 0)),
            pl.BlockSpec((1, SEQ, LANE), lambda b: (b, 0, 0)),
            pl.BlockSpec((LANE, HK), lambda b: (0, 0)),
            pl.BlockSpec((LANE, HK), lambda b: (0, 0)),
            pl.BlockSpec((1, HK), lambda b: (0, 0)),
            pl.BlockSpec((1, GLA_DV), lambda b: (0, 0)),
        ],
        out_specs=pl.BlockSpec((1, SEQ, HV), lambda b: (b, 0, 0)),
        out_shape=jax.ShapeDtypeStruct((B, SEQ, HV), bf16),
        scratch_shapes=[
            pltpu.VMEM((SEQ, HK), f32),
        ],
        compiler_params=_cparams(("parallel",)),
        name="gla",
    )(proj, proj, proj, proj, glow, wup_hi, wup_lo, bg, gain)


LRU_GRP = 8
LRU_RC = 512
LRU_PAD = 8
LRU_NT = LRU_W // LANE


def _lru_body(xb_ref, yb_ref, cw_ref, cb_ref, w_ref, b_ref, lam_ref, o_ref, xpad, a_scr, h_scr):
    grp = lax.broadcasted_iota(i32, (LRU_RC // LRU_GRP, LRU_GRP, LANE), 1)
    for t in range(LRU_NT):
        lanes = slice(t * LANE, (t + 1) * LANE)
        xpad[pl.ds(0, LRU_PAD), :] = jnp.zeros((LRU_PAD, LANE), f32)
        xpad[pl.ds(LRU_PAD, SEQ), :] = xb_ref[t].astype(f32)
        nl = -lam_ref[:, lanes]
        sp = jnp.maximum(nl, 0.0) + jnp.log1p(jnp.exp(-jnp.abs(nl)))
        for c in range(SEQ // LRU_RC):
            r0 = c * LRU_RC
            xc = cb_ref[:, lanes]
            for j in range(LRU_CONV):
                xc = xc + xpad[pl.ds(LRU_PAD - j + r0, LRU_RC), :] * cw_ref[LRU_CONV - 1 - j:LRU_CONV - j, lanes]
            z = jnp.dot(xc.astype(bf16), w_ref[t], preferred_element_type=f32) + b_ref[t]
            r = jax.nn.sigmoid(z[:, :LANE])
            gi = jax.nn.sigmoid(z[:, LANE:])
            log_a = (-LRU_C) * r * sp
            a = jnp.exp(log_a)
            u = jnp.sqrt(jnp.tanh(-log_a) * (a * a + 1.0)) * gi * xc
            a = a.reshape(LRU_RC // LRU_GRP, LRU_GRP, LANE)
            u = u.reshape(LRU_RC // LRU_GRP, LRU_GRP, LANE)
            k = 1
            while k < LRU_GRP:
                inside = grp >= k
                u = u + a * jnp.where(inside, pltpu.roll(u, k, 1), 0.0)
                a = a * jnp.where(inside, pltpu.roll(a, k, 1), 1.0)
                k *= 2
            a_scr[t, pl.ds(r0, LRU_RC), :] = a.reshape(LRU_RC, LANE)
            h_scr[t, pl.ds(r0, LRU_RC), :] = u.reshape(LRU_RC, LANE)

    def group(g, carry):
        rows = pl.ds(pl.multiple_of(g * LRU_GRP, LRU_GRP), LRU_GRP)
        out = []
        for t in range(LRU_NT):
            h = h_scr[t, rows, :] + a_scr[t, rows, :] * jnp.broadcast_to(carry[t], (LRU_GRP, LANE))
            h_scr[t, rows, :] = h
            out.append(h[LRU_GRP - 1:LRU_GRP, :])
        return tuple(out)

    lax.fori_loop(0, SEQ // LRU_GRP, group, tuple(jnp.zeros((1, LANE), f32) for _ in range(LRU_NT)), unroll=8)
    for t in range(LRU_NT):
        for c in range(SEQ // LRU_RC):
            rows = pl.ds(c * LRU_RC, LRU_RC)
            o_ref[0, rows, t * LANE:(t + 1) * LANE] = (h_scr[t, rows, :]
                                                       * jax.nn.gelu(yb_ref[t, rows, :].astype(f32))).astype(bf16)


def _lru(proj, cw, cb, w, b, lam):
    B = proj.shape[0]
    full = lambda shape: pl.BlockSpec(shape, lambda b: (0,) * len(shape))
    return pl.pallas_call(
        _lru_body,
        grid=(B,),
        in_specs=[
            pl.BlockSpec((None, LRU_NT, SEQ, LANE), lambda b: (b, XB_OFF // LRU_W, 0, 0)),
            pl.BlockSpec((None, LRU_NT, SEQ, LANE), lambda b: (b, YB_OFF // LRU_W, 0, 0)),
            full((LRU_CONV, LRU_W)), full((1, LRU_W)), full((LRU_NT, LANE, 2 * LANE)), full((LRU_NT, 1, 2 * LANE)),
            full((1, LRU_W)),
        ],
        out_specs=pl.BlockSpec((1, SEQ, LRU_W), lambda b: (b, 0, 0)),
        out_shape=jax.ShapeDtypeStruct((B, SEQ, LRU_W), bf16),
        scratch_shapes=[
            pltpu.VMEM((SEQ + LRU_PAD, LANE), f32),
            pltpu.VMEM((LRU_NT, SEQ, LANE), f32),
            pltpu.VMEM((LRU_NT, SEQ, LANE), f32),
        ],
        compiler_params=_cparams(("parallel",)),
        name="lru",
    )(proj, proj, cw, cb, w, b, lam)


def _rope_body(pos_ref, freq_ref, cos_ref, ss_ref):
    ang = pos_ref[0] * freq_ref[...]
    lane = lax.broadcasted_iota(i32, (SEQ, LANE), 1)
    sn = jnp.sin(ang)
    cos_ref[0] = jnp.cos(ang)
    ss_ref[0] = jnp.where(lane < ROPE_DIMS // 2, -sn, sn)


def _rope_tables(posf, freq):
    B = posf.shape[0]
    spec = pl.BlockSpec((1, SEQ, LANE), lambda b: (b, 0, 0))
    return pl.pallas_call(
        _rope_body,
        grid=(B,),
        in_specs=[pl.BlockSpec((1, SEQ, 1), lambda b: (b, 0, 0)), pl.BlockSpec((1, LANE), lambda b: (0, 0))],
        out_specs=[spec, spec],
        out_shape=[jax.ShapeDtypeStruct((B, SEQ, LANE), f32)] * 2,
        compiler_params=_cparams(("parallel",)),
        name="rope_tables",
    )(posf, freq)


def _rope_partner():
    half = ROPE_DIMS // 2
    perm = np.arange(LANE)
    perm[:half] += half
    perm[half:ROPE_DIMS] -= half
    return perm


ATTN_PC = 256
ATTN_G = 8


def _attn_body(q0, q1, q2, k0, k1, k2, v0, v1, v2, cos_ref, ss_ref, g_ref, p2_ref, o_ref,
               qd, kd, vd, tq, tk, tv, od, ld, on, ln, sc):
    q_refs, k_refs, v_refs = (q0, q1, q2), (k0, k1, k2), (v0, v1, v2)
    BLK = ATTN_BLK
    qg = g_ref[0:1, :] * (DIL_DH ** -0.5)
    qgr = g_ref[1:2, :] * (DIL_DH ** -0.5)
    kg = g_ref[2:3, :]
    kgr = g_ref[3:4, :]
    for p in range(N_PAT):
        kd[p, pl.ds(0, BLK), :] = jnp.zeros((BLK, LANE), bf16)
        vd[p, pl.ds(0, BLK), pl.ds(0, LANE)] = jnp.zeros((BLK, LANE), bf16)
        vd[p, :, pl.ds(LANE, LANE)] = jnp.ones((SEQ + BLK, LANE), bf16)

    qi = lax.broadcasted_iota(i32, (BLK, 2 * BLK), 0)
    kj = lax.broadcasted_iota(i32, (BLK, 2 * BLK), 1)
    is_cur = kj >= BLK
    band = jnp.where(is_cur, kj - BLK, qi) <= jnp.where(is_cur, qi, kj)

    def norm_rope(raw, partner, gain, gain_p, cs, ss):
        rf = raw.astype(f32)
        r = lax.rsqrt(jnp.mean(rf * rf, axis=-1, keepdims=True) + RMS_EPS)
        return (rf * (gain * cs) + partner * (gain_p * ss)) * r

    for p, (_, dil) in enumerate(DIL_PATTERNS):
        L = SEQ // dil
        nbl = L // BLK
        qd_p, kd_p, vd_p, sc_a, sc_b = qd.at[p], kd.at[p], vd.at[p], sc.at[2 * p], sc.at[2 * p + 1]
        if dil > 1:
            tq_p, tk_p, tv_p, od_p, ld_p = tq.at[p - 1], tk.at[p - 1], tv.at[p - 1], od.at[p - 1], ld.at[p - 1]

        for c in range(SEQ // ATTN_PC):
            rows = pl.ds(c * ATTN_PC, ATTN_PC)
            cs, ss = cos_ref[0, rows, :], ss_ref[0, rows, :]
            qr = q_refs[p][0, rows, :]
            kr = k_refs[p][0, rows, :]
            partner = jnp.dot(jnp.concatenate([qr, kr], axis=1), p2_ref[...], preferred_element_type=f32)
            q = norm_rope(qr, partner[:, :LANE], qg, qgr, cs, ss)
            k = norm_rope(kr, partner[:, LANE:], kg, kgr, cs, ss)
            if dil == 1:
                orow = pl.ds(c * ATTN_PC + BLK, ATTN_PC)
                qd_p[rows, :] = q.astype(bf16)
                kd_p[orow, :] = k.astype(bf16)
                vd_p[orow, pl.ds(0, LANE)] = v_refs[p][0, rows, :]
            else:
                tq_p[rows, :] = q
                tk_p[rows, :] = k
                tv_p[rows, :] = v_refs[p][0, rows, :].astype(f32)
        if dil > 1:
            for r in range(dil):
                src = pl.ds(r, L, stride=dil)
                qd_p[pl.ds(r * L, L), :] = tq_p[src, :].astype(bf16)
                kd_p[pl.ds(BLK + r * L, L), :] = tk_p[src, :].astype(bf16)
                vd_p[pl.ds(BLK + r * L, L), pl.ds(0, LANE)] = tv_p[src, :].astype(bf16)

        o_dst, l_dst = (on.at[p], ln.at[p]) if dil == 1 else (od_p, ld_p)

        def scores(g, dst, qd_p=qd_p, kd_p=kd_p):
            for j in range(ATTN_G):
                r0 = pl.multiple_of((g * ATTN_G + j) * BLK, BLK)
                dst[pl.ds(j * BLK, BLK), :] = lax.dot_general(qd_p[pl.ds(r0, BLK), :], kd_p[pl.ds(r0, 2 * BLK), :], NT,
                                                              preferred_element_type=f32)

        def finish(g, src, nbl=nbl, o_dst=o_dst, l_dst=l_dst, vd=vd_p):
            for j in range(ATTN_G):
                b = g * ATTN_G + j
                r0 = pl.multiple_of(b * BLK, BLK)
                first = (b % nbl) == 0
                valid = band & (is_cur | jnp.logical_not(first))
                s = jnp.where(valid, src[pl.ds(j * BLK, BLK), :], -jnp.inf)
                m = jnp.max(s, axis=-1, keepdims=True)
                e = jnp.exp(s - m).astype(bf16)
                ov = jnp.dot(e, vd[pl.ds(r0, 2 * BLK), :], preferred_element_type=f32)
                den = ov[:, LANE:]
                o_dst[pl.ds(r0, BLK), :] = ov[:, :LANE] / den
                l_dst[pl.ds(r0, BLK), :] = m + jnp.log(den)

        n_groups = SEQ // BLK // ATTN_G
        scores(0, sc_a)

        def pair(i, carry, scores=scores, finish=finish, sc_a=sc_a, sc_b=sc_b):
            scores(2 * i + 1, sc_b)
            finish(2 * i, sc_a)
            if n_groups > 2:
                scores(jnp.minimum(2 * i + 2, n_groups - 1), sc_a)
            finish(2 * i + 1, sc_b)
            return carry

        lax.fori_loop(0, n_groups // 2, pair, 0)
        if dil > 1:
            for r in range(dil):
                dst = pl.ds(r, L, stride=dil)
                on[p, dst, :] = od_p[pl.ds(r * L, L), :]
                ln[p, dst, :] = ld_p[pl.ds(r * L, L), :]

    def combine(c, carry):
        rows = pl.ds(pl.multiple_of(c * ATTN_PC, ATTN_PC), ATTN_PC)
        l0, l1, l2 = ln[0, rows, :], ln[1, rows, :], ln[2, rows, :]
        mx = jnp.maximum(jnp.maximum(l0, l1), l2)
        w0, w1, w2 = jnp.exp(l0 - mx), jnp.exp(l1 - mx), jnp.exp(l2 - mx)
        o = (w0 * on[0, rows, :] + w1 * on[1, rows, :] + w2 * on[2, rows, :]) / (w0 + w1 + w2)
        o_ref[0, rows, :] = o.astype(bf16)
        return carry

    lax.fori_loop(0, SEQ // ATTN_PC, combine, 0)


def _attn(proj, cos, ss, qg, kg):
    B = proj.shape[0]
    base = DIL_OFF // LANE
    perm = _rope_partner()
    gains = jnp.stack([qg, qg[perm], kg, kg[perm]]).astype(f32)
    pm = np.zeros((LANE, LANE), np.float32)
    pm[perm[:ROPE_DIMS], np.arange(ROPE_DIMS)] = 1.0
    p2 = jnp.asarray(np.kron(np.eye(2, dtype=np.float32), pm), dtype=bf16)

    def pspec(qkv, p):
        return pl.BlockSpec((None, 1, SEQ, LANE), lambda b, h, o=base + (qkv * N_PAT + p) * DIL_HEADS: (b, o + h, 0, 0))

    tspec = pl.BlockSpec((1, SEQ, LANE), lambda b, h: (b, 0, 0))
    in_specs = [pspec(qkv, p) for qkv in range(3) for p in range(N_PAT)] + [
        tspec, tspec, pl.BlockSpec((4, LANE), lambda b, h: (0, 0)), pl.BlockSpec((2 * LANE, 2 * LANE), lambda b, h: (0, 0))]
    return pl.pallas_call(
        _attn_body,
        grid=(B, DIL_HEADS),
        in_specs=in_specs,
        out_specs=pl.BlockSpec((1, SEQ, LANE), lambda b, h: (b, 0, h)),
        out_shape=jax.ShapeDtypeStruct((B, SEQ, DIL_HEADS * DIL_DH), bf16),
        scratch_shapes=[
            pltpu.VMEM((N_PAT, SEQ, LANE), bf16),
            pltpu.VMEM((N_PAT, SEQ + ATTN_BLK, LANE), bf16),
            pltpu.VMEM((N_PAT, SEQ + ATTN_BLK, 2 * LANE), bf16),
            pltpu.VMEM((N_PAT - 1, SEQ, LANE), f32),
            pltpu.VMEM((N_PAT - 1, SEQ, LANE), f32),
            pltpu.VMEM((N_PAT - 1, SEQ, LANE), f32),
            pltpu.VMEM((N_PAT - 1, SEQ, LANE), f32),
            pltpu.VMEM((N_PAT - 1, SEQ, LANE), f32),
            pltpu.VMEM((N_PAT, SEQ, LANE), f32),
            pltpu.VMEM((N_PAT, SEQ, LANE), f32),
            pltpu.VMEM((2 * N_PAT, ATTN_G * ATTN_BLK, 2 * ATTN_BLK), f32),
        ],
        compiler_params=_cparams(("parallel", "arbitrary")),
        name="dilated_attn",
    )(*([proj] * 9), cos, ss, gains, p2)


def _merge_body(a_ref, b_ref, c_ref, g0_ref, g1_ref, g2_ref, x_ref, wb_ref, wo_ref, n2_ref, rw_ref, rb_ref, tri_ref,
                x1_ref, h2_ref, ei_ref, ew_ref, pos_ref, cnt_ref, cb_ref, cnt_scr):
    merged = None
    for n, (br, g) in enumerate(((a_ref, g0_ref), (b_ref, g1_ref), (c_ref, g2_ref))):
        logits = jnp.concatenate([g[t] for t in range(D_MODEL // LANE)], axis=1)
        gate = 0.5 * jnp.tanh(0.5 * logits.astype(f32)) + 0.5
        t = gate * jnp.dot(br[...], wb_ref[n], preferred_element_type=f32)
        merged = t if merged is None else merged + t
    x1 = x_ref[...] + jnp.dot(merged.astype(bf16), wo_ref[...], preferred_element_type=f32)
    x1_ref[...] = x1
    h2 = ((x1 * lax.rsqrt(jnp.mean(x1 * x1, axis=-1, keepdims=True) + RMS_EPS)) * n2_ref[...]).astype(bf16)
    h2_ref[...] = h2

    lg = lax.dot_general(rw_ref[...], h2, NT, preferred_element_type=f32) + rb_ref[...]
    ex = jnp.exp(lg - jnp.max(lg, axis=0, keepdims=True))
    sc = ex / jnp.sum(ex, axis=0, keepdims=True)
    srow = [sc[e:e + 1, :] for e in range(N_EXP)]
    best = bidx = None
    for gidx in range(N_GRP):
        s0, s1, s2, s3 = srow[4 * gidx:4 * gidx + 4]
        hi01, lo01, hi23, lo23 = jnp.maximum(s0, s1), jnp.minimum(s0, s1), jnp.maximum(s2, s3), jnp.minimum(s2, s3)
        gs = jnp.maximum(hi01, hi23) + jnp.maximum(jnp.minimum(hi01, hi23), jnp.maximum(lo01, lo23))
        if best is None:
            best, bidx = gs, jnp.zeros_like(gs, dtype=i32)
        else:
            better = gs > best
            best = jnp.where(better, gs, best)
            bidx = jnp.where(better, gidx, bidx)
    cand = []
    for i in range(EXP_PER_GRP):
        v = srow[i]
        for gidx in range(1, N_GRP):
            v = jnp.where(bidx == gidx, srow[4 * gidx + i], v)
        cand.append(v)

    def argmax4(vals):
        bw, bi = vals[0], jnp.zeros_like(bidx)
        for i in range(1, EXP_PER_GRP):
            better = vals[i] > bw
            bw = jnp.where(better, vals[i], bw)
            bi = jnp.where(better, i, bi)
        return bw, bi

    w1, i1 = argmax4(cand)
    w2, i2 = argmax4([jnp.where(i1 == i, -jnp.inf, cand[i]) for i in range(EXP_PER_GRP)])
    tot = w1 + w2
    e1 = bidx * EXP_PER_GRP + i1
    e2 = bidx * EXP_PER_GRP + i2
    ei_ref[0:1, :] = e1
    ei_ref[1:2, :] = e2
    ew_ref[0:1, :] = w1 / tot
    ew_ref[1:2, :] = w2 / tot

    @pl.when(pl.program_id(0) == 0)
    def _():
        cnt_scr[...] = jnp.zeros_like(cnt_scr)

    erow = lax.broadcasted_iota(i32, (N_EXP, MERGE_TM), 0)
    oh1 = erow == e1
    oh2 = erow == e2
    both = jnp.where(oh1 | oh2, 1.0, 0.0)
    incl = jnp.dot(both.astype(bf16), tri_ref[...], preferred_element_type=f32)
    cb_ref[0] = cnt_scr[...]
    base = cnt_scr[...] + (incl - both)
    pos_ref[0:1, :] = jnp.sum(jnp.where(oh1, base, 0.0), axis=0, keepdims=True).astype(i32)
    pos_ref[1:2, :] = jnp.sum(jnp.where(oh2, base, 0.0), axis=0, keepdims=True).astype(i32)
    cnt = cnt_scr[...] + incl[:, MERGE_TM - 1:MERGE_TM]
    cnt_scr[...] = cnt
    cnt_ref[...] = cnt


def _merge(bra, brb, brc, proj, x2d, wb, wo, n2, rw, rb):
    T = bra.shape[0]
    TM = MERGE_TM
    per_b = SEQ // TM
    gt = D_MODEL // LANE

    def gspec(n):
        return pl.BlockSpec((None, gt, TM, LANE), lambda i: (i // per_b, GATE_OFF // D_MODEL + n, i % per_b, 0))

    rspec = lambda w: pl.BlockSpec((TM, w), lambda i: (i, 0))
    full = lambda shape: pl.BlockSpec(shape, lambda i: (0,) * len(shape))
    tri = (jnp.arange(TM)[:, None] <= jnp.arange(TM)[None, :]).astype(bf16)
    return pl.pallas_call(
        _merge_body,
        grid=(T // TM,),
        in_specs=[
            rspec(BR_W), rspec(BR_W), rspec(BR_W),
            gspec(0), gspec(1), gspec(2),
            rspec(D_MODEL),
            full((N_BR, BR_W, D_MODEL)), full((D_MODEL, D_MODEL)), full((1, D_MODEL)),
            full((N_EXP, D_MODEL)), full((N_EXP, 1)), full((TM, TM)),
        ],
        out_specs=[
            rspec(D_MODEL), rspec(D_MODEL),
            pl.BlockSpec((TOP_K, TM), lambda i: (0, i)),
            pl.BlockSpec((TOP_K, TM), lambda i: (0, i)),
            pl.BlockSpec((TOP_K, TM), lambda i: (0, i)),
            full((N_EXP, 1)),
            pl.BlockSpec((1, N_EXP, 1), lambda i: (i, 0, 0)),
        ],
        out_shape=[
            jax.ShapeDtypeStruct((T, D_MODEL), f32),
            jax.ShapeDtypeStruct((T, D_MODEL), bf16),
            jax.ShapeDtypeStruct((TOP_K, T), i32),
            jax.ShapeDtypeStruct((TOP_K, T), f32),
            jax.ShapeDtypeStruct((TOP_K, T), i32),
            jax.ShapeDtypeStruct((N_EXP, 1), f32),
            jax.ShapeDtypeStruct((T // TM, N_EXP, 1), f32),
        ],
        scratch_shapes=[pltpu.VMEM((N_EXP, 1), f32)],
        compiler_params=_cparams(("arbitrary",)),
        name="merge_router",
    )(bra, brb, brc, proj, proj, proj, x2d, wb, wo, n2, rw, rb, tri)


def _expert_body(be_ref, nu_ref, xs_ref, wgu_ref, wd_ref, ys_ref):
    i = pl.program_id(0)

    @pl.when(i < nu_ref[0])
    def _():
        gu = jnp.dot(xs_ref[...], wgu_ref[0], preferred_element_type=f32)
        g = gu[:, :EXP_FF]
        act = (g * jax.nn.sigmoid(g)) * gu[:, EXP_FF:]
        ys_ref[...] = jnp.dot(act.astype(bf16), wd_ref[0], preferred_element_type=f32).astype(bf16)

    @pl.when(i >= nu_ref[0])
    def _():
        ys_ref[...] = jnp.zeros_like(ys_ref)


def _experts(block_e, n_used, xs, wgu, wd):
    n_rows = xs.shape[0]
    grid_spec = pltpu.PrefetchScalarGridSpec(
        num_scalar_prefetch=2,
        grid=(n_rows // MOE_BM,),
        in_specs=[
            pl.BlockSpec((MOE_BM, D_MODEL), lambda i, be, nu: (jnp.minimum(i, nu[0] - 1), 0)),
            pl.BlockSpec((1, D_MODEL, 2 * EXP_FF), lambda i, be, nu: (be[i], 0, 0)),
            pl.BlockSpec((1, EXP_FF, D_MODEL), lambda i, be, nu: (be[i], 0, 0)),
        ],
        out_specs=pl.BlockSpec((MOE_BM, D_MODEL), lambda i, be, nu: (i, 0)),
    )
    return pl.pallas_call(
        _expert_body,
        grid_spec=grid_spec,
        out_shape=jax.ShapeDtypeStruct((n_rows, D_MODEL), bf16),
        compiler_params=_cparams(("arbitrary",)),
        name="experts",
    )(block_e, n_used, xs, wgu, wd)


COMB_TM = 1024


def _combine_body(x1_ref, y0_ref, y1_ref, w_ref, o_ref):
    w = w_ref[...]
    o_ref[...] = x1_ref[...] + y0_ref[...].astype(f32) * w[:, 0:1] + y1_ref[...].astype(f32) * w[:, 1:2]


def _combine(x1, y0, y1, w):
    T = x1.shape[0]
    rspec = pl.BlockSpec((COMB_TM, D_MODEL), lambda i: (i, 0))
    return pl.pallas_call(
        _combine_body,
        grid=(T // COMB_TM,),
        in_specs=[rspec, rspec, rspec, pl.BlockSpec((COMB_TM, TOP_K), lambda i: (i, 0))],
        out_specs=rspec,
        out_shape=jax.ShapeDtypeStruct((T, D_MODEL), f32),
        compiler_params=_cparams(("parallel",)),
        name="moe_combine",
    )(x1, y0, y1, w)


DISP_ALIGN = 16
DISP_SLOTS = MERGE_TM * TOP_K + N_EXP * DISP_ALIGN
DISP_SIZES = tuple(DISP_ALIGN << s for s in range(5, -1, -1))
TAIL_SIZES = tuple(sz for sz in (MOE_BM // 2 >> s for s in range(12)) if sz >= DISP_ALIGN)


def _pieces(length, sizes, fn):
    done = jnp.int32(0)
    for sz in sizes:
        take = (length & sz) != 0

        @pl.when(take)
        def _(done=done, sz=sz):
            fn(done, sz)

        done = done + jnp.where(take, sz, 0)


def _dispatch_body(off_ref, n_ref, ls_ref, toff_ref, tlen_ref, h2_ref, ei_ref, pos_ref, adj_ref, xs_hbm,
                   srt, zbuf, sem, zsem):
    i = pl.program_id(0)
    last = pl.num_programs(0) - 1
    cur = i % 2
    TM = MERGE_TM

    def slab_copies(step, buf, start):
        for e in range(N_EXP):
            ls = ls_ref[step * N_EXP + e]
            off = off_ref[step * N_EXP + e]

            def piece(done, sz, ls=ls, off=off):
                cp = pltpu.make_async_copy(srt.at[buf, pl.ds(pl.multiple_of(ls + done, DISP_ALIGN), sz)],
                                           xs_hbm.at[pl.ds(pl.multiple_of(off + done, DISP_ALIGN), sz)], sem.at[buf])
                cp.start() if start else cp.wait()

            _pieces(n_ref[step * N_EXP + e], DISP_SIZES, piece)

    def tail_copies(start):
        for e in range(N_EXP):
            off = toff_ref[e]

            def piece(done, sz, off=off):
                cp = pltpu.make_async_copy(zbuf.at[pl.ds(0, sz)],
                                           xs_hbm.at[pl.ds(pl.multiple_of(off + done, DISP_ALIGN), sz)], zsem)
                cp.start() if start else cp.wait()

            _pieces(tlen_ref[e], TAIL_SIZES, piece)

        zrows = zbuf.shape[0]

        def block(c, carry):
            cp = pltpu.make_async_copy(zbuf, xs_hbm.at[pl.ds(pl.multiple_of(toff_ref[N_EXP] + c * zrows, zrows), zrows)],
                                       zsem)
            cp.start() if start else cp.wait()
            return carry

        lax.fori_loop(0, tlen_ref[N_EXP] // zrows, block, 0)

    @pl.when(i == 0)
    def _():
        zbuf[...] = jnp.zeros_like(zbuf)
        tail_copies(True)

    @pl.when(i >= 2)
    def _():
        slab_copies(i - 2, cur, False)

    erow = lax.broadcasted_iota(i32, (N_EXP, TM), 0)
    adj = adj_ref[0]

    def slot(k):
        sel = jnp.sum(jnp.where(erow == ei_ref[k:k + 1, :], adj, 0.0), axis=0, keepdims=True)
        return pos_ref[k:k + 1, :] + sel.astype(i32)

    srow = lax.broadcasted_iota(i32, (DISP_SLOTS, TM), 0)
    perm = jnp.where((srow == slot(0)) | (srow == slot(1)), 1.0, 0.0).astype(bf16)
    srt[cur] = jnp.dot(perm, h2_ref[...], preferred_element_type=f32).astype(bf16)
    slab_copies(i, cur, True)

    @pl.when(i == last)
    def _():
        slab_copies(i, cur, False)

        @pl.when(i >= 1)
        def _():
            slab_copies(i - 1, 1 - cur, False)

        tail_copies(False)


def _dispatch(off, n16, lstart, tail_off, tail_len, h2, eidx, pos, adj, n_rows):
    T = h2.shape[0]
    TM = MERGE_TM
    grid_spec = pltpu.PrefetchScalarGridSpec(
        num_scalar_prefetch=5,
        grid=(T // TM,),
        in_specs=[
            pl.BlockSpec((TM, D_MODEL), lambda i, *_: (i, 0)),
            pl.BlockSpec((TOP_K, TM), lambda i, *_: (0, i)),
            pl.BlockSpec((TOP_K, TM), lambda i, *_: (0, i)),
            pl.BlockSpec((1, N_EXP, 1), lambda i, *_: (i, 0, 0)),
        ],
        out_specs=pl.BlockSpec(memory_space=pl.ANY),
        scratch_shapes=[
            pltpu.VMEM((2, DISP_SLOTS, D_MODEL), bf16),
            pltpu.VMEM((TAIL_SIZES[0], D_MODEL), bf16),
            pltpu.SemaphoreType.DMA((2,)),
            pltpu.SemaphoreType.DMA(()),
        ],
    )
    return pl.pallas_call(
        _dispatch_body,
        grid_spec=grid_spec,
        out_shape=jax.ShapeDtypeStruct((n_rows, D_MODEL), bf16),
        compiler_params=_cparams(("arbitrary",)),
        name="moe_dispatch",
    )(off, n16, lstart, tail_off, tail_len, h2, eidx, pos, adj)


def _moe(x1, h2, eidx, ew, pos, cnt, cb, wgu, wd):
    T = x1.shape[0]
    TM = MERGE_TM
    n_tiles = T // TM
    ceil_to = lambda v, m: (v + m - 1) // m * m
    n_rows = ceil_to(T * TOP_K + n_tiles * N_EXP * (DISP_ALIGN - 1) + N_EXP * (MOE_BM - 1), MOE_BM)
    n_blocks = n_rows // MOE_BM
    cbi = cb[:, :, 0].astype(i32)
    tile_cnt = jnp.concatenate([cbi[1:], cnt[:, 0].astype(i32)[None, :]], axis=0) - cbi
    n16 = ceil_to(tile_cnt, DISP_ALIGN)
    lstart = jnp.cumsum(n16, axis=1) - n16
    before = jnp.cumsum(n16, axis=0) - n16
    used = jnp.sum(n16, axis=0)
    padded = ceil_to(used, MOE_BM)
    pad_ends = jnp.cumsum(padded)
    off = (pad_ends - padded)[None, :] + before
    adj = (lstart - cbi).astype(f32)[:, :, None]
    row_adj = jnp.repeat(off - cbi, TM, axis=0).T
    dest = pos
    for e in range(N_EXP):
        dest = dest + jnp.where(eidx == e, row_adj[e][None, :], 0)
    blk_start = jnp.arange(n_blocks, dtype=i32) * MOE_BM
    block_e = jnp.minimum(jnp.sum((blk_start[:, None] >= pad_ends[None, :]).astype(i32), axis=1), N_EXP - 1)
    n_used = (pad_ends[-1:] // MOE_BM).astype(i32)
    tail_off = jnp.concatenate([pad_ends - padded + used, pad_ends[-1:]])
    tail_len = jnp.concatenate([padded - used, n_rows - pad_ends[-1:]])
    xs = _dispatch(off.reshape(-1), n16.reshape(-1), lstart.reshape(-1), tail_off, tail_len, h2, eidx, pos, adj, n_rows)
    ys = _experts(block_e, n_used, xs, wgu, wd)
    return _combine(x1, ys[dest[0]], ys[dest[1]], ew.T)


def _pack_w_in(w):
    cuts = np.cumsum([256, 256, 512, 512, 16, 512, 512, 4608, 3072])[:-1].tolist()
    q, k, v, og, glow, xb, yb, qkv, gates = jnp.split(w, cuts, axis=-1)
    packed = jnp.concatenate([gates, qkv, xb, yb, q, k, v, og], axis=-1).astype(bf16)
    wg = jnp.pad(glow, ((0, 0), (0, LANE - GLA_RANK))).astype(bf16)
    return packed, wg


def _block_diag_lru(w_a, w_x):
    bs = LRU_W // LRU_BLOCKS
    per = LANE // bs
    tiles = []
    for t in range(LRU_W // LANE):
        halves = []
        for w in (w_a, w_x):
            m = jnp.zeros((LANE, LANE), f32)
            for j in range(per):
                m = m.at[j * bs:(j + 1) * bs, j * bs:(j + 1) * bs].set(w[t * per + j])
            halves.append(m)
        tiles.append(jnp.concatenate(halves, axis=1))
    return jnp.stack(tiles).astype(bf16)


def kernel(x, positions, norm1_gain, w_in, gla_gate_up, gla_gate_bias, gla_out_gain, lru_conv_w, lru_conv_b, lru_w_a, lru_b_a, lru_w_x, lru_b_x, lru_lambda, q_norm_gain, k_norm_gain, w_branch, w_out, norm2_gain, router_w, router_b, w_gate, w_up, w_down):
    B, S, D = x.shape
    assert S == SEQ and D == D_MODEL
    T = B * S
    depth = w_in.shape[0]

    half = ROPE_DIMS // 2
    inv_freq = ROPE_THETA ** (-jnp.arange(half, dtype=f32) / half)
    freq = jnp.zeros((1, LANE), f32).at[0, :ROPE_DIMS].set(jnp.concatenate([inv_freq, inv_freq]))
    cos, ss = _rope_tables(positions.astype(f32)[:, :, None], freq)

    rw = router_w.T.astype(bf16)
    rb = router_b.astype(f32)[:, None]
    for l in range(depth):
        w_packed, wg = _pack_w_in(w_in[l])
        wup = jnp.pad(gla_gate_up[l], ((0, LANE - GLA_RANK), (0, 0)))
        nt = LRU_W // LANE
        lru_w = _block_diag_lru(lru_w_a[l], lru_w_x[l])
        lru_b = jnp.concatenate([lru_b_a[l].reshape(nt, 1, LANE), lru_b_x[l].reshape(nt, 1, LANE)], axis=-1)
        wgu = jnp.concatenate([w_gate[l], w_up[l]], axis=-1).astype(bf16)

        proj, glow = _inproj(x, norm1_gain[l][None, :], w_packed, wg)
        br_a = _gla(proj, glow, wup, gla_gate_bias[l][None, :], gla_out_gain[l][None, :])
        br_b = _lru(proj, lru_conv_w[l], lru_conv_b[l][None, :], lru_w, lru_b, lru_lambda[l][None, :])
        br_c = _attn(proj, cos, ss, q_norm_gain[l], k_norm_gain[l])
        routed = _merge(br_a.reshape(T, BR_W), br_b.reshape(T, BR_W), br_c.reshape(T, BR_W), proj, x.reshape(T, D),
                        w_branch[l].astype(bf16), w_out[l].astype(bf16), norm2_gain[l][None, :], rw, rb)
        x = _moe(*routed, wgu, w_down[l].astype(bf16)).reshape(B, S, D)
    return x
```

```python
import jax
import jax.numpy as jnp
import numpy as np
from jax import lax
from jax.experimental import pallas as pl
from jax.experimental.pallas import tpu as pltpu

f32 = jnp.float32
bf16 = jnp.bfloat16
i32 = jnp.int32

D_MODEL = 1024
SEQ = 2048
RMS_EPS = 1e-6
GLA_HEADS, GLA_DK, GLA_DV, GLA_RANK, GLA_NORM, GLA_CHUNK = 4, 64, 128, 16, 16.0, 64
LRU_W, LRU_BLOCKS, LRU_CONV, LRU_C = 512, 8, 4, 8.0
DIL_PATTERNS = ((128, 1), (512, 4), (2048, 16))
N_PAT, DIL_HEADS, DIL_DH, ATTN_BLK = 3, 4, 128, 128
ROPE_THETA, ROPE_DIMS = 500000.0, 32
N_BR, BR_W = 3, 512
N_EXP, N_GRP, EXP_PER_GRP, TOP_K, EXP_FF = 16, 4, 4, 2, 512

LANE = 128
GATE_OFF, DIL_OFF, XB_OFF, YB_OFF, Q_OFF, K_OFF, V_OFF, OG_OFF = 0, 3072, 7680, 8192, 8704, 8960, 9216, 9728
PROJ_COLS = 10240
PROJ_TN = 2048
PROJ_RC = 512
MERGE_TM = 512
MOE_BM = 1024
VMEM_LIMIT = 56 * 1024 * 1024

NT = (((1,), (1,)), ((), ()))


def _cparams(sem):
    return pltpu.CompilerParams(dimension_semantics=sem, vmem_limit_bytes=VMEM_LIMIT)


def _inproj_body(x_ref, g_ref, w_ref, wg_ref, proj_ref, glow_ref, h_scr):
    j = pl.program_id(1)
    nrc = SEQ // PROJ_RC

    @pl.when(j == 0)
    def _():
        for c in range(nrc):
            rows = pl.ds(c * PROJ_RC, PROJ_RC)
            x = x_ref[0, rows, :]
            ms = jnp.mean(x * x, axis=-1, keepdims=True)
            h = ((x * lax.rsqrt(ms + RMS_EPS)) * g_ref[...]).astype(bf16)
            h_scr[rows, :] = h
            glow_ref[0, rows, :] = jnp.dot(h, wg_ref[...], preferred_element_type=f32)

    for c in range(nrc):
        rows = pl.ds(c * PROJ_RC, PROJ_RC)
        res = jnp.dot(h_scr[rows, :], w_ref[...], preferred_element_type=f32).astype(bf16)
        for t in range(PROJ_TN // LANE):
            proj_ref[0, t, rows, :] = res[:, t * LANE:(t + 1) * LANE]


def _inproj(x, gain, w, wg):
    B = x.shape[0]
    nj = PROJ_COLS // PROJ_TN
    return pl.pallas_call(
        _inproj_body,
        grid=(B, nj),
        in_specs=[
            pl.BlockSpec((1, SEQ, D_MODEL), lambda b, j: (b, 0, 0)),
            pl.BlockSpec((1, D_MODEL), lambda b, j: (0, 0)),
            pl.BlockSpec((D_MODEL, PROJ_TN), lambda b, j: (0, j)),
            pl.BlockSpec((D_MODEL, LANE), lambda b, j: (0, 0)),
        ],
        out_specs=[
            pl.BlockSpec((1, PROJ_TN // LANE, SEQ, LANE), lambda b, j: (b, j, 0, 0)),
            pl.BlockSpec((1, SEQ, LANE), lambda b, j: (b, 0, 0)),
        ],
        out_shape=[
            jax.ShapeDtypeStruct((B, PROJ_COLS // LANE, SEQ, LANE), bf16),
            jax.ShapeDtypeStruct((B, SEQ, LANE), f32),
        ],
        scratch_shapes=[pltpu.VMEM((SEQ, D_MODEL), bf16)],
        compiler_params=_cparams(("parallel", "arbitrary")),
        name="inproj",
    )(x, gain, w, wg)


def _gla_body(q_ref, k_ref, v_ref, og_ref, glow_ref, wh_ref, wl_ref, bg_ref, gain_ref, o_ref, la_scr):
    C = GLA_CHUNK
    HK = GLA_HEADS * GLA_DK
    PR = 512
    for c in range(SEQ // PR):
        rows = pl.ds(c * PR, PR)
        g = glow_ref[0, rows, :]
        gh = g.astype(bf16)
        gl = (g - gh.astype(f32)).astype(bf16)
        z = (jnp.dot(gh, wh_ref[...], preferred_element_type=f32)
             + jnp.dot(gl, wh_ref[...], preferred_element_type=f32)
             + jnp.dot(gh, wl_ref[...], preferred_element_type=f32)) + bg_ref[...]
        la_scr[rows, :] = (jnp.minimum(z, 0.0) - jnp.log1p(jnp.exp(-jnp.abs(z)))) * (1.0 / GLA_NORM)

    causal = (lax.broadcasted_iota(i32, (C, GLA_HEADS * C), 0)
              >= lax.broadcasted_iota(i32, (C, GLA_HEADS * C), 1) % C)
    tri = (lax.broadcasted_iota(i32, (C, C), 0) >= lax.broadcasted_iota(i32, (C, C), 1)).astype(bf16)
    lane_head = lax.broadcasted_iota(i32, (1, HK), 1) // GLA_DK
    gain = gain_ref[...]
    zero_v = jnp.zeros((C, GLA_DV), bf16)

    def chunk(n, carry):
        rows = pl.ds(pl.multiple_of(n * C, C), C)
        la = la_scr[rows, :]
        p1 = la.astype(bf16)
        r1 = la - p1.astype(f32)
        p2 = r1.astype(bf16)
        p3 = (r1 - p2.astype(f32)).astype(bf16)
        c3 = jnp.dot(tri, jnp.concatenate([p1, p2, p3], axis=1), preferred_element_type=f32)
        cum = (c3[:, :HK] + c3[:, HK:2 * HK]) + c3[:, 2 * HK:]
        cum_t = cum.T
        tot_t = cum_t[:, C - 1:C]
        qf = jnp.concatenate([q_ref[0, t, rows, :] for t in range(HK // LANE)], axis=1).astype(f32)
        kf = jnp.concatenate([k_ref[0, t, rows, :] for t in range(HK // LANE)], axis=1).astype(f32)
        qd = (qf * (GLA_DK ** -0.5) * jnp.exp(cum)).astype(bf16)
        ki = kf * jnp.exp(-cum)
        kd_t = (kf.T * jnp.exp(tot_t - cum_t)).astype(bf16)
        dec_t = jnp.exp(tot_t)
        heads = range(GLA_HEADS)
        vcols = [v_ref[0, h, rows, :] for h in heads]
        k_st = jnp.concatenate([jnp.where(lane_head == h, ki, 0.0).astype(bf16) for h in heads], axis=0)
        v_bd = jnp.concatenate([jnp.concatenate([vcols[h] if g == h else zero_v for g in heads], axis=1)
                                for h in heads], axis=0)
        s = lax.dot_general(qd, k_st, NT, preferred_element_type=f32)
        s = jnp.where(causal, s, 0.0).astype(bf16)
        zero_s = jnp.zeros((GLA_DK, GLA_DV), bf16)
        st_bd = jnp.concatenate([jnp.concatenate([carry[h].astype(bf16) if g == h else zero_s for g in heads], axis=1)
                                 for h in heads], axis=0)
        o = jnp.dot(s, v_bd, preferred_element_type=f32) + jnp.dot(qd, st_bd, preferred_element_type=f32)
        new = []
        for h in heads:
            cols = slice(h * GLA_DV, (h + 1) * GLA_DV)
            hr = slice(h * GLA_DK, (h + 1) * GLA_DK)
            new.append(carry[h] * dec_t[hr, :] + jnp.dot(kd_t[hr, :], vcols[h], preferred_element_type=f32))
            oh = o[:, cols]
            oh = oh * lax.rsqrt(jnp.mean(oh * oh, axis=-1, keepdims=True) + RMS_EPS) * gain
            g = og_ref[0, h, rows, :].astype(f32)
            o_ref[0, rows, cols] = (oh * (g * jax.nn.sigmoid(g))).astype(bf16)
        return tuple(new)

    lax.fori_loop(0, SEQ // C, chunk, tuple(jnp.zeros((GLA_DK, GLA_DV), f32) for _ in range(GLA_HEADS)), unroll=8)


def _gla(proj, glow, wup, bg, gain):
    B = proj.shape[0]
    HK = GLA_HEADS * GLA_DK
    HV = GLA_HEADS * GLA_DV
    wup_hi = wup.astype(bf16)
    wup_lo = (wup - wup_hi.astype(f32)).astype(bf16)
    return pl.pallas_call(
        _gla_body,
        grid=(B,),
        in_specs=[
            pl.BlockSpec((1, HK // LANE, SEQ, LANE), lambda b: (b, Q_OFF // HK, 0, 0)),
            pl.BlockSpec((1, HK // LANE, SEQ, LANE), lambda b: (b, K_OFF // HK, 0, 0)),
            pl.BlockSpec((1, HV // LANE, SEQ, LANE), lambda b: (b, V_OFF // HV, 0, 0)),
            pl.BlockSpec((1, HV // LANE, SEQ, LANE), lambda b: (b, OG_OFF // HV, 0, 0)),
            pl.BlockSpec((1, SEQ, LANE), lambda b: (b, 0, 0)),
            pl.BlockSpec((LANE, HK), lambda b: (0, 0)),
            pl.BlockSpec((LANE, HK), lambda b: (0, 0)),
            pl.BlockSpec((1, HK), lambda b: (0, 0)),
            pl.BlockSpec((1, GLA_DV), lambda b: (0, 0)),
        ],
        out_specs=pl.BlockSpec((1, SEQ, HV), lambda b: (b, 0, 0)),
        out_shape=jax.ShapeDtypeStruct((B, SEQ, HV), bf16),
        scratch_shapes=[
            pltpu.VMEM((SEQ, HK), f32),
        ],
        compiler_params=_cparams(("parallel",)),
        name="gla",
    )(proj, proj, proj, proj, glow, wup_hi, wup_lo, bg, gain)


LRU_GRP = 8
LRU_RC = 512
LRU_PAD = 8
LRU_NT = LRU_W // LANE


def _lru_body(xb_ref, yb_ref, cw_ref, cb_ref, w_ref, b_ref, lam_ref, o_ref, xpad, a_scr, h_scr):
    grp = lax.broadcasted_iota(i32, (LRU_RC // LRU_GRP, LRU_GRP, LANE), 1)
    for t in range(LRU_NT):
        lanes = slice(t * LANE, (t + 1) * LANE)
        xpad[pl.ds(0, LRU_PAD), :] = jnp.zeros((LRU_PAD, LANE), f32)
        xpad[pl.ds(LRU_PAD, SEQ), :] = xb_ref[t].astype(f32)
        nl = -lam_ref[:, lanes]
        sp = jnp.maximum(nl, 0.0) + jnp.log1p(jnp.exp(-jnp.abs(nl)))
        for c in range(SEQ // LRU_RC):
            r0 = c * LRU_RC
            xc = cb_ref[:, lanes]
            for j in range(LRU_CONV):
                xc = xc + xpad[pl.ds(LRU_PAD - j + r0, LRU_RC), :] * cw_ref[LRU_CONV - 1 - j:LRU_CONV - j, lanes]
            z = jnp.dot(xc.astype(bf16), w_ref[t], preferred_element_type=f32) + b_ref[t]
            r = jax.nn.sigmoid(z[:, :LANE])
            gi = jax.nn.sigmoid(z[:, LANE:])
            log_a = (-LRU_C) * r * sp
            a = jnp.exp(log_a)
            u = jnp.sqrt(jnp.tanh(-log_a) * (a * a + 1.0)) * gi * xc
            a = a.reshape(LRU_RC // LRU_GRP, LRU_GRP, LANE)
            u = u.reshape(LRU_RC // LRU_GRP, LRU_GRP, LANE)
            k = 1
            while k < LRU_GRP:
                inside = grp >= k
                u = u + a * jnp.where(inside, pltpu.roll(u, k, 1), 0.0)
                a = a * jnp.where(inside, pltpu.roll(a, k, 1), 1.0)
                k *= 2
            a_scr[t, pl.ds(r0, LRU_RC), :] = a.reshape(LRU_RC, LANE)
            h_scr[t, pl.ds(r0, LRU_RC), :] = u.reshape(LRU_RC, LANE)

    def group(g, carry):
        rows = pl.ds(pl.multiple_of(g * LRU_GRP, LRU_GRP), LRU_GRP)
        out = []
        for t in range(LRU_NT):
            h = h_scr[t, rows, :] + a_scr[t, rows, :] * jnp.broadcast_to(carry[t], (LRU_GRP, LANE))
            h_scr[t, rows, :] = h
            out.append(h[LRU_GRP - 1:LRU_GRP, :])
        return tuple(out)

    lax.fori_loop(0, SEQ // LRU_GRP, group, tuple(jnp.zeros((1, LANE), f32) for _ in range(LRU_NT)), unroll=8)
    for t in range(LRU_NT):
        for c in range(SEQ // LRU_RC):
            rows = pl.ds(c * LRU_RC, LRU_RC)
            o_ref[0, rows, t * LANE:(t + 1) * LANE] = (h_scr[t, rows, :]
                                                       * jax.nn.gelu(yb_ref[t, rows, :].astype(f32))).astype(bf16)


def _lru(proj, cw, cb, w, b, lam):
    B = proj.shape[0]
    full = lambda shape: pl.BlockSpec(shape, lambda b: (0,) * len(shape))
    return pl.pallas_call(
        _lru_body,
        grid=(B,),
        in_specs=[
            pl.BlockSpec((None, LRU_NT, SEQ, LANE), lambda b: (b, XB_OFF // LRU_W, 0, 0)),
            pl.BlockSpec((None, LRU_NT, SEQ, LANE), lambda b: (b, YB_OFF // LRU_W, 0, 0)),
            full((LRU_CONV, LRU_W)), full((1, LRU_W)), full((LRU_NT, LANE, 2 * LANE)), full((LRU_NT, 1, 2 * LANE)),
            full((1, LRU_W)),
        ],
        out_specs=pl.BlockSpec((1, SEQ, LRU_W), lambda b: (b, 0, 0)),
        out_shape=jax.ShapeDtypeStruct((B, SEQ, LRU_W), bf16),
        scratch_shapes=[
            pltpu.VMEM((SEQ + LRU_PAD, LANE), f32),
            pltpu.VMEM((LRU_NT, SEQ, LANE), f32),
            pltpu.VMEM((LRU_NT, SEQ, LANE), f32),
        ],
        compiler_params=_cparams(("parallel",)),
        name="lru",
    )(proj, proj, cw, cb, w, b, lam)


def _rope_body(pos_ref, freq_ref, cos_ref, ss_ref):
    pack = pos_ref.shape[0]
    lane = lax.broadcasted_iota(i32, (SEQ, LANE), 1)
    pos = pos_ref[pack - 1]
    for m in range(pack - 2, -1, -1):
        pos = jnp.where(lane < (m + 1) * ROPE_DIMS, pos_ref[m], pos)
    ang = pos * freq_ref[...]
    cs_all = jnp.cos(ang)
    sn_all = jnp.sin(ang)
    for m in range(pack):
        shift = (LANE - m * ROPE_DIMS) % LANE
        cs = cs_all if shift == 0 else pltpu.roll(cs_all, shift, 1)
        sn = sn_all if shift == 0 else pltpu.roll(sn_all, shift, 1)
        cos_ref[m] = jnp.where(lane < ROPE_DIMS, cs, 1.0)
        ss_ref[m] = jnp.where(lane < ROPE_DIMS // 2, -sn, jnp.where(lane < ROPE_DIMS, sn, 0.0))


def _rope_tables(posf, freq):
    B = posf.shape[0]
    pack = LANE // ROPE_DIMS
    pack = pack if B % pack == 0 else 1
    spec = pl.BlockSpec((pack, SEQ, LANE), lambda b: (b, 0, 0))
    return pl.pallas_call(
        _rope_body,
        grid=(B // pack,),
        in_specs=[pl.BlockSpec((pack, SEQ, 1), lambda b: (b, 0, 0)), pl.BlockSpec((1, LANE), lambda b: (0, 0))],
        out_specs=[spec, spec],
        out_shape=[jax.ShapeDtypeStruct((B, SEQ, LANE), f32)] * 2,
        compiler_params=_cparams(("parallel",)),
        name="rope_tables",
    )(posf, freq)


def _rope_partner():
    half = ROPE_DIMS // 2
    perm = np.arange(LANE)
    perm[:half] += half
    perm[half:ROPE_DIMS] -= half
    return perm


ATTN_PC = 256
ATTN_G = 8


def _attn_body(q0, q1, q2, k0, k1, k2, v0, v1, v2, cos_ref, ss_ref, g_ref, p2_ref, o_ref,
               qd, kd, vd, tq, tk, tv, od, ld, on, ln, sc):
    q_refs, k_refs, v_refs = (q0, q1, q2), (k0, k1, k2), (v0, v1, v2)
    BLK = ATTN_BLK
    qg = g_ref[0:1, :] * (DIL_DH ** -0.5)
    qgr = g_ref[1:2, :] * (DIL_DH ** -0.5)
    kg = g_ref[2:3, :]
    kgr = g_ref[3:4, :]
    for p in range(N_PAT):
        kd[p, pl.ds(0, BLK), :] = jnp.zeros((BLK, LANE), bf16)
        vd[p, pl.ds(0, BLK), pl.ds(0, LANE)] = jnp.zeros((BLK, LANE), bf16)
        vd[p, :, pl.ds(LANE, LANE)] = jnp.ones((SEQ + BLK, LANE), bf16)

    qi = lax.broadcasted_iota(i32, (BLK, 2 * BLK), 0)
    kj = lax.broadcasted_iota(i32, (BLK, 2 * BLK), 1)
    is_cur = kj >= BLK
    band = jnp.where(is_cur, kj - BLK, qi) <= jnp.where(is_cur, qi, kj)

    def norm_rope(raw, partner, gain, gain_p, cs, ss):
        rf = raw.astype(f32)
        r = lax.rsqrt(jnp.mean(rf * rf, axis=-1, keepdims=True) + RMS_EPS)
        return (rf * (gain * cs) + partner * (gain_p * ss)) * r

    for p, (_, dil) in enumerate(DIL_PATTERNS):
        L = SEQ // dil
        nbl = L // BLK
        qd_p, kd_p, vd_p, sc_a, sc_b = qd.at[p], kd.at[p], vd.at[p], sc.at[2 * p], sc.at[2 * p + 1]
        if dil > 1:
            tq_p, tk_p, tv_p, od_p, ld_p = tq.at[p - 1], tk.at[p - 1], tv.at[p - 1], od.at[p - 1], ld.at[p - 1]

        for c in range(SEQ // ATTN_PC):
            rows = pl.ds(c * ATTN_PC, ATTN_PC)
            cs, ss = cos_ref[0, rows, :], ss_ref[0, rows, :]
            qr = q_refs[p][0, rows, :]
            kr = k_refs[p][0, rows, :]
            partner = jnp.dot(jnp.concatenate([qr, kr], axis=1), p2_ref[...], preferred_element_type=f32)
            q = norm_rope(qr, partner[:, :LANE], qg, qgr, cs, ss)
            k = norm_rope(kr, partner[:, LANE:], kg, kgr, cs, ss)
            if dil == 1:
                orow = pl.ds(c * ATTN_PC + BLK, ATTN_PC)
                qd_p[rows, :] = q.astype(bf16)
                kd_p[orow, :] = k.astype(bf16)
                vd_p[orow, pl.ds(0, LANE)] = v_refs[p][0, rows, :]
            else:
                tq_p[rows, :] = q
                tk_p[rows, :] = k
                tv_p[rows, :] = v_refs[p][0, rows, :].astype(f32)
        if dil > 1:
            for r in range(dil):
                src = pl.ds(r, L, stride=dil)
                qd_p[pl.ds(r * L, L), :] = tq_p[src, :].astype(bf16)
                kd_p[pl.ds(BLK + r * L, L), :] = tk_p[src, :].astype(bf16)
                vd_p[pl.ds(BLK + r * L, L), pl.ds(0, LANE)] = tv_p[src, :].astype(bf16)

        o_dst, l_dst = (on.at[p], ln.at[p]) if dil == 1 else (od_p, ld_p)

        def scores(g, dst, qd_p=qd_p, kd_p=kd_p):
            for j in range(ATTN_G):
                r0 = pl.multiple_of((g * ATTN_G + j) * BLK, BLK)
                dst[pl.ds(j * BLK, BLK), :] = lax.dot_general(qd_p[pl.ds(r0, BLK), :], kd_p[pl.ds(r0, 2 * BLK), :], NT,
                                                              preferred_element_type=f32)

        def finish(g, src, nbl=nbl, o_dst=o_dst, l_dst=l_dst, vd=vd_p):
            for j in range(ATTN_G):
                b = g * ATTN_G + j
                r0 = pl.multiple_of(b * BLK, BLK)
                first = (b % nbl) == 0
                valid = band & (is_cur | jnp.logical_not(first))
                s = jnp.where(valid, src[pl.ds(j * BLK, BLK), :], -jnp.inf)
                m = jnp.max(s, axis=-1, keepdims=True)
                e = jnp.exp(s - m).astype(bf16)
                ov = jnp.dot(e, vd[pl.ds(r0, 2 * BLK), :], preferred_element_type=f32)
                den = ov[:, LANE:]
                o_dst[pl.ds(r0, BLK), :] = ov[:, :LANE] / den
                l_dst[pl.ds(r0, BLK), :] = m + jnp.log(den)

        n_groups = SEQ // BLK // ATTN_G
        scores(0, sc_a)

        def pair(i, carry, scores=scores, finish=finish, sc_a=sc_a, sc_b=sc_b):
            scores(2 * i + 1, sc_b)
            finish(2 * i, sc_a)
            if n_groups > 2:
                scores(jnp.minimum(2 * i + 2, n_groups - 1), sc_a)
            finish(2 * i + 1, sc_b)
            return carry

        lax.fori_loop(0, n_groups // 2, pair, 0)
        if dil > 1:
            for r in range(dil):
                dst = pl.ds(r, L, stride=dil)
                on[p, dst, :] = od_p[pl.ds(r * L, L), :]
                ln[p, dst, :] = ld_p[pl.ds(r * L, L), :]

    def combine(c, carry):
        rows = pl.ds(pl.multiple_of(c * ATTN_PC, ATTN_PC), ATTN_PC)
        l0, l1, l2 = ln[0, rows, :], ln[1, rows, :], ln[2, rows, :]
        mx = jnp.maximum(jnp.maximum(l0, l1), l2)
        w0, w1, w2 = jnp.exp(l0 - mx), jnp.exp(l1 - mx), jnp.exp(l2 - mx)
        o = (w0 * on[0, rows, :] + w1 * on[1, rows, :] + w2 * on[2, rows, :]) / (w0 + w1 + w2)
        o_ref[0, rows, :] = o.astype(bf16)
        return carry

    lax.fori_loop(0, SEQ // ATTN_PC, combine, 0)


def _attn(proj, cos, ss, qg, kg):
    B = proj.shape[0]
    base = DIL_OFF // LANE
    perm = _rope_partner()
    gains = jnp.stack([qg, qg[perm], kg, kg[perm]]).astype(f32)
    pm = np.zeros((LANE, LANE), np.float32)
    pm[perm[:ROPE_DIMS], np.arange(ROPE_DIMS)] = 1.0
    p2 = jnp.asarray(np.kron(np.eye(2, dtype=np.float32), pm), dtype=bf16)

    def pspec(qkv, p):
        return pl.BlockSpec((None, 1, SEQ, LANE), lambda b, h, o=base + (qkv * N_PAT + p) * DIL_HEADS: (b, o + h, 0, 0))

    tspec = pl.BlockSpec((1, SEQ, LANE), lambda b, h: (b, 0, 0))
    in_specs = [pspec(qkv, p) for qkv in range(3) for p in range(N_PAT)] + [
        tspec, tspec, pl.BlockSpec((4, LANE), lambda b, h: (0, 0)), pl.BlockSpec((2 * LANE, 2 * LANE), lambda b, h: (0, 0))]
    return pl.pallas_call(
        _attn_body,
        grid=(B, DIL_HEADS),
        in_specs=in_specs,
        out_specs=pl.BlockSpec((1, SEQ, LANE), lambda b, h: (b, 0, h)),
        out_shape=jax.ShapeDtypeStruct((B, SEQ, DIL_HEADS * DIL_DH), bf16),
        scratch_shapes=[
            pltpu.VMEM((N_PAT, SEQ, LANE), bf16),
            pltpu.VMEM((N_PAT, SEQ + ATTN_BLK, LANE), bf16),
            pltpu.VMEM((N_PAT, SEQ + ATTN_BLK, 2 * LANE), bf16),
            pltpu.VMEM((N_PAT - 1, SEQ, LANE), f32),
            pltpu.VMEM((N_PAT - 1, SEQ, LANE), f32),
            pltpu.VMEM((N_PAT - 1, SEQ, LANE), f32),
            pltpu.VMEM((N_PAT - 1, SEQ, LANE), f32),
            pltpu.VMEM((N_PAT - 1, SEQ, LANE), f32),
            pltpu.VMEM((N_PAT, SEQ, LANE), f32),
            pltpu.VMEM((N_PAT, SEQ, LANE), f32),
            pltpu.VMEM((2 * N_PAT, ATTN_G * ATTN_BLK, 2 * ATTN_BLK), f32),
        ],
        compiler_params=_cparams(("parallel", "arbitrary")),
        name="dilated_attn",
    )(*([proj] * 9), cos, ss, gains, p2)


def _merge_body(a_ref, b_ref, c_ref, g0_ref, g1_ref, g2_ref, x_ref, wb_ref, wo_ref, n2_ref, rw_ref, rb_ref, tri_ref,
                x1_ref, h2_ref, ei_ref, ew_ref, pos_ref, cnt_ref, cb_ref, cnt_scr):
    merged = None
    for n, (br, g) in enumerate(((a_ref, g0_ref), (b_ref, g1_ref), (c_ref, g2_ref))):
        logits = jnp.concatenate([g[t] for t in range(D_MODEL // LANE)], axis=1)
        gate = 0.5 * jnp.tanh(0.5 * logits.astype(f32)) + 0.5
        t = gate * jnp.dot(br[...], wb_ref[n], preferred_element_type=f32)
        merged = t if merged is None else merged + t
    x1 = x_ref[...] + jnp.dot(merged.astype(bf16), wo_ref[...], preferred_element_type=f32)
    x1_ref[...] = x1
    h2 = ((x1 * lax.rsqrt(jnp.mean(x1 * x1, axis=-1, keepdims=True) + RMS_EPS)) * n2_ref[...]).astype(bf16)
    h2_ref[...] = h2

    lg = lax.dot_general(rw_ref[...], h2, NT, preferred_element_type=f32) + rb_ref[...]
    ex = jnp.exp(lg - jnp.max(lg, axis=0, keepdims=True))
    sc = ex / jnp.sum(ex, axis=0, keepdims=True)
    srow = [sc[e:e + 1, :] for e in range(N_EXP)]
    best = bidx = None
    for gidx in range(N_GRP):
        s0, s1, s2, s3 = srow[4 * gidx:4 * gidx + 4]
        hi01, lo01, hi23, lo23 = jnp.maximum(s0, s1), jnp.minimum(s0, s1), jnp.maximum(s2, s3), jnp.minimum(s2, s3)
        gs = jnp.maximum(hi01, hi23) + jnp.maximum(jnp.minimum(hi01, hi23), jnp.maximum(lo01, lo23))
        if best is None:
            best, bidx = gs, jnp.zeros_like(gs, dtype=i32)
        else:
            better = gs > best
            best = jnp.where(better, gs, best)
            bidx = jnp.where(better, gidx, bidx)
    cand = []
    for i in range(EXP_PER_GRP):
        v = srow[i]
        for gidx in range(1, N_GRP):
            v = jnp.where(bidx == gidx, srow[4 * gidx + i], v)
        cand.append(v)

    def argmax4(vals):
        bw, bi = vals[0], jnp.zeros_like(bidx)
        for i in range(1, EXP_PER_GRP):
            better = vals[i] > bw
            bw = jnp.where(better, vals[i], bw)
            bi = jnp.where(better, i, bi)
        return bw, bi

    w1, i1 = argmax4(cand)
    w2, i2 = argmax4([jnp.where(i1 == i, -jnp.inf, cand[i]) for i in range(EXP_PER_GRP)])
    tot = w1 + w2
    e1 = bidx * EXP_PER_GRP + i1
    e2 = bidx * EXP_PER_GRP + i2
    ei_ref[0:1, :] = e1
    ei_ref[1:2, :] = e2
    ew_ref[0:1, :] = w1 / tot
    ew_ref[1:2, :] = w2 / tot

    @pl.when(pl.program_id(0) == 0)
    def _():
        cnt_scr[...] = jnp.zeros_like(cnt_scr)

    erow = lax.broadcasted_iota(i32, (N_EXP, MERGE_TM), 0)
    oh1 = erow == e1
    oh2 = erow == e2
    both = jnp.where(oh1 | oh2, 1.0, 0.0)
    incl = jnp.dot(both.astype(bf16), tri_ref[...], preferred_element_type=f32)
    cb_ref[0] = cnt_scr[...]
    base = cnt_scr[...] + (incl - both)
    pos_ref[0:1, :] = jnp.sum(jnp.where(oh1, base, 0.0), axis=0, keepdims=True).astype(i32)
    pos_ref[1:2, :] = jnp.sum(jnp.where(oh2, base, 0.0), axis=0, keepdims=True).astype(i32)
    cnt = cnt_scr[...] + incl[:, MERGE_TM - 1:MERGE_TM]
    cnt_scr[...] = cnt
    cnt_ref[...] = cnt


def _merge(bra, brb, brc, proj, x2d, wb, wo, n2, rw, rb):
    T = bra.shape[0]
    TM = MERGE_TM
    per_b = SEQ // TM
    gt = D_MODEL // LANE

    def gspec(n):
        return pl.BlockSpec((None, gt, TM, LANE), lambda i: (i // per_b, GATE_OFF // D_MODEL + n, i % per_b, 0))

    rspec = lambda w: pl.BlockSpec((TM, w), lambda i: (i, 0))
    full = lambda shape: pl.BlockSpec(shape, lambda i: (0,) * len(shape))
    tri = (jnp.arange(TM)[:, None] <= jnp.arange(TM)[None, :]).astype(bf16)
    return pl.pallas_call(
        _merge_body,
        grid=(T // TM,),
        in_specs=[
            rspec(BR_W), rspec(BR_W), rspec(BR_W),
            gspec(0), gspec(1), gspec(2),
            rspec(D_MODEL),
            full((N_BR, BR_W, D_MODEL)), full((D_MODEL, D_MODEL)), full((1, D_MODEL)),
            full((N_EXP, D_MODEL)), full((N_EXP, 1)), full((TM, TM)),
        ],
        out_specs=[
            rspec(D_MODEL), rspec(D_MODEL),
            pl.BlockSpec((TOP_K, TM), lambda i: (0, i)),
            pl.BlockSpec((TOP_K, TM), lambda i: (0, i)),
            pl.BlockSpec((TOP_K, TM), lambda i: (0, i)),
            full((N_EXP, 1)),
            pl.BlockSpec((1, N_EXP, 1), lambda i: (i, 0, 0)),
        ],
        out_shape=[
            jax.ShapeDtypeStruct((T, D_MODEL), f32),
            jax.ShapeDtypeStruct((T, D_MODEL), bf16),
            jax.ShapeDtypeStruct((TOP_K, T), i32),
            jax.ShapeDtypeStruct((TOP_K, T), f32),
            jax.ShapeDtypeStruct((TOP_K, T), i32),
            jax.ShapeDtypeStruct((N_EXP, 1), f32),
            jax.ShapeDtypeStruct((T // TM, N_EXP, 1), f32),
        ],
        scratch_shapes=[pltpu.VMEM((N_EXP, 1), f32)],
        compiler_params=_cparams(("arbitrary",)),
        name="merge_router",
    )(bra, brb, brc, proj, proj, proj, x2d, wb, wo, n2, rw, rb, tri)


def _expert_body(be_ref, nu_ref, xs_ref, wgu_ref, wd_ref, ys_ref):
    i = pl.program_id(0)

    @pl.when(i < nu_ref[0])
    def _():
        gu = jnp.dot(xs_ref[...], wgu_ref[0], preferred_element_type=f32)
        g = gu[:, :EXP_FF]
        act = (g * jax.nn.sigmoid(g)) * gu[:, EXP_FF:]
        ys_ref[...] = jnp.dot(act.astype(bf16), wd_ref[0], preferred_element_type=f32).astype(bf16)

    @pl.when(i >= nu_ref[0])
    def _():
        ys_ref[...] = jnp.zeros_like(ys_ref)


def _experts(block_e, n_used, xs, wgu, wd):
    n_rows = xs.shape[0]
    grid_spec = pltpu.PrefetchScalarGridSpec(
        num_scalar_prefetch=2,
        grid=(n_rows // MOE_BM,),
        in_specs=[
            pl.BlockSpec((MOE_BM, D_MODEL), lambda i, be, nu: (jnp.minimum(i, nu[0] - 1), 0)),
            pl.BlockSpec((1, D_MODEL, 2 * EXP_FF), lambda i, be, nu: (be[i], 0, 0)),
            pl.BlockSpec((1, EXP_FF, D_MODEL), lambda i, be, nu: (be[i], 0, 0)),
        ],
        out_specs=pl.BlockSpec((MOE_BM, D_MODEL), lambda i, be, nu: (i, 0)),
    )
    return pl.pallas_call(
        _expert_body,
        grid_spec=grid_spec,
        out_shape=jax.ShapeDtypeStruct((n_rows, D_MODEL), bf16),
        compiler_params=_cparams(("arbitrary",)),
        name="experts",
    )(block_e, n_used, xs, wgu, wd)


COMB_TM = 1024


def _combine_body(x1_ref, y0_ref, y1_ref, w_ref, o_ref):
    w = w_ref[...]
    o_ref[...] = x1_ref[...] + y0_ref[...].astype(f32) * w[:, 0:1] + y1_ref[...].astype(f32) * w[:, 1:2]


def _combine(x1, y0, y1, w):
    T = x1.shape[0]
    rspec = pl.BlockSpec((COMB_TM, D_MODEL), lambda i: (i, 0))
    return pl.pallas_call(
        _combine_body,
        grid=(T // COMB_TM,),
        in_specs=[rspec, rspec, rspec, pl.BlockSpec((COMB_TM, TOP_K), lambda i: (i, 0))],
        out_specs=rspec,
        out_shape=jax.ShapeDtypeStruct((T, D_MODEL), f32),
        compiler_params=_cparams(("parallel",)),
        name="moe_combine",
    )(x1, y0, y1, w)


DISP_ALIGN = 16
DISP_SLOTS = MERGE_TM * TOP_K + N_EXP * DISP_ALIGN
DISP_SIZES = tuple(DISP_ALIGN << s for s in range(5, -1, -1))
TAIL_SIZES = tuple(sz for sz in (MOE_BM // 2 >> s for s in range(12)) if sz >= DISP_ALIGN)


def _pieces(length, sizes, fn):
    done = jnp.int32(0)
    for sz in sizes:
        take = (length & sz) != 0

        @pl.when(take)
        def _(done=done, sz=sz):
            fn(done, sz)

        done = done + jnp.where(take, sz, 0)


def _dispatch_body(off_ref, n_ref, ls_ref, toff_ref, tlen_ref, h2_ref, ei_ref, pos_ref, adj_ref, xs_hbm,
                   srt, zbuf, sem, zsem):
    i = pl.program_id(0)
    last = pl.num_programs(0) - 1
    cur = i % 2
    TM = MERGE_TM

    def slab_copies(step, buf, start):
        for e in range(N_EXP):
            ls = ls_ref[step * N_EXP + e]
            off = off_ref[step * N_EXP + e]

            def piece(done, sz, ls=ls, off=off):
                cp = pltpu.make_async_copy(srt.at[buf, pl.ds(pl.multiple_of(ls + done, DISP_ALIGN), sz)],
                                           xs_hbm.at[pl.ds(pl.multiple_of(off + done, DISP_ALIGN), sz)], sem.at[buf])
                cp.start() if start else cp.wait()

            _pieces(n_ref[step * N_EXP + e], DISP_SIZES, piece)

    def tail_copies(start):
        for e in range(N_EXP):
            off = toff_ref[e]

            def piece(done, sz, off=off):
                cp = pltpu.make_async_copy(zbuf.at[pl.ds(0, sz)],
                                           xs_hbm.at[pl.ds(pl.multiple_of(off + done, DISP_ALIGN), sz)], zsem)
                cp.start() if start else cp.wait()

            _pieces(tlen_ref[e], TAIL_SIZES, piece)

        zrows = zbuf.shape[0]

        def block(c, carry):
            cp = pltpu.make_async_copy(zbuf, xs_hbm.at[pl.ds(pl.multiple_of(toff_ref[N_EXP] + c * zrows, zrows), zrows)],
                                       zsem)
            cp.start() if start else cp.wait()
            return carry

        lax.fori_loop(0, tlen_ref[N_EXP] // zrows, block, 0)

    @pl.when(i == 0)
    def _():
        zbuf[...] = jnp.zeros_like(zbuf)
        tail_copies(True)

    @pl.when(i >= 2)
    def _():
        slab_copies(i - 2, cur, False)

    erow = lax.broadcasted_iota(i32, (N_EXP, TM), 0)
    adj = adj_ref[0]

    def slot(k):
        sel = jnp.sum(jnp.where(erow == ei_ref[k:k + 1, :], adj, 0.0), axis=0, keepdims=True)
        return pos_ref[k:k + 1, :] + sel.astype(i32)

    srow = lax.broadcasted_iota(i32, (DISP_SLOTS, TM), 0)
    perm = jnp.where((srow == slot(0)) | (srow == slot(1)), 1.0, 0.0).astype(bf16)
    srt[cur] = jnp.dot(perm, h2_ref[...], preferred_element_type=f32).astype(bf16)
    slab_copies(i, cur, True)

    @pl.when(i == last)
    def _():
        slab_copies(i, cur, False)

        @pl.when(i >= 1)
        def _():
            slab_copies(i - 1, 1 - cur, False)

        tail_copies(False)


def _dispatch(off, n16, lstart, tail_off, tail_len, h2, eidx, pos, adj, n_rows):
    T = h2.shape[0]
    TM = MERGE_TM
    grid_spec = pltpu.PrefetchScalarGridSpec(
        num_scalar_prefetch=5,
        grid=(T // TM,),
        in_specs=[
            pl.BlockSpec((TM, D_MODEL), lambda i, *_: (i, 0)),
            pl.BlockSpec((TOP_K, TM), lambda i, *_: (0, i)),
            pl.BlockSpec((TOP_K, TM), lambda i, *_: (0, i)),
            pl.BlockSpec((1, N_EXP, 1), lambda i, *_: (i, 0, 0)),
        ],
        out_specs=pl.BlockSpec(memory_space=pl.ANY),
        scratch_shapes=[
            pltpu.VMEM((2, DISP_SLOTS, D_MODEL), bf16),
            pltpu.VMEM((TAIL_SIZES[0], D_MODEL), bf16),
            pltpu.SemaphoreType.DMA((2,)),
            pltpu.SemaphoreType.DMA(()),
        ],
    )
    return pl.pallas_call(
        _dispatch_body,
        grid_spec=grid_spec,
        out_shape=jax.ShapeDtypeStruct((n_rows, D_MODEL), bf16),
        compiler_params=_cparams(("arbitrary",)),
        name="moe_dispatch",
    )(off, n16, lstart, tail_off, tail_len, h2, eidx, pos, adj)


def _moe(x1, h2, eidx, ew, pos, cnt, cb, wgu, wd):
    T = x1.shape[0]
    TM = MERGE_TM
    n_tiles = T // TM
    ceil_to = lambda v, m: (v + m - 1) // m * m
    n_rows = ceil_to(T * TOP_K + n_tiles * N_EXP * (DISP_ALIGN - 1) + N_EXP * (MOE_BM - 1), MOE_BM)
    n_blocks = n_rows // MOE_BM
    cbi = cb[:, :, 0].astype(i32)
    tile_cnt = jnp.concatenate([cbi[1:], cnt[:, 0].astype(i32)[None, :]], axis=0) - cbi
    n16 = ceil_to(tile_cnt, DISP_ALIGN)
    lstart = jnp.cumsum(n16, axis=1) - n16
    before = jnp.cumsum(n16, axis=0) - n16
    used = jnp.sum(n16, axis=0)
    padded = ceil_to(used, MOE_BM)
    pad_ends = jnp.cumsum(padded)
    off = (pad_ends - padded)[None, :] + before
    adj = (lstart - cbi).astype(f32)[:, :, None]
    row_adj = jnp.repeat(off - cbi, TM, axis=0).T
    dest = pos
    for e in range(N_EXP):
        dest = dest + jnp.where(eidx == e, row_adj[e][None, :], 0)
    blk_start = jnp.arange(n_blocks, dtype=i32) * MOE_BM
    block_e = jnp.minimum(jnp.sum((blk_start[:, None] >= pad_ends[None, :]).astype(i32), axis=1), N_EXP - 1)
    n_used = (pad_ends[-1:] // MOE_BM).astype(i32)
    tail_off = jnp.concatenate([pad_ends - padded + used, pad_ends[-1:]])
    tail_len = jnp.concatenate([padded - used, n_rows - pad_ends[-1:]])
    xs = _dispatch(off.reshape(-1), n16.reshape(-1), lstart.reshape(-1), tail_off, tail_len, h2, eidx, pos, adj, n_rows)
    ys = _experts(block_e, n_used, xs, wgu, wd)
    return _combine(x1, ys[dest[0]], ys[dest[1]], ew.T)


def _pack_w_in(w):
    cuts = np.cumsum([256, 256, 512, 512, 16, 512, 512, 4608, 3072])[:-1].tolist()
    q, k, v, og, glow, xb, yb, qkv, gates = jnp.split(w, cuts, axis=-1)
    packed = jnp.concatenate([gates, qkv, xb, yb, q, k, v, og], axis=-1).astype(bf16)
    wg = jnp.pad(glow, ((0, 0), (0, LANE - GLA_RANK))).astype(bf16)
    return packed, wg


def _block_diag_lru(w_a, w_x):
    bs = LRU_W // LRU_BLOCKS
    per = LANE // bs
    tiles = []
    for t in range(LRU_W // LANE):
        halves = []
        for w in (w_a, w_x):
            m = jnp.zeros((LANE, LANE), f32)
            for j in range(per):
                m = m.at[j * bs:(j + 1) * bs, j * bs:(j + 1) * bs].set(w[t * per + j])
            halves.append(m)
        tiles.append(jnp.concatenate(halves, axis=1))
    return jnp.stack(tiles).astype(bf16)


def kernel(x, positions, norm1_gain, w_in, gla_gate_up, gla_gate_bias, gla_out_gain, lru_conv_w, lru_conv_b, lru_w_a, lru_b_a, lru_w_x, lru_b_x, lru_lambda, q_norm_gain, k_norm_gain, w_branch, w_out, norm2_gain, router_w, router_b, w_gate, w_up, w_down):
    B, S, D = x.shape
    assert S == SEQ and D == D_MODEL
    T = B * S
    depth = w_in.shape[0]

    half = ROPE_DIMS // 2
    inv_freq = ROPE_THETA ** (-jnp.arange(half, dtype=f32) / half)
    freq = jnp.tile(jnp.concatenate([inv_freq, inv_freq]), LANE // ROPE_DIMS)[None, :]
    cos, ss = _rope_tables(positions.astype(f32)[:, :, None], freq)

    rw = router_w.T.astype(bf16)
    rb = router_b.astype(f32)[:, None]
    for l in range(depth):
        w_packed, wg = _pack_w_in(w_in[l])
        wup = jnp.pad(gla_gate_up[l], ((0, LANE - GLA_RANK), (0, 0)))
        nt = LRU_W // LANE
        lru_w = _block_diag_lru(lru_w_a[l], lru_w_x[l])
        lru_b = jnp.concatenate([lru_b_a[l].reshape(nt, 1, LANE), lru_b_x[l].reshape(nt, 1, LANE)], axis=-1)
        wgu = jnp.concatenate([w_gate[l], w_up[l]], axis=-1).astype(bf16)

        proj, glow = _inproj(x, norm1_gain[l][None, :], w_packed, wg)
        br_a = _gla(proj, glow, wup, gla_gate_bias[l][None, :], gla_out_gain[l][None, :])
        br_b = _lru(proj, lru_conv_w[l], lru_conv_b[l][None, :], lru_w, lru_b, lru_lambda[l][None, :])
        br_c = _attn(proj, cos, ss, q_norm_gain[l], k_norm_gain[l])
        routed = _merge(br_a.reshape(T, BR_W), br_b.reshape(T, BR_W), br_c.reshape(T, BR_W), proj, x.reshape(T, D),
                        w_branch[l].astype(bf16), w_out[l].astype(bf16), norm2_gain[l][None, :], rw, rb)
        x = _moe(*routed, wgu, w_down[l].astype(bf16)).reshape(B, S, D)
    return x
```
